```python
import math
import jax, jax.numpy as jnp
from jax import lax
import numpy as np

D_MODEL = 1024
BATCH = 4
SEQ = 8192
DEPTH = 1

EPS = 1e-6
ROPE_THETA = 10000.0
NEG = -1e30
M_HEADS = 4
M_DQK = 64
M_DV = 128
M_CHUNK = 64
M_CONV = 4
M_QK = M_HEADS * M_DQK
M_V = M_HEADS * M_DV
N_HEADS = 8
N_KV = 2
N_HG = N_HEADS // N_KV
N_DH = 64
N_Q = N_HEADS * N_DH
N_KVW = N_KV * N_DH
CMP_LEN = 32
CMP_STRIDE = 16
CMP_HIDDEN = 256
SEL_BLOCK = 64
SEL_TOPN = 16
WINDOW = 512
Q_BLOCK = 128
FORCE_BONUS = 1e4
D_FF = 2816
FFN_CONV = 3
IN_SIZES = (M_QK, M_QK, M_V, M_V, M_HEADS, M_HEADS,
            N_Q, N_KVW, N_KVW, N_KVW, N_KVW, N_KVW, N_KVW, 3 * N_HEADS,
            D_MODEL, D_MODEL)
D_IN = sum(IN_SIZES)

kernel_name = "hybrid_mlstm_nsa_convffn"


def rmsnorm(x, w):
    xf = x.astype(jnp.float32)
    y = xf * lax.rsqrt(jnp.mean(xf * xf, axis=-1, keepdims=True) + EPS)
    return (y * w.astype(jnp.float32)).astype(x.dtype)


def causal_dwconv(x, w, b):
    K = w.shape[0]
    S = x.shape[1]
    xp = jnp.pad(x, ((0, 0), (K - 1, 0), (0, 0)))
    y = xp[:, 0:S] * w[0]
    for k in range(1, K):
        y = y + xp[:, k:k + S] * w[k]
    return y + b


def rope_tables(S):
    pos = jnp.arange(S, dtype=jnp.float32)
    inv = ROPE_THETA ** (-jnp.arange(0, N_DH, 2, dtype=jnp.float32) / N_DH)
    ang = pos[:, None] * inv[None, :]
    return jnp.cos(ang), jnp.sin(ang)


def apply_rope(x, cos, sin):
    x1, x2 = jnp.split(x.astype(jnp.float32), 2, axis=-1)
    c = cos[None, :, None, :]
    s = sin[None, :, None, :]
    return jnp.concatenate([x1 * c - x2 * s, x2 * c + x1 * s], axis=-1).astype(x.dtype)


def mlstm_chunkwise(q, k, v, i_pre, logf):
    B, NH, S, DQK = q.shape
    DV = v.shape[-1]
    L = M_CHUNK
    NC = S // L
    q = q.reshape(B, NH, NC, L, DQK) * (DQK ** -0.5)
    k = k.reshape(B, NH, NC, L, DQK)
    v = v.reshape(B, NH, NC, L, DV)
    i_pre = i_pre.reshape(B, NH, NC, L)
    logf = logf.reshape(B, NH, NC, L)
    b = jnp.cumsum(logf, axis=-1)
    g = b[..., -1]
    a = g[..., None] - b + i_pre
    a_max = jnp.max(a, axis=-1)
    w = jnp.exp(a - a_max[..., None])
    dC = jnp.einsum('bhcl,bhcld,bhcle->bhcde', w, k, v)
    dn = jnp.einsum('bhcl,bhcld->bhcd', w, k)

    def step(carry, inp):
        C, n, m = carry
        dC_c, dn_c, g_c, am_c = inp
        m_new = jnp.maximum(g_c + m, am_c)
        decay = jnp.exp(g_c + m - m_new)
        scl = jnp.exp(am_c - m_new)
        C_new = decay[..., None, None] * C + scl[..., None, None] * dC_c
        n_new = decay[..., None] * n + scl[..., None] * dn_c
        return (C_new, n_new, m_new), (C, n, m)

    init = (jnp.zeros((B, NH, DQK, DV), jnp.float32),
            jnp.zeros((B, NH, DQK), jnp.float32),
            jnp.zeros((B, NH), jnp.float32))
    xs = (jnp.moveaxis(dC, 2, 0), jnp.moveaxis(dn, 2, 0),
          jnp.moveaxis(g, 2, 0), jnp.moveaxis(a_max, 2, 0))
    _, (C_prev, n_prev, m_prev) = lax.scan(step, init, xs)
    C_prev = jnp.moveaxis(C_prev, 0, 2)
    n_prev = jnp.moveaxis(n_prev, 0, 2)
    m_prev = jnp.moveaxis(m_prev, 0, 2)

    causal = jnp.tril(jnp.ones((L, L), dtype=bool))
    D = b[..., :, None] - b[..., None, :] + i_pre[..., None, :]
    D = jnp.where(causal, D, NEG)
    inter = b + m_prev[..., None]
    m_t = jnp.maximum(inter, jnp.max(D, axis=-1))
    Dw = jnp.exp(D - m_t[..., None])
    s = jnp.einsum('bhctd,bhcsd->bhcts', q, k) * Dw
    isc = jnp.exp(inter - m_t)
    num = isc[..., None] * jnp.einsum('bhctd,bhcde->bhcte', q, C_prev) \
        + jnp.einsum('bhcts,bhcse->bhcte', s, v)
    den = isc * jnp.einsum('bhctd,bhcd->bhct', q, n_prev) + jnp.sum(s, axis=-1)
    h = num / jnp.maximum(jnp.abs(den), jnp.exp(-m_t))[..., None]
    return h.reshape(B, NH, S, DV)


def compress_blocks(kv, pe, w1, w2):
    B, G, S, dh = kv.shape
    ncmp = (S - CMP_LEN) // CMP_STRIDE + 1
    idx = (jnp.arange(ncmp) * CMP_STRIDE)[:, None] + jnp.arange(CMP_LEN)[None, :]
    blk = kv[:, :, idx] + pe
    flat = blk.reshape(B, G, ncmp, CMP_LEN * dh)
    return jax.nn.gelu(flat @ w1, approximate=False) @ w2


def masked_softmax(s, mask):
    s = jnp.where(mask, s.astype(jnp.float32), NEG)
    p = jax.nn.softmax(s, axis=-1)
    return jnp.where(mask, p, 0.0)


def nsa_branches(q, kc, vc, ks, vs, kw, vw):
    B, G, HG, S, dh = q.shape
    nqb = S // Q_BLOCK
    nsel = S // SEL_BLOCK
    topn = min(SEL_TOPN, nsel)
    ncmp = kc.shape[2]
    scale = dh ** -0.5
    cstart = jnp.arange(ncmp) * CMP_STRIDE
    cend = cstart + CMP_LEN - 1
    sstart = jnp.arange(nsel) * SEL_BLOCK
    send = sstart + SEL_BLOCK - 1
    overlap = ((cstart[:, None] <= send[None, :]) & (cend[:, None] >= sstart[None, :])).astype(jnp.float32)
    ksb = ks.reshape(B, G, nsel, SEL_BLOCK, dh)
    vsb = vs.reshape(B, G, nsel, SEL_BLOCK, dh)
    kwp = jnp.pad(kw, ((0, 0), (0, 0), (WINDOW, 0), (0, 0)))
    vwp = jnp.pad(vw, ((0, 0), (0, 0), (WINDOW, 0), (0, 0)))
    qblocks = jnp.moveaxis(q.reshape(B, G, HG, nqb, Q_BLOCK, dh), 3, 0)
    gather = jax.vmap(jax.vmap(lambda blocks, ix: blocks[ix]))
    jsel = jnp.arange(nsel)
    win_off = jnp.arange(Q_BLOCK + WINDOW)
    sb_off = jnp.arange(SEL_BLOCK)

    def one_block(args):
        qi, qb = args
        s0 = qi * Q_BLOCK
        tpos = s0 + jnp.arange(Q_BLOCK)
        cmask = cend[None, :] <= tpos[:, None]
        pc = masked_softmax(jnp.einsum('bghqd,bgnd->bghqn', qb, kc) * scale, cmask)
        o_cmp = jnp.einsum('bghqn,bgnd->bghqd', pc.astype(vc.dtype), vc)
        imp = jnp.einsum('bghqn,nj->bgqj', pc, overlap)
        cur = tpos // SEL_BLOCK
        forced = (jsel[None, :] == 0) | (jsel[None, :] == cur[:, None]) | (jsel[None, :] == cur[:, None] - 1)
        imp = jnp.where(forced, FORCE_BONUS, imp)
        imp = jnp.where(jsel[None, :] > cur[:, None], NEG, imp)
        _, idx = lax.top_k(imp, topn)
        ksel = gather(ksb, idx).reshape(B, G, Q_BLOCK, topn * SEL_BLOCK, dh)
        vsel = gather(vsb, idx).reshape(B, G, Q_BLOCK, topn * SEL_BLOCK, dh)
        spos = (idx[..., None] * SEL_BLOCK + sb_off).reshape(B, G, Q_BLOCK, topn * SEL_BLOCK)
        smask = (spos <= tpos[None, None, :, None])[:, :, None]
        ps = masked_softmax(jnp.einsum('bghqd,bgqmd->bghqm', qb, ksel) * scale, smask)
        o_slc = jnp.einsum('bghqm,bgqmd->bghqd', ps.astype(vsel.dtype), vsel)
        kwin = lax.dynamic_slice_in_dim(kwp, s0, Q_BLOCK + WINDOW, axis=2)
        vwin = lax.dynamic_slice_in_dim(vwp, s0, Q_BLOCK + WINDOW, axis=2)
        kpos = s0 - WINDOW + win_off
        wmask = (kpos[None, :] <= tpos[:, None]) & (kpos[None, :] > tpos[:, None] - WINDOW) & (kpos[None, :] >= 0)
        pw = masked_softmax(jnp.einsum('bghqd,bgkd->bghqk', qb, kwin) * scale, wmask)
        o_win = jnp.einsum('bghqk,bgkd->bghqd', pw.astype(vwin.dtype), vwin)
        return o_cmp, o_slc, o_win

    o_cmp, o_slc, o_win = lax.map(one_block, (jnp.arange(nqb), qblocks))
    unblock = lambda o: jnp.moveaxis(o, 0, 3).reshape(B, G, HG, S, dh)
    return unblock(o_cmp), unblock(o_slc), unblock(o_win)


def hybrid_layer(x, cos, sin, norm1_w, w_in, m_conv_w, m_conv_b, m_igate_b, m_fgate_b, m_out_norm_w,
                 q_norm_w, kcmp_norm_w, kslc_norm_w, kwin_norm_w,
                 cmp_k_pe, cmp_k_w1, cmp_k_w2, cmp_v_pe, cmp_v_w1, cmp_v_w2,
                 w_up_m, w_up_n, merge_gate_b, w_out,
                 norm2_w, ffn_w_up, ffn_conv_w, ffn_conv_b, ffn_w_down):
    B, S, _ = x.shape
    h = rmsnorm(x, norm1_w)
    proj = h @ w_in
    split_pts = np.cumsum(IN_SIZES)[:-1].tolist()
    (mq, mk, mv, mo, mi, mf, nq, kc, vc, ks, vs, kw, vw, ng, gm, gn) = jnp.split(proj, split_pts, axis=-1)

    qk = jax.nn.silu(causal_dwconv(jnp.concatenate([mq, mk], axis=-1), m_conv_w, m_conv_b))
    mq, mk = jnp.split(qk, 2, axis=-1)
    to_mheads = lambda t, d: t.reshape(B, S, M_HEADS, d).transpose(0, 2, 1, 3).astype(jnp.float32)
    i_pre = (mi + m_igate_b).astype(jnp.float32).transpose(0, 2, 1)
    logf = jax.nn.log_sigmoid((mf + m_fgate_b).astype(jnp.float32)).transpose(0, 2, 1)
    hm = mlstm_chunkwise(to_mheads(mq, M_DQK), to_mheads(mk, M_DQK), to_mheads(mv, M_DV), i_pre, logf)
    hm = rmsnorm(hm.transpose(0, 2, 1, 3), m_out_norm_w)
    hm = hm.reshape(B, S, M_V).astype(x.dtype) * jax.nn.sigmoid(mo)

    qn = apply_rope(rmsnorm(nq.reshape(B, S, N_HEADS, N_DH), q_norm_w), cos, sin)
    kv_heads = lambda t: t.reshape(B, S, N_KV, N_DH)
    to_groups = lambda t: t.transpose(0, 2, 1, 3)
    kc_t = to_groups(apply_rope(rmsnorm(kv_heads(kc), kcmp_norm_w), cos, sin))
    ks_t = to_groups(apply_rope(rmsnorm(kv_heads(ks), kslc_norm_w), cos, sin))
    kw_t = to_groups(apply_rope(rmsnorm(kv_heads(kw), kwin_norm_w), cos, sin))
    vc_t = to_groups(kv_heads(vc))
    vs_t = to_groups(kv_heads(vs))
    vw_t = to_groups(kv_heads(vw))
    qg = qn.reshape(B, S, N_KV, N_HG, N_DH).transpose(0, 2, 3, 1, 4)
    kcmp = compress_blocks(kc_t, cmp_k_pe, cmp_k_w1, cmp_k_w2)
    vcmp = compress_blocks(vc_t, cmp_v_pe, cmp_v_w1, cmp_v_w2)
    o_cmp, o_slc, o_win = nsa_branches(qg, kcmp, vcmp, ks_t, vs_t, kw_t, vw_t)
    gates = jax.nn.sigmoid(ng.reshape(B, S, N_KV, N_HG, 3).transpose(0, 2, 3, 1, 4))
    on = gates[..., 0:1] * o_cmp + gates[..., 1:2] * o_slc + gates[..., 2:3] * o_win
    on = on.transpose(0, 3, 1, 2, 4).reshape(B, S, N_Q)

    y = jax.nn.sigmoid(gm + merge_gate_b[0]) * (hm @ w_up_m) \
        + jax.nn.sigmoid(gn + merge_gate_b[1]) * (on @ w_up_n)
    x = x + y @ w_out

    h2 = rmsnorm(x, norm2_w)
    a, v = jnp.split(h2 @ ffn_w_up, 2, axis=-1)
    a = jax.nn.gelu(causal_dwconv(a, ffn_conv_w, ffn_conv_b), approximate=False)
    return x + (a * v) @ ffn_w_down


def setup_inputs(seed: int = 0) -> dict:
    key = jax.random.key(seed)
    ks = jax.random.split(key, 32)
    f32 = jnp.float32
    nrm = lambda k, shape, s: jax.random.normal(k, shape, f32) * s
    gain = lambda k, shape: 1.0 + 0.02 * jax.random.normal(k, shape, f32)
    Ld = DEPTH
    return {
        "x": nrm(ks[0], (BATCH, SEQ, D_MODEL), 1.0),
        "norm1_w": gain(ks[1], (Ld, D_MODEL)),
        "w_in": nrm(ks[2], (Ld, D_MODEL, D_IN), D_MODEL ** -0.5),
        "m_conv_w": nrm(ks[3], (Ld, M_CONV, 2 * M_QK), M_CONV ** -0.5),
        "m_conv_b": nrm(ks[4], (Ld, 2 * M_QK), 0.01),
        "m_igate_b": nrm(ks[5], (Ld, M_HEADS), 0.1),
        "m_fgate_b": jnp.linspace(3.0, 6.0, M_HEADS, dtype=f32)[None, :] + nrm(ks[6], (Ld, M_HEADS), 0.01),
        "m_out_norm_w": gain(ks[7], (Ld, M_HEADS, M_DV)),
        "q_norm_w": gain(ks[8], (Ld, N_DH)),
        "kcmp_norm_w": gain(ks[9], (Ld, N_DH)),
        "kslc_norm_w": gain(ks[10], (Ld, N_DH)),
        "kwin_norm_w": gain(ks[11], (Ld, N_DH)),
        "cmp_k_pe": nrm(ks[12], (Ld, CMP_LEN, N_DH), 0.02),
        "cmp_k_w1": nrm(ks[13], (Ld, CMP_LEN * N_DH, CMP_HIDDEN), (CMP_LEN * N_DH) ** -0.5),
        "cmp_k_w2": nrm(ks[14], (Ld, CMP_HIDDEN, N_DH), CMP_HIDDEN ** -0.5),
        "cmp_v_pe": nrm(ks[15], (Ld, CMP_LEN, N_DH), 0.02),
        "cmp_v_w1": nrm(ks[16], (Ld, CMP_LEN * N_DH, CMP_HIDDEN), (CMP_LEN * N_DH) ** -0.5),
        "cmp_v_w2": nrm(ks[17], (Ld, CMP_HIDDEN, N_DH), CMP_HIDDEN ** -0.5),
        "w_up_m": nrm(ks[18], (Ld, M_V, D_MODEL), M_V ** -0.5),
        "w_up_n": nrm(ks[19], (Ld, N_Q, D_MODEL), N_Q ** -0.5),
        "merge_gate_b": nrm(ks[20], (Ld, 2, D_MODEL), 0.01),
        "w_out": nrm(ks[21], (Ld, D_MODEL, D_MODEL), D_MODEL ** -0.5),
        "norm2_w": gain(ks[22], (Ld, D_MODEL)),
        "ffn_w_up": nrm(ks[23], (Ld, D_MODEL, 2 * D_FF), D_MODEL ** -0.5),
        "ffn_conv_w": nrm(ks[24], (Ld, FFN_CONV, D_FF), FFN_CONV ** -0.5),
        "ffn_conv_b": nrm(ks[25], (Ld, D_FF), 0.01),
        "ffn_w_down": nrm(ks[26], (Ld, D_FF, D_MODEL), D_FF ** -0.5),
    }


def reference(x, norm1_w, w_in, m_conv_w, m_conv_b, m_igate_b, m_fgate_b, m_out_norm_w,
              q_norm_w, kcmp_norm_w, kslc_norm_w, kwin_norm_w,
              cmp_k_pe, cmp_k_w1, cmp_k_w2, cmp_v_pe, cmp_v_w1, cmp_v_w2,
              w_up_m, w_up_n, merge_gate_b, w_out,
              norm2_w, ffn_w_up, ffn_conv_w, ffn_conv_b, ffn_w_down):
    cos, sin = rope_tables(x.shape[1])
    layer_params = (norm1_w, w_in, m_conv_w, m_conv_b, m_igate_b, m_fgate_b, m_out_norm_w,
                    q_norm_w, kcmp_norm_w, kslc_norm_w, kwin_norm_w,
                    cmp_k_pe, cmp_k_w1, cmp_k_w2, cmp_v_pe, cmp_v_w1, cmp_v_w2,
                    w_up_m, w_up_n, merge_gate_b, w_out,
                    norm2_w, ffn_w_up, ffn_conv_w, ffn_conv_b, ffn_w_down)
    for layer in range(DEPTH):
        x = hybrid_layer(x, cos, sin, *[p[layer] for p in layer_params])
    return x
```

```python
import functools
import math

import jax
import jax.numpy as jnp
import numpy as np
from jax import lax
from jax.experimental import pallas as pl
from jax.experimental.pallas import tpu as pltpu

D_MODEL = 1024
EPS = 1e-6
ROPE_THETA = 10000.0
NEG = -1e30
M_HEADS = 4
M_DQK = 64
M_DV = 128
M_CONV = 4
M_QK = M_HEADS * M_DQK
M_V = M_HEADS * M_DV
N_HEADS = 8
N_KV = 2
N_HG = N_HEADS // N_KV
N_DH = 64
N_Q = N_HEADS * N_DH
N_KVW = N_KV * N_DH
CMP_LEN = 32
CMP_STRIDE = 16
CMP_HIDDEN = 256
SEL_BLOCK = 64
SEL_TOPN = 16
WINDOW = 512
D_FF = 2816
FFN_CONV = 3

_OFF = {}
_o = 0
for _name, _size in (("mq", M_QK), ("mk", M_QK), ("mv", M_V), ("mo", M_V), ("mi", M_HEADS), ("mf", M_HEADS),
                     ("nq", N_Q), ("kc", N_KVW), ("vc", N_KVW), ("ks", N_KVW), ("vs", N_KVW), ("kw", N_KVW),
                     ("vw", N_KVW), ("ng", 3 * N_HEADS), ("gm", D_MODEL), ("gn", D_MODEL)):
    _OFF[_name] = (_o, _o + _size)
    _o += _size

LANES = 128
MLSTM_CHUNK = 128
Q_TILE = 128
KEY_TILE = 256
WIN_SPAN = 768
VMEM_LIMIT = 56 * 1024 * 1024

F32 = jnp.float32
BF16 = jnp.bfloat16
HIGHEST = lax.Precision.HIGHEST


def _dot(a, b):
    return jnp.dot(a, b, preferred_element_type=F32)


def _rmsnorm_rows(x, w):
    return x * lax.rsqrt(jnp.mean(x * x, axis=-1, keepdims=True) + EPS) * w


def _sigmoid(x):
    return 1.0 / (1.0 + jnp.exp(-x))


def _gelu(x):
    return 0.5 * x * (1.0 + lax.erf(x * (1.0 / math.sqrt(2.0))))


def _params(sem):
    return pltpu.CompilerParams(dimension_semantics=sem, vmem_limit_bytes=VMEM_LIMIT)


def _proj_m_kernel(x_ref, n1w_ref, w_ref, cw_ref, cb_ref, sb_ref,
                   q_ref, kT_ref, v_ref, og_ref, small_ref, smallT_ref, buf_ref, *, tiles_per_seq):
    i = pl.program_id(0)
    tm = x_ref.shape[0]
    hn = _rmsnorm_rows(x_ref[...], n1w_ref[...]).astype(BF16)

    @pl.when(i % tiles_per_seq == 0)
    def _():
        buf_ref[0:8, :] = jnp.zeros((8, 2 * M_QK), F32)

    qk = _dot(hn, w_ref[:, 0:2 * M_QK])
    buf_ref[8:8 + tm, :] = qk
    acc = cb_ref[...] + cw_ref[M_CONV - 1:M_CONV, :] * qk
    for k in range(M_CONV - 1):
        acc = acc + cw_ref[k:k + 1, :] * buf_ref[8 - (M_CONV - 1) + k:8 - (M_CONV - 1) + k + tm, :]
    buf_ref[0:8, :] = buf_ref[tm:tm + 8, :]
    act = acc * _sigmoid(acc)
    q_ref[...] = (act[:, 0:M_QK] * (M_DQK ** -0.5)).astype(BF16)
    kT_ref[...] = act[:, M_QK:2 * M_QK].T.astype(BF16)
    v_ref[...] = _dot(hn, w_ref[:, 2 * M_QK:2 * M_QK + M_V]).astype(BF16)
    og_ref[...] = _sigmoid(_dot(hn, w_ref[:, 2 * M_QK + M_V:2 * M_QK + 2 * M_V]))
    sm = _dot(hn, w_ref[:, 2 * M_QK + 2 * M_V:]) + sb_ref[...]
    lane = lax.broadcasted_iota(jnp.int32, sm.shape, 1)
    logsig = jnp.minimum(sm, 0.0) - jnp.log1p(jnp.exp(-jnp.abs(sm)))
    sm = jnp.where(lane < M_HEADS, sm, jnp.where(lane < 2 * M_HEADS, logsig, _sigmoid(sm)))
    small_ref[...] = sm
    smallT_ref[...] = sm.T[0:32, :]


def _proj_m(x2, n1w, w, cw, cb, sb, seq, tm):
    t = x2.shape[0]
    ncol = w.shape[1]
    kern = functools.partial(_proj_m_kernel, tiles_per_seq=seq // tm)
    return pl.pallas_call(
        kern,
        grid=(t // tm,),
        in_specs=[
            pl.BlockSpec((tm, D_MODEL), lambda i: (i, 0)),
            pl.BlockSpec((1, D_MODEL), lambda i: (0, 0)),
            pl.BlockSpec((D_MODEL, ncol), lambda i: (0, 0)),
            pl.BlockSpec((M_CONV, 2 * M_QK), lambda i: (0, 0)),
            pl.BlockSpec((1, 2 * M_QK), lambda i: (0, 0)),
            pl.BlockSpec((1, LANES), lambda i: (0, 0)),
        ],
        out_specs=[
            pl.BlockSpec((tm, M_QK), lambda i: (i, 0)),
            pl.BlockSpec((M_QK, tm), lambda i: (0, i)),
            pl.BlockSpec((tm, M_V), lambda i: (i, 0)),
            pl.BlockSpec((tm, M_V), lambda i: (i, 0)),
            pl.BlockSpec((tm, LANES), lambda i: (i, 0)),
            pl.BlockSpec((32, tm), lambda i: (0, i)),
        ],
        out_shape=[
            jax.ShapeDtypeStruct((t, M_QK), BF16),
            jax.ShapeDtypeStruct((M_QK, t), BF16),
            jax.ShapeDtypeStruct((t, M_V), BF16),
            jax.ShapeDtypeStruct((t, M_V), F32),
            jax.ShapeDtypeStruct((t, LANES), F32),
            jax.ShapeDtypeStruct((32, t), F32),
        ],
        scratch_shapes=[pltpu.VMEM((tm + 8, 2 * M_QK), F32)],
        compiler_params=_params(("arbitrary",)),
        name="proj_m",
    )(x2, n1w, w, cw, cb, sb)


def _proj_n_kernel(x_ref, n1w_ref, w_ref, qnw_ref, knw_ref, cosn_ref, sinn_ref, cosT_ref, sinT_ref,
                   qT_ref, kc_ref, vc_ref, ks_ref, kw_ref, vsT_ref, vwT_ref):
    tm = x_ref.shape[0]
    hn = _rmsnorm_rows(x_ref[...], n1w_ref[...]).astype(BF16)
    qT = _dot(hn, w_ref[:, 0:N_Q]).T
    cosT = cosT_ref[...]
    sinT = sinT_ref[...]
    qnw = qnw_ref[...]
    half = N_DH // 2
    for h in range(N_HEADS):
        xh = qT[h * N_DH:(h + 1) * N_DH, :]
        xn = xh * lax.rsqrt(jnp.mean(xh * xh, axis=0, keepdims=True) + EPS) * qnw
        x1 = xn[0:half, :]
        x2 = xn[half:, :]
        o = jnp.concatenate([x1 * cosT - x2 * sinT, x2 * cosT + x1 * sinT], axis=0) * (N_DH ** -0.5)
        qT_ref[h * N_DH:(h + 1) * N_DH, :] = o.astype(BF16)

    lane = lax.broadcasted_iota(jnp.int32, (tm, N_KVW), 1)
    first_head = lane < N_DH
    first_half = (lane % N_DH) < half
    cosn = cosn_ref[...]
    sinn = sinn_ref[...]

    def knorm_rope(col, row):
        k = _dot(hn, w_ref[:, col:col + N_KVW])
        k2 = k * k
        s0 = jnp.sum(k2[:, 0:N_DH], axis=1, keepdims=True) * (1.0 / N_DH)
        s1 = jnp.sum(k2[:, N_DH:], axis=1, keepdims=True) * (1.0 / N_DH)
        r = jnp.where(first_head, lax.rsqrt(s0 + EPS), lax.rsqrt(s1 + EPS))
        kn = k * r * knw_ref[row:row + 1, :]
        rot = jnp.where(first_half, pltpu.roll(kn, N_KVW - half, 1), pltpu.roll(kn, half, 1))
        return kn * cosn + rot * sinn

    c0 = N_Q
    kc_ref[...] = knorm_rope(c0, 0)
    vc_ref[...] = _dot(hn, w_ref[:, c0 + N_KVW:c0 + 2 * N_KVW])
    ks_ref[...] = knorm_rope(c0 + 2 * N_KVW, 1).astype(BF16)
    vsT = _dot(hn, w_ref[:, c0 + 3 * N_KVW:c0 + 4 * N_KVW]).T.astype(BF16)
    kw_ref[...] = knorm_rope(c0 + 4 * N_KVW, 2).astype(BF16)
    vwT = _dot(hn, w_ref[:, c0 + 5 * N_KVW:c0 + 6 * N_KVW]).T.astype(BF16)
    for j in range(tm // LANES):
        vsT_ref[j] = vsT[:, j * LANES:(j + 1) * LANES]
        vwT_ref[j] = vwT[:, j * LANES:(j + 1) * LANES]


def _proj_n(x2, n1w, w, qnw, knw, cosn, sinn, cosT, sinT, seq, tm):
    t = x2.shape[0]
    ncol = w.shape[1]
    tps = seq // tm
    half = N_DH // 2
    return pl.pallas_call(
        _proj_n_kernel,
        grid=(t // tm,),
        in_specs=[
            pl.BlockSpec((tm, D_MODEL), lambda i: (i, 0)),
            pl.BlockSpec((1, D_MODEL), lambda i: (0, 0)),
            pl.BlockSpec((D_MODEL, ncol), lambda i: (0, 0)),
            pl.BlockSpec((N_DH, 1), lambda i: (0, 0)),
            pl.BlockSpec((8, N_KVW), lambda i: (0, 0)),
            pl.BlockSpec((tm, N_KVW), lambda i: (i % tps, 0)),
            pl.BlockSpec((tm, N_KVW), lambda i: (i % tps, 0)),
            pl.BlockSpec((half, tm), lambda i: (0, i % tps)),
            pl.BlockSpec((half, tm), lambda i: (0, i % tps)),
        ],
        out_specs=[
            pl.BlockSpec((N_Q, tm), lambda i: (0, i)),
            pl.BlockSpec((tm, N_KVW), lambda i: (i, 0)),
            pl.BlockSpec((tm, N_KVW), lambda i: (i, 0)),
            pl.BlockSpec((tm, N_KVW), lambda i: (i, 0)),
            pl.BlockSpec((tm, N_KVW), lambda i: (i, 0)),
            pl.BlockSpec((tm // LANES, N_KVW, LANES), lambda i: (i, 0, 0)),
            pl.BlockSpec((tm // LANES, N_KVW, LANES), lambda i: (i, 0, 0)),
        ],
        out_shape=[
            jax.ShapeDtypeStruct((N_Q, t), BF16),
            jax.ShapeDtypeStruct((t, N_KVW), F32),
            jax.ShapeDtypeStruct((t, N_KVW), F32),
            jax.ShapeDtypeStruct((t, N_KVW), BF16),
            jax.ShapeDtypeStruct((t, N_KVW), BF16),
            jax.ShapeDtypeStruct((t // LANES, N_KVW, LANES), BF16),
            jax.ShapeDtypeStruct((t // LANES, N_KVW, LANES), BF16),
        ],
        compiler_params=_params(("arbitrary",)),
        name="proj_n",
    )(x2, n1w, w, qnw, knw, cosn, sinn, cosT, sinT)


def _compress_kernel(x_ref, pe_ref, w1_ref, w2_ref, o_ref):
    nb = o_ref.shape[2]
    half = (CMP_LEN // 2) * N_DH
    pe = pe_ref[0]
    xa = (x_ref[0, 0, 0:nb, :] + pe[:, 0:half]).astype(BF16)
    xb = (x_ref[0, 0, pl.ds(1, nb), :] + pe[:, half:]).astype(BF16)
    hid = _dot(xa, w1_ref[0, 0:half, :]) + _dot(xb, w1_ref[0, half:, :])
    o_ref[0, 0] = _dot(_gelu(hid).astype(BF16), w2_ref[0])


def _compress(xkv, pe, w1, w2):
    _, bg, nbp, width = xkv.shape
    nb = nbp - 8
    return pl.pallas_call(
        _compress_kernel,
        grid=(2, bg),
        in_specs=[
            pl.BlockSpec((1, 1, nbp, width), lambda a, b: (a, b, 0, 0)),
            pl.BlockSpec((1, 1, CMP_LEN * N_DH), lambda a, b: (a, 0, 0)),
            pl.BlockSpec((1, CMP_LEN * N_DH, CMP_HIDDEN), lambda a, b: (a, 0, 0)),
            pl.BlockSpec((1, CMP_HIDDEN, N_DH), lambda a, b: (a, 0, 0)),
        ],
        out_specs=pl.BlockSpec((1, 1, nb, N_DH), lambda a, b: (a, b, 0, 0)),
        out_shape=jax.ShapeDtypeStruct((2, bg, nb, N_DH), F32),
        compiler_params=_params(("arbitrary", "arbitrary")),
        name="compress",
    )(xkv, pe, w1, w2)


def _mlstm_kernel(q_ref, kT_ref, v_ref, small_ref, smallT_ref, og_ref, onw_ref, o_ref, c_ref, m_ref):
    L = MLSTM_CHUNK
    tb = q_ref.shape[0]

    @pl.when(pl.program_id(1) == 0)
    def _():
        c_ref[...] = jnp.zeros(c_ref.shape, F32)
        m_ref[...] = jnp.zeros(m_ref.shape, F32)

    row = lax.broadcasted_iota(jnp.int32, (L, L), 0)
    col = lax.broadcasted_iota(jnp.int32, (L, L), 1)
    causal = col <= row
    tril = causal.astype(F32)
    triu = (row <= col).astype(F32)
    ones_col = (col == 0).astype(BF16)
    zeros_c = jnp.zeros((M_DQK, 2 * M_DV), BF16)

    for c in range(tb // L):
        sl = slice(c * L, (c + 1) * L)
        sm = small_ref[sl, :]
        smT = smallT_ref[0:8, sl]
        bcol_all = jnp.dot(tril, sm, preferred_element_type=F32, precision=HIGHEST)
        brow_all = jnp.dot(smT, triu, preferred_element_type=F32, precision=HIGHEST)
        for h in range(M_HEADS):
            b_col = bcol_all[:, M_HEADS + h:M_HEADS + h + 1]
            b_row = brow_all[M_HEADS + h:M_HEADS + h + 1, :]
            i_row = smT[h:h + 1, :]
            g = b_row[:, L - 1:L]
            m_prev = m_ref[h:h + 1, 0:1]
            d = jnp.where(causal, b_col - b_row + i_row, NEG)
            inter = b_col + m_prev
            m_t = jnp.maximum(inter, jnp.max(d, axis=1, keepdims=True))
            dw = jnp.exp(d - m_t)
            pair = h // 2
            q_pair = q_ref[sl, pair * LANES:(pair + 1) * LANES]
            qm = jnp.where((col // M_DQK) == (h % 2), q_pair, jnp.zeros_like(q_pair))
            kT_pair = kT_ref[pair * LANES:(pair + 1) * LANES, sl]
            s = _dot(qm, kT_pair)
            sw = (s * dw).astype(BF16)
            v_aug = jnp.concatenate([v_ref[sl, h * M_DV:(h + 1) * M_DV], ones_col], axis=1)
            c_h = c_ref[h]
            c_bf = c_h.astype(BF16)
            c_pad = jnp.concatenate([c_bf, zeros_c] if h % 2 == 0 else [zeros_c, c_bf], axis=0)
            isc = jnp.exp(inter - m_t)
            nd = isc * _dot(qm, c_pad) + _dot(sw, v_aug)
            num = nd[:, 0:M_DV]
            den = nd[:, M_DV:M_DV + 1]
            hh = num / jnp.maximum(jnp.abs(den), jnp.exp(-m_t))
            a_row = g - b_row + i_row
            a_max = jnp.max(a_row, axis=1, keepdims=True)
            w_row = jnp.exp(a_row - a_max)
            kT_h = kT_ref[h * M_DQK:(h + 1) * M_DQK, sl].astype(F32)
            dc = _dot((kT_h * w_row).astype(BF16), v_aug)
            m_new = jnp.maximum(g + m_prev, a_max)
            c_ref[h] = jnp.exp(g + m_prev - m_new) * c_h + jnp.exp(a_max - m_new) * dc
            m_ref[h:h + 1, :] = jnp.broadcast_to(m_new, (1, LANES))
            hn = _rmsnorm_rows(hh, onw_ref[h:h + 1, :])
            o_ref[sl, h * M_DV:(h + 1) * M_DV] = hn * og_ref[sl, h * M_DV:(h + 1) * M_DV]


def _mlstm(q, kT, v, small, smallT, og, onw, batch, seq, tb):
    t = q.shape[0]
    nblk = seq // tb
    return pl.pallas_call(
        _mlstm_kernel,
        grid=(batch, nblk),
        in_specs=[
            pl.BlockSpec((tb, M_QK), lambda b, j: (b * nblk + j, 0)),
            pl.BlockSpec((M_QK, tb), lambda b, j: (0, b * nblk + j)),
            pl.BlockSpec((tb, M_V), lambda b, j: (b * nblk + j, 0)),
            pl.BlockSpec((tb, LANES), lambda b, j: (b * nblk + j, 0)),
            pl.BlockSpec((32, tb), lambda b, j: (0, b * nblk + j)),
            pl.BlockSpec((tb, M_V), lambda b, j: (b * nblk + j, 0)),
            pl.BlockSpec((8, M_DV), lambda b, j: (0, 0)),
        ],
        out_specs=pl.BlockSpec((tb, M_V), lambda b, j: (b * nblk + j, 0)),
        out_shape=jax.ShapeDtypeStruct((t, M_V), F32),
        scratch_shapes=[pltpu.VMEM((M_HEADS, M_DQK, 2 * M_DV), F32), pltpu.VMEM((8, LANES), F32)],
        compiler_params=_params(("arbitrary", "arbitrary")),
        name="mlstm",
    )(q, kT, v, small, smallT, og, onw)


def _nsa_kernel(qT_ref, kcmp_ref, vcmpT_ref, ovT_ref, ks_ref, vsT_ref, kw_ref, vwT_ref, gT_ref,
                o_ref, bias_ref, *, n_rounds):
    qi = pl.program_id(1)
    tq = Q_TILE
    nb = kcmp_ref.shape[0]
    nsel = ovT_ref.shape[0]
    q0 = qi * tq
    wide = N_HG * tq
    lane_w = lax.broadcasted_iota(jnp.int32, (1, wide), 1)
    tpos_w = q0 + (lane_w % tq)
    tpos = q0 + lax.broadcasted_iota(jnp.int32, (1, tq), 1)
    zeros_q = jnp.zeros((N_DH, wide), BF16)
    key_in_tile = lax.broadcasted_iota(jnp.int32, (KEY_TILE, 1), 0)
    outs = []

    for g in range(N_KV):
        rows = slice(g * N_DH, (g + 1) * N_DH)
        q4 = jnp.concatenate(
            [qT_ref[(g * N_HG + h) * N_DH:(g * N_HG + h + 1) * N_DH, :] for h in range(N_HG)], axis=1)
        qpad = jnp.concatenate([q4, zeros_q] if g == 0 else [zeros_q, q4], axis=0)

        sc = _dot(kcmp_ref[...], qpad)
        cend = lax.broadcasted_iota(jnp.int32, (nb, 1), 0) * CMP_STRIDE + (CMP_LEN - 1)
        cmask = cend <= tpos_w
        sc = jnp.where(cmask, sc, NEG)
        pc = jnp.where(cmask, jnp.exp(sc - jnp.max(sc, axis=0, keepdims=True)), 0.0)
        lc = jnp.sum(pc, axis=0, keepdims=True)
        pc = pc * (1.0 / jnp.maximum(lc, 1e-30))
        o_cmp = _dot(vcmpT_ref[rows, :], pc.astype(BF16))
        psum = pc[:, 0:tq]
        for h in range(1, N_HG):
            psum = psum + pc[:, h * tq:(h + 1) * tq]
        imp = jnp.dot(ovT_ref[...], psum, preferred_element_type=F32, precision=HIGHEST)

        jblk = lax.broadcasted_iota(jnp.int32, (nsel, tq), 0)
        cur = tpos // SEL_BLOCK
        forced = (jblk == 0) | (jblk == cur) | (jblk == cur - 1)
        cand = (jblk >= 1) & (jblk <= cur - 2)
        val = jnp.where(cand, imp, -jnp.inf)
        sel = forced
        jblk_f = jblk.astype(F32)
        for _ in range(n_rounds):
            mx = jnp.max(val, axis=0, keepdims=True)
            first = jnp.min(jnp.where(val == mx, jblk_f, float(nsel)), axis=0, keepdims=True)
            hit = jblk_f == first
            sel = sel | hit
            val = jnp.where(hit, -jnp.inf, val)
        bias_ref[g] = jnp.where(sel, 0.0, NEG)

        def sel_scores(kt):
            start = pl.multiple_of(kt * KEY_TILE, KEY_TILE)
            s = _dot(ks_ref[pl.ds(start, KEY_TILE), :], qpad)
            parts = []
            for jj in range(KEY_TILE // SEL_BLOCK):
                brow = bias_ref[g, pl.ds(kt * (KEY_TILE // SEL_BLOCK) + jj, 1), :]
                brow = jnp.concatenate([brow] * N_HG, axis=1)
                parts.append(jnp.broadcast_to(brow, (SEL_BLOCK, wide)))
            return s + jnp.concatenate(parts, axis=0), start

        def sel_values(kt):
            return jnp.concatenate(
                [vsT_ref[kt * (KEY_TILE // LANES) + j, rows, :] for j in range(KEY_TILE // LANES)], axis=1)

        def online(carry, s, vt):
            m, l, acc = carry
            m_new = jnp.maximum(m, jnp.max(s, axis=0, keepdims=True))
            alpha = jnp.exp(m - m_new)
            p = jnp.exp(s - m_new)
            l = alpha * l + jnp.sum(p, axis=0, keepdims=True)
            acc = alpha * acc + _dot(vt, p.astype(BF16))
            return m_new, l, acc

        def full_tile(kt, carry):
            s, _ = sel_scores(kt)
            return online(carry, s, sel_values(kt))

        n_full = qi // (KEY_TILE // tq)
        carry = (jnp.full((1, wide), NEG, F32), jnp.zeros((1, wide), F32), jnp.zeros((N_DH, wide), F32))
        carry = lax.fori_loop(0, n_full, full_tile, carry)
        s, start = sel_scores(n_full)
        s = jnp.where(start + key_in_tile <= tpos_w, s, NEG)
        _, l_s, acc_s = online(carry, s, sel_values(n_full))
        o_slc = acc_s * (1.0 / l_s)

        start_w = pl.multiple_of(jnp.maximum(q0 - (WIN_SPAN - tq), 0), LANES)
        sw = _dot(kw_ref[pl.ds(start_w, WIN_SPAN), :], qpad)
        kpos = start_w + lax.broadcasted_iota(jnp.int32, (WIN_SPAN, 1), 0)
        wmask = (kpos <= tpos_w) & (kpos > tpos_w - WINDOW)
        sw = jnp.where(wmask, sw, NEG)
        pw = jnp.exp(sw - jnp.max(sw, axis=0, keepdims=True))
        lw = jnp.sum(pw, axis=0, keepdims=True)
        vwin = jnp.concatenate(
            [vwT_ref[start_w // LANES + j, rows, :] for j in range(WIN_SPAN // LANES)], axis=1)
        o_win = _dot(vwin, pw.astype(BF16)) * (1.0 / lw)

        for h in range(N_HG):
            cs = slice(h * tq, (h + 1) * tq)
            r = 2 * M_HEADS + (g * N_HG + h) * 3
            outs.append(gT_ref[r:r + 1, :] * o_cmp[:, cs] + gT_ref[r + 1:r + 2, :] * o_slc[:, cs]
                        + gT_ref[r + 2:r + 3, :] * o_win[:, cs])

    o_ref[...] = jnp.concatenate(outs, axis=0).T


def _nsa(qT, kcmp, vcmpT, ovT, ks, vsT, kw, vwT, smallT, batch, seq):
    t = qT.shape[1]
    nq = seq // Q_TILE
    nb = kcmp.shape[1]
    nsel = ovT.shape[0]
    n_rounds = max(min(SEL_TOPN, nsel) - 3, 0)
    kern = functools.partial(_nsa_kernel, n_rounds=n_rounds)
    return pl.pallas_call(
        kern,
        grid=(batch, nq),
        in_specs=[
            pl.BlockSpec((N_Q, Q_TILE), lambda b, i: (0, b * nq + i)),
            pl.BlockSpec((None, nb, N_KVW), lambda b, i: (b, 0, 0)),
            pl.BlockSpec((None, N_KVW, nb), lambda b, i: (b, 0, 0)),
            pl.BlockSpec((nsel, nb), lambda b, i: (0, 0)),
            pl.BlockSpec((seq, N_KVW), lambda b, i: (b, 0)),
            pl.BlockSpec((seq // LANES, N_KVW, LANES), lambda b, i: (b, 0, 0)),
            pl.BlockSpec((seq, N_KVW), lambda b, i: (b, 0)),
            pl.BlockSpec((seq // LANES, N_KVW, LANES), lambda b, i: (b, 0, 0)),
            pl.BlockSpec((32, Q_TILE), lambda b, i: (0, b * nq + i)),
        ],
        out_specs=pl.BlockSpec((Q_TILE, N_Q), lambda b, i: (b * nq + i, 0)),
        out_shape=jax.ShapeDtypeStruct((t, N_Q), F32),
        scratch_shapes=[pltpu.VMEM((N_KV, nsel, Q_TILE), F32)],
        compiler_params=_params(("arbitrary", "arbitrary")),
        name="nsa",
    )(qT, kcmp, vcmpT, ovT, ks, vsT, kw, vwT, smallT)


def _merge_kernel(x_ref, n1w_ref, wg_ref, gb_ref, hm_ref, on_ref, wm_ref, wn_ref, wo_ref, o_ref):
    x = x_ref[...]
    hn = _rmsnorm_rows(x, n1w_ref[...]).astype(BF16)
    gm = _sigmoid(_dot(hn, wg_ref[:, 0:D_MODEL]) + gb_ref[0:1, :])
    gn = _sigmoid(_dot(hn, wg_ref[:, D_MODEL:]) + gb_ref[1:2, :])
    y = gm * _dot(hm_ref[...].astype(BF16), wm_ref[...]) + gn * _dot(on_ref[...].astype(BF16), wn_ref[...])
    o_ref[...] = x + _dot(y.astype(BF16), wo_ref[...])


def _merge(x2, n1w, wg, gb, hm, on, wm, wn, wo, tm):
    t = x2.shape[0]
    const = lambda i: (0, 0)
    return pl.pallas_call(
        _merge_kernel,
        grid=(t // tm,),
        in_specs=[
            pl.BlockSpec((tm, D_MODEL), lambda i: (i, 0)),
            pl.BlockSpec((1, D_MODEL), const),
            pl.BlockSpec((D_MODEL, 2 * D_MODEL), const),
            pl.BlockSpec((2, D_MODEL), const),
            pl.BlockSpec((tm, M_V), lambda i: (i, 0)),
            pl.BlockSpec((tm, N_Q), lambda i: (i, 0)),
            pl.BlockSpec((M_V, D_MODEL), const),
            pl.BlockSpec((N_Q, D_MODEL), const),
            pl.BlockSpec((D_MODEL, D_MODEL), const),
        ],
        out_specs=pl.BlockSpec((tm, D_MODEL), lambda i: (i, 0)),
        out_shape=jax.ShapeDtypeStruct((t, D_MODEL), F32),
        compiler_params=_params(("arbitrary",)),
        name="merge",
    )(x2, n1w, wg, gb, hm, on, wm, wn, wo)


def _ffn_kernel(x_ref, n2w_ref, wup_ref, cw_ref, cb_ref, wdn_ref, o_ref, buf_ref, *, tiles_per_seq):
    i = pl.program_id(0)
    tm = x_ref.shape[0]
    x = x_ref[...]
    hn = _rmsnorm_rows(x, n2w_ref[...]).astype(BF16)

    @pl.when(i % tiles_per_seq == 0)
    def _():
        buf_ref[0:8, :] = jnp.zeros((8, D_FF), F32)

    a = _dot(hn, wup_ref[:, 0:D_FF])
    buf_ref[8:8 + tm, :] = a
    acc = cb_ref[...] + cw_ref[FFN_CONV - 1:FFN_CONV, :] * a
    for k in range(FFN_CONV - 1):
        acc = acc + cw_ref[k:k + 1, :] * buf_ref[8 - (FFN_CONV - 1) + k:8 - (FFN_CONV - 1) + k + tm, :]
    buf_ref[0:8, :] = buf_ref[tm:tm + 8, :]
    v = _dot(hn, wup_ref[:, D_FF:])
    o_ref[...] = x + _dot((_gelu(acc) * v).astype(BF16), wdn_ref[...])


def _ffn(x2, n2w, wup, cw, cb, wdn, seq, tm):
    t = x2.shape[0]
    const = lambda i: (0, 0)
    kern = functools.partial(_ffn_kernel, tiles_per_seq=seq // tm)
    return pl.pallas_call(
        kern,
        grid=(t // tm,),
        in_specs=[
            pl.BlockSpec((tm, D_MODEL), lambda i: (i, 0)),
            pl.BlockSpec((1, D_MODEL), const),
            pl.BlockSpec((D_MODEL, 2 * D_FF), const, pipeline_mode=pl.Buffered(1)),
            pl.BlockSpec((FFN_CONV, D_FF), const),
            pl.BlockSpec((1, D_FF), const),
            pl.BlockSpec((D_FF, D_MODEL), const, pipeline_mode=pl.Buffered(1)),
        ],
        out_specs=pl.BlockSpec((tm, D_MODEL), lambda i: (i, 0)),
        out_shape=jax.ShapeDtypeStruct((t, D_MODEL), F32),
        scratch_shapes=[pltpu.VMEM((tm + 8, D_FF), F32)],
        compiler_params=_params(("arbitrary",)),
        name="ffn",
    )(x2, n2w, wup, cw, cb, wdn)


def _cols(w, *names):
    return jnp.concatenate([w[:, _OFF[n][0]:_OFF[n][1]] for n in names], axis=1)


def _layer(x, n1w, w_in, m_conv_w, m_conv_b, m_igate_b, m_fgate_b, m_out_norm_w,
           q_norm_w, kcmp_norm_w, kslc_norm_w, kwin_norm_w,
           cmp_k_pe, cmp_k_w1, cmp_k_w2, cmp_v_pe, cmp_v_w1, cmp_v_w2,
           w_up_m, w_up_n, merge_gate_b, w_out, norm2_w, ffn_w_up, ffn_conv_w, ffn_conv_b, ffn_w_down):
    batch, seq, _ = x.shape
    t = batch * seq
    x2 = x.reshape(t, D_MODEL)
    n1w2 = n1w.reshape(1, D_MODEL)
    tm = 256

    small_pad = jnp.zeros((D_MODEL, LANES - 2 * M_HEADS - 3 * N_HEADS), F32)
    w_m = jnp.concatenate([_cols(w_in, "mq", "mk", "mv", "mo", "mi", "mf", "ng"), small_pad], axis=1).astype(BF16)
    w_n = _cols(w_in, "nq", "kc", "vc", "ks", "vs", "kw", "vw").astype(BF16)
    w_g = _cols(w_in, "gm", "gn").astype(BF16)
    sbias = jnp.concatenate([m_igate_b, m_fgate_b, jnp.zeros((LANES - 2 * M_HEADS,), F32)]).reshape(1, LANES)

    half = N_DH // 2
    pos = jnp.arange(seq, dtype=F32)
    inv = ROPE_THETA ** (-jnp.arange(0, N_DH, 2, dtype=F32) / N_DH)
    ang = pos[:, None] * inv[None, :]
    cos, sin = jnp.cos(ang), jnp.sin(ang)
    cosn = jnp.tile(cos, (1, N_KVW // half))
    sinn = jnp.tile(jnp.concatenate([-sin, sin], axis=1), (1, N_KV))
    cosT, sinT = cos.T, sin.T
    knw = jnp.zeros((8, N_KVW), F32).at[0:3].set(
        jnp.stack([jnp.tile(w, N_KV) for w in (kcmp_norm_w, kslc_norm_w, kwin_norm_w)]))

    q_m, kT_m, v_m, og, small, smallT = _proj_m(
        x2, n1w2, w_m, m_conv_w, m_conv_b.reshape(1, -1), sbias, seq, tm)
    qT, kc, vc, ks, kw, vsT, vwT = _proj_n(
        x2, n1w2, w_n, q_norm_w.reshape(N_DH, 1), knw, cosn, sinn, cosT, sinT, seq, tm)

    nb = seq // CMP_STRIDE

    def to_blocks(a):
        a = a.reshape(batch, nb, CMP_STRIDE, N_KV, N_DH).transpose(0, 3, 1, 2, 4)
        a = a.reshape(batch * N_KV, nb, CMP_STRIDE * N_DH)
        return jnp.pad(a, ((0, 0), (0, 8), (0, 0)))

    xkv = jnp.stack([to_blocks(kc), to_blocks(vc)])
    pe = jnp.stack([cmp_k_pe.reshape(1, -1), cmp_v_pe.reshape(1, -1)])
    w1 = jnp.stack([cmp_k_w1, cmp_v_w1]).astype(BF16)
    w2 = jnp.stack([cmp_k_w2, cmp_v_w2]).astype(BF16)
    cmp = _compress(xkv, pe, w1, w2).reshape(2, batch, N_KV, nb, N_DH)
    kcmp = cmp[0].transpose(0, 2, 1, 3).reshape(batch, nb, N_KVW).astype(BF16)
    vcmpT = cmp[1].transpose(0, 1, 3, 2).reshape(batch, N_KVW, nb).astype(BF16)

    nsel = seq // SEL_BLOCK
    cstart = np.arange(nb) * CMP_STRIDE
    sstart = np.arange(nsel) * SEL_BLOCK
    ovT = ((cstart[None, :] <= sstart[:, None] + SEL_BLOCK - 1)
           & (cstart[None, :] + CMP_LEN - 1 >= sstart[:, None])).astype(np.float32)

    onw = jnp.zeros((8, M_DV), F32).at[0:M_HEADS].set(m_out_norm_w)
    hm = _mlstm(q_m, kT_m, v_m, small, smallT, og, onw, batch, seq, 4 * MLSTM_CHUNK)
    on = _nsa(qT, kcmp, vcmpT, jnp.asarray(ovT), ks, vsT, kw, vwT, smallT, batch, seq)

    x1 = _merge(x2, n1w2, w_g, merge_gate_b, hm, on, w_up_m.astype(BF16), w_up_n.astype(BF16),
                w_out.astype(BF16), tm)
    out = _ffn(x1, norm2_w.reshape(1, D_MODEL), ffn_w_up.astype(BF16), ffn_conv_w, ffn_conv_b.reshape(1, -1),
               ffn_w_down.astype(BF16), seq, tm)
    return out.reshape(batch, seq, D_MODEL)


def kernel(x, norm1_w, w_in, m_conv_w, m_conv_b, m_igate_b, m_fgate_b, m_out_norm_w, q_norm_w, kcmp_norm_w,
           kslc_norm_w, kwin_norm_w, cmp_k_pe, cmp_k_w1, cmp_k_w2, cmp_v_pe, cmp_v_w1, cmp_v_w2, w_up_m, w_up_n,
           merge_gate_b, w_out, norm2_w, ffn_w_up, ffn_conv_w, ffn_conv_b, ffn_w_down):
    params = (norm1_w, w_in, m_conv_w, m_conv_b, m_igate_b, m_fgate_b, m_out_norm_w, q_norm_w, kcmp_norm_w,
              kslc_norm_w, kwin_norm_w, cmp_k_pe, cmp_k_w1, cmp_k_w2, cmp_v_pe, cmp_v_w1, cmp_v_w2, w_up_m, w_up_n,
              merge_gate_b, w_out, norm2_w, ffn_w_up, ffn_conv_w, ffn_conv_b, ffn_w_down)
    for layer in range(norm1_w.shape[0]):
        x = _layer(x, *[p[layer] for p in params])
    return x
```

```python
import functools
import math

import jax
import jax.numpy as jnp
import numpy as np
from jax import lax
from jax.experimental import pallas as pl
from jax.experimental.pallas import tpu as pltpu

D_MODEL = 1024
EPS = 1e-6
ROPE_THETA = 10000.0
NEG = -1e30
M_HEADS = 4
M_DQK = 64
M_DV = 128
M_CONV = 4
M_QK = M_HEADS * M_DQK
M_V = M_HEADS * M_DV
N_HEADS = 8
N_KV = 2
N_HG = N_HEADS // N_KV
N_DH = 64
N_Q = N_HEADS * N_DH
N_KVW = N_KV * N_DH
CMP_LEN = 32
CMP_STRIDE = 16
CMP_HIDDEN = 256
SEL_BLOCK = 64
SEL_TOPN = 16
WINDOW = 512
D_FF = 2816
FFN_CONV = 3

_OFF = {}
_o = 0
for _name, _size in (("mq", M_QK), ("mk", M_QK), ("mv", M_V), ("mo", M_V), ("mi", M_HEADS), ("mf", M_HEADS),
                     ("nq", N_Q), ("kc", N_KVW), ("vc", N_KVW), ("ks", N_KVW), ("vs", N_KVW), ("kw", N_KVW),
                     ("vw", N_KVW), ("ng", 3 * N_HEADS), ("gm", D_MODEL), ("gn", D_MODEL)):
    _OFF[_name] = (_o, _o + _size)
    _o += _size

LANES = 128
MLSTM_CHUNK = 128
Q_TILE = 128
KEY_TILE = 256
WIN_SPAN = WINDOW + Q_TILE
VMEM_LIMIT = 56 * 1024 * 1024

LOG2E = math.log2(math.e)

F32 = jnp.float32
BF16 = jnp.bfloat16
HIGHEST = lax.Precision.HIGHEST


def _dot(a, b):
    return jnp.dot(a, b, preferred_element_type=F32)


def _rmsnorm_rows(x, w):
    return x * lax.rsqrt(jnp.mean(x * x, axis=-1, keepdims=True) + EPS) * w


def _sigmoid(x):
    return 1.0 / (1.0 + jnp.exp(-x))


def _gelu(x):
    return 0.5 * x * (1.0 + lax.erf(x * (1.0 / math.sqrt(2.0))))


def _params(sem):
    return pltpu.CompilerParams(dimension_semantics=sem, vmem_limit_bytes=VMEM_LIMIT)


def _proj_m_kernel(x_ref, n1w_ref, w_ref, cw_ref, cb_ref, sb_ref,
                   q_ref, kT_ref, v_ref, og_ref, small_ref, smallT_ref, buf_ref, *, tiles_per_seq):
    i = pl.program_id(0)
    tm = x_ref.shape[0]
    hn = _rmsnorm_rows(x_ref[...], n1w_ref[...]).astype(BF16)

    @pl.when(i % tiles_per_seq == 0)
    def _():
        buf_ref[0:8, :] = jnp.zeros((8, 2 * M_QK), F32)

    qk = _dot(hn, w_ref[:, 0:2 * M_QK])
    buf_ref[8:8 + tm, :] = qk
    acc = cb_ref[...] + cw_ref[M_CONV - 1:M_CONV, :] * qk
    for k in range(M_CONV - 1):
        acc = acc + cw_ref[k:k + 1, :] * buf_ref[8 - (M_CONV - 1) + k:8 - (M_CONV - 1) + k + tm, :]
    buf_ref[0:8, :] = buf_ref[tm:tm + 8, :]
    act = acc * _sigmoid(acc)
    q_ref[...] = (act[:, 0:M_QK] * (M_DQK ** -0.5)).astype(BF16)
    kT_ref[...] = act[:, M_QK:2 * M_QK].T.astype(BF16)
    v_ref[...] = _dot(hn, w_ref[:, 2 * M_QK:2 * M_QK + M_V]).astype(BF16)
    og_ref[...] = _sigmoid(_dot(hn, w_ref[:, 2 * M_QK + M_V:2 * M_QK + 2 * M_V]))
    sm = _dot(hn, w_ref[:, 2 * M_QK + 2 * M_V:]) + sb_ref[...]
    lane = lax.broadcasted_iota(jnp.int32, sm.shape, 1)
    logsig = jnp.minimum(sm, 0.0) - jnp.log1p(jnp.exp(-jnp.abs(sm)))
    sm = jnp.where(lane < M_HEADS, sm, jnp.where(lane < 2 * M_HEADS, logsig, _sigmoid(sm)))
    small_ref[...] = sm
    smallT_ref[...] = sm.T[0:32, :]


def _proj_m(x2, n1w, w, cw, cb, sb, seq, tm):
    t = x2.shape[0]
    ncol = w.shape[1]
    kern = functools.partial(_proj_m_kernel, tiles_per_seq=seq // tm)
    return pl.pallas_call(
        kern,
        grid=(t // tm,),
        in_specs=[
            pl.BlockSpec((tm, D_MODEL), lambda i: (i, 0)),
            pl.BlockSpec((1, D_MODEL), lambda i: (0, 0)),
            pl.BlockSpec((D_MODEL, ncol), lambda i: (0, 0)),
            pl.BlockSpec((M_CONV, 2 * M_QK), lambda i: (0, 0)),
            pl.BlockSpec((1, 2 * M_QK), lambda i: (0, 0)),
            pl.BlockSpec((1, LANES), lambda i: (0, 0)),
        ],
        out_specs=[
            pl.BlockSpec((tm, M_QK), lambda i: (i, 0)),
            pl.BlockSpec((M_QK, tm), lambda i: (0, i)),
            pl.BlockSpec((tm, M_V), lambda i: (i, 0)),
            pl.BlockSpec((tm, M_V), lambda i: (i, 0)),
            pl.BlockSpec((tm, LANES), lambda i: (i, 0)),
            pl.BlockSpec((32, tm), lambda i: (0, i)),
        ],
        out_shape=[
            jax.ShapeDtypeStruct((t, M_QK), BF16),
            jax.ShapeDtypeStruct((M_QK, t), BF16),
            jax.ShapeDtypeStruct((t, M_V), BF16),
            jax.ShapeDtypeStruct((t, M_V), F32),
            jax.ShapeDtypeStruct((t, LANES), F32),
            jax.ShapeDtypeStruct((32, t), F32),
        ],
        scratch_shapes=[pltpu.VMEM((tm + 8, 2 * M_QK), F32)],
        compiler_params=_params(("arbitrary",)),
        name="proj_m",
    )(x2, n1w, w, cw, cb, sb)


def _proj_n_kernel(x_ref, n1w_ref, w_ref, qnw_ref, knw_ref, cosn_ref, sinn_ref, cosT_ref, sinT_ref,
                   qT_ref, kc_ref, vc_ref, ks_ref, kw_ref, vsT_ref, vwT_ref):
    tm = x_ref.shape[0]
    hn = _rmsnorm_rows(x_ref[...], n1w_ref[...]).astype(BF16)
    qT = _dot(hn, w_ref[:, 0:N_Q]).T
    cosT = cosT_ref[...]
    sinT = sinT_ref[...]
    qnw = qnw_ref[...]
    half = N_DH // 2
    for h in range(N_HEADS):
        xh = qT[h * N_DH:(h + 1) * N_DH, :]
        xn = xh * lax.rsqrt(jnp.mean(xh * xh, axis=0, keepdims=True) + EPS) * qnw
        x1 = xn[0:half, :]
        x2 = xn[half:, :]
        o = jnp.concatenate([x1 * cosT - x2 * sinT, x2 * cosT + x1 * sinT], axis=0) * (LOG2E * N_DH ** -0.5)
        qT_ref[h * N_DH:(h + 1) * N_DH, :] = o.astype(BF16)

    lane = lax.broadcasted_iota(jnp.int32, (tm, N_KVW), 1)
    first_head = lane < N_DH
    first_half = (lane % N_DH) < half
    cosn = cosn_ref[...]
    sinn = sinn_ref[...]

    def knorm_rope(col, row):
        k = _dot(hn, w_ref[:, col:col + N_KVW])
        k2 = k * k
        s0 = jnp.sum(k2[:, 0:N_DH], axis=1, keepdims=True) * (1.0 / N_DH)
        s1 = jnp.sum(k2[:, N_DH:], axis=1, keepdims=True) * (1.0 / N_DH)
        r = jnp.where(first_head, lax.rsqrt(s0 + EPS), lax.rsqrt(s1 + EPS))
        kn = k * r * knw_ref[row:row + 1, :]
        rot = jnp.where(first_half, pltpu.roll(kn, N_KVW - half, 1), pltpu.roll(kn, half, 1))
        return kn * cosn + rot * sinn

    c0 = N_Q
    kc_ref[...] = knorm_rope(c0, 0)
    vc_ref[...] = _dot(hn, w_ref[:, c0 + N_KVW:c0 + 2 * N_KVW])
    ks_ref[...] = knorm_rope(c0 + 2 * N_KVW, 1).astype(BF16)
    vsT = _dot(hn, w_ref[:, c0 + 3 * N_KVW:c0 + 4 * N_KVW]).T.astype(BF16)
    kw_ref[...] = knorm_rope(c0 + 4 * N_KVW, 2).astype(BF16)
    vwT = _dot(hn, w_ref[:, c0 + 5 * N_KVW:c0 + 6 * N_KVW]).T.astype(BF16)
    for j in range(tm // LANES):
        vsT_ref[j] = vsT[:, j * LANES:(j + 1) * LANES]
        vwT_ref[j] = vwT[:, j * LANES:(j + 1) * LANES]


def _proj_n(x2, n1w, w, qnw, knw, cosn, sinn, cosT, sinT, seq, tm):
    t = x2.shape[0]
    ncol = w.shape[1]
    tps = seq // tm
    half = N_DH // 2
    return pl.pallas_call(
        _proj_n_kernel,
        grid=(t // tm,),
        in_specs=[
            pl.BlockSpec((tm, D_MODEL), lambda i: (i, 0)),
            pl.BlockSpec((1, D_MODEL), lambda i: (0, 0)),
            pl.BlockSpec((D_MODEL, ncol), lambda i: (0, 0)),
            pl.BlockSpec((N_DH, 1), lambda i: (0, 0)),
            pl.BlockSpec((8, N_KVW), lambda i: (0, 0)),
            pl.BlockSpec((tm, N_KVW), lambda i: (i % tps, 0)),
            pl.BlockSpec((tm, N_KVW), lambda i: (i % tps, 0)),
            pl.BlockSpec((half, tm), lambda i: (0, i % tps)),
            pl.BlockSpec((half, tm), lambda i: (0, i % tps)),
        ],
        out_specs=[
            pl.BlockSpec((N_Q, tm), lambda i: (0, i)),
            pl.BlockSpec((tm, N_KVW), lambda i: (i, 0)),
            pl.BlockSpec((tm, N_KVW), lambda i: (i, 0)),
            pl.BlockSpec((tm, N_KVW), lambda i: (i, 0)),
            pl.BlockSpec((tm, N_KVW), lambda i: (i, 0)),
            pl.BlockSpec((tm // LANES, N_KVW, LANES), lambda i: (i, 0, 0)),
            pl.BlockSpec((tm // LANES, N_KVW, LANES), lambda i: (i, 0, 0)),
        ],
        out_shape=[
            jax.ShapeDtypeStruct((N_Q, t), BF16),
            jax.ShapeDtypeStruct((t, N_KVW), F32),
            jax.ShapeDtypeStruct((t, N_KVW), F32),
            jax.ShapeDtypeStruct((t, N_KVW), BF16),
            jax.ShapeDtypeStruct((t, N_KVW), BF16),
            jax.ShapeDtypeStruct((t // LANES, N_KVW, LANES), BF16),
            jax.ShapeDtypeStruct((t // LANES, N_KVW, LANES), BF16),
        ],
        compiler_params=_params(("arbitrary",)),
        name="proj_n",
    )(x2, n1w, w, qnw, knw, cosn, sinn, cosT, sinT)


def _compress_kernel(x_ref, pe_ref, w1_ref, w2_ref, o_ref):
    nb = o_ref.shape[2]
    half = (CMP_LEN // 2) * N_DH
    pe = pe_ref[0]
    xa = (x_ref[0, 0, 0:nb, :] + pe[:, 0:half]).astype(BF16)
    xb = (x_ref[0, 0, pl.ds(1, nb), :] + pe[:, half:]).astype(BF16)
    hid = _dot(xa, w1_ref[0, 0:half, :]) + _dot(xb, w1_ref[0, half:, :])
    o_ref[0, 0] = _dot(_gelu(hid).astype(BF16), w2_ref[0])


def _compress(xkv, pe, w1, w2):
    _, bg, nbp, width = xkv.shape
    nb = nbp - 8
    return pl.pallas_call(
        _compress_kernel,
        grid=(2, bg),
        in_specs=[
            pl.BlockSpec((1, 1, nbp, width), lambda a, b: (a, b, 0, 0)),
            pl.BlockSpec((1, 1, CMP_LEN * N_DH), lambda a, b: (a, 0, 0)),
            pl.BlockSpec((1, CMP_LEN * N_DH, CMP_HIDDEN), lambda a, b: (a, 0, 0)),
            pl.BlockSpec((1, CMP_HIDDEN, N_DH), lambda a, b: (a, 0, 0)),
        ],
        out_specs=pl.BlockSpec((1, 1, nb, N_DH), lambda a, b: (a, b, 0, 0)),
        out_shape=jax.ShapeDtypeStruct((2, bg, nb, N_DH), F32),
        compiler_params=_params(("arbitrary", "arbitrary")),
        name="compress",
    )(xkv, pe, w1, w2)


def _mlstm_kernel(q_ref, kT_ref, v_ref, small_ref, smallT_ref, og_ref, onw_ref, o_ref, c_ref, m_ref):
    L = MLSTM_CHUNK
    tb = q_ref.shape[0]

    @pl.when(pl.program_id(1) == 0)
    def _():
        c_ref[...] = jnp.zeros(c_ref.shape, F32)
        m_ref[...] = jnp.zeros(m_ref.shape, F32)

    row = lax.broadcasted_iota(jnp.int32, (L, L), 0)
    col = lax.broadcasted_iota(jnp.int32, (L, L), 1)
    causal = col <= row
    tril = causal.astype(F32)
    triu = (row <= col).astype(F32)
    ones_col = (col == 0).astype(BF16)
    zeros_c = jnp.zeros((M_DQK, 2 * M_DV), BF16)

    for c in range(tb // L):
        sl = slice(c * L, (c + 1) * L)
        sm = small_ref[sl, :]
        smT = smallT_ref[0:8, sl]
        bcol_all = jnp.dot(tril, sm, preferred_element_type=F32, precision=HIGHEST)
        brow_all = jnp.dot(smT, triu, preferred_element_type=F32, precision=HIGHEST)
        for h in range(M_HEADS):
            b_col = bcol_all[:, M_HEADS + h:M_HEADS + h + 1]
            b_row = brow_all[M_HEADS + h:M_HEADS + h + 1, :]
            i_row = smT[h:h + 1, :]
            g = b_row[:, L - 1:L]
            m_prev = m_ref[h:h + 1, 0:1]
            d = jnp.where(causal, b_col - b_row + i_row, NEG)
            inter = b_col + m_prev
            m_t = jnp.maximum(inter, jnp.max(d, axis=1, keepdims=True))
            dw = jnp.exp(d - m_t)
            pair = h // 2
            q_pair = q_ref[sl, pair * LANES:(pair + 1) * LANES]
            qm = jnp.where((col // M_DQK) == (h % 2), q_pair, jnp.zeros_like(q_pair))
            kT_pair = kT_ref[pair * LANES:(pair + 1) * LANES, sl]
            s = _dot(qm, kT_pair)
            sw = (s * dw).astype(BF16)
            v_aug = jnp.concatenate([v_ref[sl, h * M_DV:(h + 1) * M_DV], ones_col], axis=1)
            c_h = c_ref[h]
            c_bf = c_h.astype(BF16)
            c_pad = jnp.concatenate([c_bf, zeros_c] if h % 2 == 0 else [zeros_c, c_bf], axis=0)
            isc = jnp.exp(inter - m_t)
            nd = isc * _dot(qm, c_pad) + _dot(sw, v_aug)
            num = nd[:, 0:M_DV]
            den = nd[:, M_DV:M_DV + 1]
            hh = num / jnp.maximum(jnp.abs(den), jnp.exp(-m_t))
            a_row = g - b_row + i_row
            a_max = jnp.max(a_row, axis=1, keepdims=True)
            w_row = jnp.exp(a_row - a_max)
            kT_h = kT_ref[h * M_DQK:(h + 1) * M_DQK, sl].astype(F32)
            dc = _dot((kT_h * w_row).astype(BF16), v_aug)
            m_new = jnp.maximum(g + m_prev, a_max)
            c_ref[h] = jnp.exp(g + m_prev - m_new) * c_h + jnp.exp(a_max - m_new) * dc
            m_ref[h:h + 1, :] = jnp.broadcast_to(m_new, (1, LANES))
            hn = _rmsnorm_rows(hh, onw_ref[h:h + 1, :])
            o_ref[sl, h * M_DV:(h + 1) * M_DV] = hn * og_ref[sl, h * M_DV:(h + 1) * M_DV]


def _mlstm(q, kT, v, small, smallT, og, onw, batch, seq, tb):
    t = q.shape[0]
    nblk = seq // tb
    return pl.pallas_call(
        _mlstm_kernel,
        grid=(batch, nblk),
        in_specs=[
            pl.BlockSpec((tb, M_QK), lambda b, j: (b * nblk + j, 0)),
            pl.BlockSpec((M_QK, tb), lambda b, j: (0, b * nblk + j)),
            pl.BlockSpec((tb, M_V), lambda b, j: (b * nblk + j, 0)),
            pl.BlockSpec((tb, LANES), lambda b, j: (b * nblk + j, 0)),
            pl.BlockSpec((32, tb), lambda b, j: (0, b * nblk + j)),
            pl.BlockSpec((tb, M_V), lambda b, j: (b * nblk + j, 0)),
            pl.BlockSpec((8, M_DV), lambda b, j: (0, 0)),
        ],
        out_specs=pl.BlockSpec((tb, M_V), lambda b, j: (b * nblk + j, 0)),
        out_shape=jax.ShapeDtypeStruct((t, M_V), F32),
        scratch_shapes=[pltpu.VMEM((M_HEADS, M_DQK, 2 * M_DV), F32), pltpu.VMEM((8, LANES), F32)],
        compiler_params=_params(("arbitrary", "arbitrary")),
        name="mlstm",
    )(q, kT, v, small, smallT, og, onw)


def _nsa_kernel(qT_ref, kcmp_ref, vcmpT_ref, ks_ref, e_ref, vsT_ref, kw_ref, vwT_ref, gT_ref,
                o_ref, rhs_ref, ps_ref, s0_ref, s1_ref, c0_ref, c1_ref, m_ref, acc_ref, *, nsel, n_rounds):
    qi = pl.program_id(1)
    tq = Q_TILE
    nb = kcmp_ref.shape[0]
    nselp = e_ref.shape[1]
    q0 = qi * tq
    wide = N_HG * tq
    lane_w = lax.broadcasted_iota(jnp.int32, (1, wide), 1)
    tpos_w = q0 + (lane_w % tq)
    tpos = q0 + lax.broadcasted_iota(jnp.int32, (1, tq), 1)
    zeros_q = jnp.zeros((N_DH, wide), BF16)
    ones_v = jnp.ones((16, LANES), BF16)
    grows = [slice(g * N_DH, (g + 1) * N_DH) for g in range(N_KV)]
    gcols = [slice(g * wide, (g + 1) * wide) for g in range(N_KV)]

    def values(v_ref, first, count, rows):
        return jnp.concatenate(
            [jnp.concatenate([v_ref[first + j, rows, :], ones_v], axis=0) for j in range(count)], axis=1)

    start_w = pl.multiple_of(jnp.maximum(q0 - WINDOW, 0), LANES)
    sc, sw = [], []
    for g in range(N_KV):
        q4 = jnp.concatenate(
            [qT_ref[(g * N_HG + h) * N_DH:(g * N_HG + h + 1) * N_DH, :] for h in range(N_HG)], axis=1)
        qpad = jnp.concatenate([q4, zeros_q] if g == 0 else [zeros_q, q4], axis=0)
        rhs_ref[0:2 * N_DH, gcols[g]] = qpad
        sc.append(_dot(kcmp_ref[...], qpad))
        sw.append(_dot(kw_ref[pl.ds(start_w, WIN_SPAN), :], qpad))

    cend = lax.broadcasted_iota(jnp.int32, (nb, 1), 0) * CMP_STRIDE + (CMP_LEN - 1)
    cmask = cend <= tpos_w
    any_visible = tpos_w >= CMP_LEN - 1
    ratio = SEL_BLOCK // CMP_STRIDE
    o_cmp, imp = [], []
    for g in range(N_KV):
        s = jnp.where(cmask, sc[g], NEG)
        pc = jnp.exp2(s - jnp.max(s, axis=0, keepdims=True))
        lc = jnp.sum(pc, axis=0, keepdims=True)
        pc = pc * jnp.where(any_visible, 1.0 / lc, 0.0)
        o_cmp.append(_dot(vcmpT_ref[grows[g], :], pc.astype(BF16)))
        psum = pc[:, 0:tq]
        for h in range(1, N_HG):
            psum = psum + pc[:, h * tq:(h + 1) * tq]
        ps_ref[g, 0:8, :] = jnp.zeros((8, tq), F32)
        ps_ref[g, 8:8 + nb, :] = psum
        acc = None
        for k in range(-((CMP_LEN - 1) // CMP_STRIDE), ratio):
            part = ps_ref[g, pl.ds(8 + k, nsel, stride=ratio), :]
            acc = part if acc is None else acc + part
        imp.append(acc)

    jblk = lax.broadcasted_iota(jnp.int32, (nsel, tq), 0)
    cur = tpos // SEL_BLOCK
    forced = (jblk == 0) | (jblk == cur) | (jblk == cur - 1)
    cand = (jblk >= 1) & (jblk <= cur - 2)
    jblk_f = jblk.astype(F32)
    val = [jnp.where(cand, imp[g], -jnp.inf) for g in range(N_KV)]
    sel = [forced] * N_KV
    for _ in range(n_rounds):
        for g in range(N_KV):
            mx = jnp.max(val[g], axis=0, keepdims=True)
            first = jnp.min(jnp.where(val[g] == mx, jblk_f, float(nsel)), axis=0, keepdims=True)
            hit = jblk_f == first
            sel[g] = sel[g] | hit
            val[g] = jnp.where(hit, -jnp.inf, val[g])
    for g in range(N_KV):
        bias = jnp.where(sel[g], 0.0, NEG).astype(BF16)
        if nselp > nsel:
            bias = jnp.concatenate([bias, jnp.zeros((nselp - nsel, tq), BF16)], axis=0)
        rhs_ref[2 * N_DH:, gcols[g]] = jnp.concatenate([bias] * N_HG, axis=1)

    dist = (tpos_w - start_w) - lax.broadcasted_iota(jnp.int32, (WIN_SPAN, 1), 0)
    wmask = lax.bitcast_convert_type(dist, jnp.uint32) < WINDOW
    o_win = []
    for g in range(N_KV):
        s = jnp.where(wmask, sw[g], NEG)
        pw = jnp.exp2(s - jnp.max(s, axis=0, keepdims=True))
        ow = _dot(values(vwT_ref, start_w // LANES, WIN_SPAN // LANES, grows[g]), pw.astype(BF16))
        o_win.append(ow[0:N_DH, :] * (1.0 / ow[N_DH:N_DH + 1, :]))

    def scores(kt, g):
        start = pl.multiple_of(kt * KEY_TILE, KEY_TILE)
        lhs = jnp.concatenate([ks_ref[pl.ds(start, KEY_TILE), :], e_ref[pl.ds(start, KEY_TILE), :]], axis=1)
        return _dot(lhs, rhs_ref[:, g * wide:(g + 1) * wide])

    def produce(kt, s_ref, c_ref):
        for g in range(N_KV):
            s = scores(kt, g)
            s_ref[g] = s
            c_ref[g] = jnp.max(s, axis=0, keepdims=True)

    def consume(kt, s_ref, c_ref, causal_tile=False):
        for g in range(N_KV):
            s = s_ref[g]
            if causal_tile:
                kpos = kt * KEY_TILE + lax.broadcasted_iota(jnp.int32, (KEY_TILE, 1), 0)
                s = jnp.where(kpos <= tpos_w, s, NEG)
                smax = jnp.max(s, axis=0, keepdims=True)
            else:
                smax = c_ref[g]
            m = m_ref[g]
            m_new = jnp.maximum(m, smax)
            p = jnp.exp2(s - m_new).astype(BF16)
            vt = values(vsT_ref, kt * (KEY_TILE // LANES), KEY_TILE // LANES, slice(g * N_DH, (g + 1) * N_DH))
            acc_ref[g] = jnp.exp2(m - m_new) * acc_ref[g] + _dot(vt, p)
            m_ref[g] = m_new

    n_full = qi // (KEY_TILE // tq)
    odd = n_full % 2
    m_ref[...] = jnp.full(m_ref.shape, NEG, F32)
    acc_ref[...] = jnp.zeros(acc_ref.shape, F32)

    @pl.when(odd == 0)
    def _():
        produce(0, s0_ref, c0_ref)

    @pl.when(odd == 1)
    def _():
        produce(0, s1_ref, c1_ref)
        produce(1, s0_ref, c0_ref)
        consume(0, s1_ref, c1_ref)

    def pair(j, _):
        kt = odd + 2 * j
        produce(kt + 1, s1_ref, c1_ref)
        consume(kt, s0_ref, c0_ref)
        produce(kt + 2, s0_ref, c0_ref)
        consume(kt + 1, s1_ref, c1_ref)
        return 0

    lax.fori_loop(0, n_full // 2, pair, 0)
    consume(n_full, s0_ref, c0_ref, causal_tile=True)
    outs = []
    for g in range(N_KV):
        o_slc = acc_ref[g, 0:N_DH, :] * (1.0 / acc_ref[g, N_DH:N_DH + 1, :])
        for h in range(N_HG):
            cs = slice(h * tq, (h + 1) * tq)
            r = 2 * M_HEADS + (g * N_HG + h) * 3
            outs.append(gT_ref[r:r + 1, :] * o_cmp[g][:, cs] + gT_ref[r + 1:r + 2, :] * o_slc[:, cs]
                        + gT_ref[r + 2:r + 3, :] * o_win[g][:, cs])

    o_ref[...] = jnp.concatenate(outs, axis=0).T


def _nsa(qT, kcmp, vcmpT, ks, emap, vsT, kw, vwT, smallT, batch, seq):
    t = qT.shape[1]
    nq = seq // Q_TILE
    nb = kcmp.shape[1]
    nsel = seq // SEL_BLOCK
    nselp = emap.shape[1]
    wide = N_HG * Q_TILE
    n_rounds = max(min(SEL_TOPN, nsel) - 3, 0)
    kern = functools.partial(_nsa_kernel, nsel=nsel, n_rounds=n_rounds)
    return pl.pallas_call(
        kern,
        grid=(batch, nq),
        in_specs=[
            pl.BlockSpec((N_Q, Q_TILE), lambda b, i: (0, b * nq + i)),
            pl.BlockSpec((None, nb, N_KVW), lambda b, i: (b, 0, 0)),
            pl.BlockSpec((None, N_KVW, nb), lambda b, i: (b, 0, 0)),
            pl.BlockSpec((seq, N_KVW), lambda b, i: (b, 0)),
            pl.BlockSpec((seq, nselp), lambda b, i: (0, 0)),
            pl.BlockSpec((seq // LANES, N_KVW, LANES), lambda b, i: (b, 0, 0)),
            pl.BlockSpec((seq, N_KVW), lambda b, i: (b, 0)),
            pl.BlockSpec((seq // LANES, N_KVW, LANES), lambda b, i: (b, 0, 0)),
            pl.BlockSpec((32, Q_TILE), lambda b, i: (0, b * nq + i)),
        ],
        out_specs=pl.BlockSpec((Q_TILE, N_Q), lambda b, i: (b * nq + i, 0)),
        out_shape=jax.ShapeDtypeStruct((t, N_Q), F32),
        scratch_shapes=[
            pltpu.VMEM((2 * N_DH + nselp, N_KV * wide), BF16),
            pltpu.VMEM((N_KV, nb + 8, Q_TILE), F32),
            pltpu.VMEM((N_KV, KEY_TILE, wide), F32),
            pltpu.VMEM((N_KV, KEY_TILE, wide), F32),
            pltpu.VMEM((N_KV, 1, wide), F32),
            pltpu.VMEM((N_KV, 1, wide), F32),
            pltpu.VMEM((N_KV, 1, wide), F32),
            pltpu.VMEM((N_KV, N_DH + 16, wide), F32),
        ],
        compiler_params=_params(("arbitrary", "arbitrary")),
        name="nsa",
    )(qT, kcmp, vcmpT, ks, emap, vsT, kw, vwT, smallT)


def _merge_kernel(x_ref, n1w_ref, wg_ref, gb_ref, hm_ref, on_ref, wm_ref, wn_ref, wo_ref, o_ref):
    x = x_ref[...]
    hn = _rmsnorm_rows(x, n1w_ref[...]).astype(BF16)
    gm = _sigmoid(_dot(hn, wg_ref[:, 0:D_MODEL]) + gb_ref[0:1, :])
    gn = _sigmoid(_dot(hn, wg_ref[:, D_MODEL:]) + gb_ref[1:2, :])
    y = gm * _dot(hm_ref[...].astype(BF16), wm_ref[...]) + gn * _dot(on_ref[...].astype(BF16), wn_ref[...])
    o_ref[...] = x + _dot(y.astype(BF16), wo_ref[...])


def _merge(x2, n1w, wg, gb, hm, on, wm, wn, wo, tm):
    t = x2.shape[0]
    const = lambda i: (0, 0)
    return pl.pallas_call(
        _merge_kernel,
        grid=(t // tm,),
        in_specs=[
            pl.BlockSpec((tm, D_MODEL), lambda i: (i, 0)),
            pl.BlockSpec((1, D_MODEL), const),
            pl.BlockSpec((D_MODEL, 2 * D_MODEL), const),
            pl.BlockSpec((2, D_MODEL), const),
            pl.BlockSpec((tm, M_V), lambda i: (i, 0)),
            pl.BlockSpec((tm, N_Q), lambda i: (i, 0)),
            pl.BlockSpec((M_V, D_MODEL), const),
            pl.BlockSpec((N_Q, D_MODEL), const),
            pl.BlockSpec((D_MODEL, D_MODEL), const),
        ],
        out_specs=pl.BlockSpec((tm, D_MODEL), lambda i: (i, 0)),
        out_shape=jax.ShapeDtypeStruct((t, D_MODEL), F32),
        compiler_params=_params(("arbitrary",)),
        name="merge",
    )(x2, n1w, wg, gb, hm, on, wm, wn, wo)


def _ffn_kernel(x_ref, n2w_ref, wup_ref, cw_ref, cb_ref, wdn_ref, o_ref, buf_ref, *, tiles_per_seq):
    i = pl.program_id(0)
    tm = x_ref.shape[0]
    x = x_ref[...]
    hn = _rmsnorm_rows(x, n2w_ref[...]).astype(BF16)

    @pl.when(i % tiles_per_seq == 0)
    def _():
        buf_ref[0:8, :] = jnp.zeros((8, D_FF), F32)

    a = _dot(hn, wup_ref[:, 0:D_FF])
    buf_ref[8:8 + tm, :] = a
    acc = cb_ref[...] + cw_ref[FFN_CONV - 1:FFN_CONV, :] * a
    for k in range(FFN_CONV - 1):
        acc = acc + cw_ref[k:k + 1, :] * buf_ref[8 - (FFN_CONV - 1) + k:8 - (FFN_CONV - 1) + k + tm, :]
    buf_ref[0:8, :] = buf_ref[tm:tm + 8, :]
    v = _dot(hn, wup_ref[:, D_FF:])
    o_ref[...] = x + _dot((_gelu(acc) * v).astype(BF16), wdn_ref[...])


def _ffn(x2, n2w, wup, cw, cb, wdn, seq, tm):
    t = x2.shape[0]
    const = lambda i: (0, 0)
    kern = functools.partial(_ffn_kernel, tiles_per_seq=seq // tm)
    return pl.pallas_call(
        kern,
        grid=(t // tm,),
        in_specs=[
            pl.BlockSpec((tm, D_MODEL), lambda i: (i, 0)),
            pl.BlockSpec((1, D_MODEL), const),
            pl.BlockSpec((D_MODEL, 2 * D_FF), const, pipeline_mode=pl.Buffered(1)),
            pl.BlockSpec((FFN_CONV, D_FF), const),
            pl.BlockSpec((1, D_FF), const),
            pl.BlockSpec((D_FF, D_MODEL), const, pipeline_mode=pl.Buffered(1)),
        ],
        out_specs=pl.BlockSpec((tm, D_MODEL), lambda i: (i, 0)),
        out_shape=jax.ShapeDtypeStruct((t, D_MODEL), F32),
        scratch_shapes=[pltpu.VMEM((tm + 8, D_FF), F32)],
        compiler_params=_params(("arbitrary",)),
        name="ffn",
    )(x2, n2w, wup, cw, cb, wdn)


def _cols(w, *names):
    return jnp.concatenate([w[:, _OFF[n][0]:_OFF[n][1]] for n in names], axis=1)


def _layer(x, n1w, w_in, m_conv_w, m_conv_b, m_igate_b, m_fgate_b, m_out_norm_w,
           q_norm_w, kcmp_norm_w, kslc_norm_w, kwin_norm_w,
           cmp_k_pe, cmp_k_w1, cmp_k_w2, cmp_v_pe, cmp_v_w1, cmp_v_w2,
           w_up_m, w_up_n, merge_gate_b, w_out, norm2_w, ffn_w_up, ffn_conv_w, ffn_conv_b, ffn_w_down):
    batch, seq, _ = x.shape
    t = batch * seq
    x2 = x.reshape(t, D_MODEL)
    n1w2 = n1w.reshape(1, D_MODEL)
    tm = 256

    small_pad = jnp.zeros((D_MODEL, LANES - 2 * M_HEADS - 3 * N_HEADS), F32)
    w_m = jnp.concatenate([_cols(w_in, "mq", "mk", "mv", "mo", "mi", "mf", "ng"), small_pad], axis=1).astype(BF16)
    w_n = _cols(w_in, "nq", "kc", "vc", "ks", "vs", "kw", "vw").astype(BF16)
    w_g = _cols(w_in, "gm", "gn").astype(BF16)
    sbias = jnp.concatenate([m_igate_b, m_fgate_b, jnp.zeros((LANES - 2 * M_HEADS,), F32)]).reshape(1, LANES)

    half = N_DH // 2
    pos = jnp.arange(seq, dtype=F32)
    inv = ROPE_THETA ** (-jnp.arange(0, N_DH, 2, dtype=F32) / N_DH)
    ang = pos[:, None] * inv[None, :]
    cos, sin = jnp.cos(ang), jnp.sin(ang)
    cosn = jnp.tile(cos, (1, N_KVW // half))
    sinn = jnp.tile(jnp.concatenate([-sin, sin], axis=1), (1, N_KV))
    cosT, sinT = cos.T, sin.T
    knw = jnp.zeros((8, N_KVW), F32).at[0:3].set(
        jnp.stack([jnp.tile(w, N_KV) for w in (kcmp_norm_w, kslc_norm_w, kwin_norm_w)]))

    q_m, kT_m, v_m, og, small, smallT = _proj_m(
        x2, n1w2, w_m, m_conv_w, m_conv_b.reshape(1, -1), sbias, seq, tm)
    qT, kc, vc, ks, kw, vsT, vwT = _proj_n(
        x2, n1w2, w_n, q_norm_w.reshape(N_DH, 1), knw, cosn, sinn, cosT, sinT, seq, tm)

    nb = seq // CMP_STRIDE

    def to_blocks(a):
        a = a.reshape(batch, nb, CMP_STRIDE, N_KV, N_DH).transpose(0, 3, 1, 2, 4)
        a = a.reshape(batch * N_KV, nb, CMP_STRIDE * N_DH)
        return jnp.pad(a, ((0, 0), (0, 8), (0, 0)))

    xkv = jnp.stack([to_blocks(kc), to_blocks(vc)])
    pe = jnp.stack([cmp_k_pe.reshape(1, -1), cmp_v_pe.reshape(1, -1)])
    w1 = jnp.stack([cmp_k_w1, cmp_v_w1]).astype(BF16)
    w2 = jnp.stack([cmp_k_w2, cmp_v_w2]).astype(BF16)
    cmp = _compress(xkv, pe, w1, w2).reshape(2, batch, N_KV, nb, N_DH)
    kcmp = cmp[0].transpose(0, 2, 1, 3).reshape(batch, nb, N_KVW).astype(BF16)
    vcmpT = cmp[1].transpose(0, 1, 3, 2).reshape(batch, N_KVW, nb).astype(BF16)

    onw = jnp.zeros((8, M_DV), F32).at[0:M_HEADS].set(m_out_norm_w)
    hm = _mlstm(q_m, kT_m, v_m, small, smallT, og, onw, batch, seq, 4 * MLSTM_CHUNK)
    nselp = -(-(seq // SEL_BLOCK) // LANES) * LANES
    emap = (np.arange(seq)[:, None] // SEL_BLOCK == np.arange(nselp)[None, :]).astype(np.float32)
    on = _nsa(qT, kcmp, vcmpT, ks, jnp.asarray(emap, dtype=BF16), vsT, kw, vwT, smallT, batch, seq)

    x1 = _merge(x2, n1w2, w_g, merge_gate_b, hm, on, w_up_m.astype(BF16), w_up_n.astype(BF16),
                w_out.astype(BF16), tm)
    out = _ffn(x1, norm2_w.reshape(1, D_MODEL), ffn_w_up.astype(BF16), ffn_conv_w, ffn_conv_b.reshape(1, -1),
               ffn_w_down.astype(BF16), seq, tm)
    return out.reshape(batch, seq, D_MODEL)


def kernel(x, norm1_w, w_in, m_conv_w, m_conv_b, m_igate_b, m_fgate_b, m_out_norm_w, q_norm_w, kcmp_norm_w,
           kslc_norm_w, kwin_norm_w, cmp_k_pe, cmp_k_w1, cmp_k_w2, cmp_v_pe, cmp_v_w1, cmp_v_w2, w_up_m, w_up_n,
           merge_gate_b, w_out, norm2_w, ffn_w_up, ffn_conv_w, ffn_conv_b, ffn_w_down):
    params = (norm1_w, w_in, m_conv_w, m_conv_b, m_igate_b, m_fgate_b, m_out_norm_w, q_norm_w, kcmp_norm_w,
              kslc_norm_w, kwin_norm_w, cmp_k_pe, cmp_k_w1, cmp_k_w2, cmp_v_pe, cmp_v_w1, cmp_v_w2, w_up_m, w_up_n,
              merge_gate_b, w_out, norm2_w, ffn_w_up, ffn_conv_w, ffn_conv_b, ffn_w_down)
    for layer in range(norm1_w.shape[0]):
        x = _layer(x, *[p[layer] for p in params])
    return x
```

```python
import functools
import math

import jax
import jax.numpy as jnp
import numpy as np
from jax import lax
from jax.experimental import pallas as pl
from jax.experimental.pallas import tpu as pltpu

D_MODEL = 1024
EPS = 1e-6
ROPE_THETA = 10000.0
NEG = -1e30
M_HEADS = 4
M_DQK = 64
M_DV = 128
M_CONV = 4
M_QK = M_HEADS * M_DQK
M_V = M_HEADS * M_DV
N_HEADS = 8
N_KV = 2
N_HG = N_HEADS // N_KV
N_DH = 64
N_Q = N_HEADS * N_DH
N_KVW = N_KV * N_DH
CMP_LEN = 32
CMP_STRIDE = 16
CMP_HIDDEN = 256
SEL_BLOCK = 64
SEL_TOPN = 16
WINDOW = 512
D_FF = 2816
FFN_CONV = 3

_OFF = {}
_o = 0
for _name, _size in (("mq", M_QK), ("mk", M_QK), ("mv", M_V), ("mo", M_V), ("mi", M_HEADS), ("mf", M_HEADS),
                     ("nq", N_Q), ("kc", N_KVW), ("vc", N_KVW), ("ks", N_KVW), ("vs", N_KVW), ("kw", N_KVW),
                     ("vw", N_KVW), ("ng", 3 * N_HEADS), ("gm", D_MODEL), ("gn", D_MODEL)):
    _OFF[_name] = (_o, _o + _size)
    _o += _size

LANES = 128
MLSTM_CHUNK = 128
Q_TILE = 128
KEY_TILE = 256
WIN_SPAN = WINDOW + Q_TILE
VMEM_LIMIT = 56 * 1024 * 1024

LOG2E = math.log2(math.e)

F32 = jnp.float32
BF16 = jnp.bfloat16
HIGHEST = lax.Precision.HIGHEST


def _dot(a, b):
    return jnp.dot(a, b, preferred_element_type=F32)


def _rmsnorm_rows(x, w):
    return x * lax.rsqrt(jnp.mean(x * x, axis=-1, keepdims=True) + EPS) * w


def _sigmoid(x):
    return 1.0 / (1.0 + jnp.exp(-x))


def _gelu(x):
    return 0.5 * x * (1.0 + lax.erf(x * (1.0 / math.sqrt(2.0))))


def _params(sem):
    return pltpu.CompilerParams(dimension_semantics=sem, vmem_limit_bytes=VMEM_LIMIT)


def _proj_m_kernel(x_ref, n1w_ref, w_ref, cw_ref, cb_ref, sb_ref,
                   q_ref, kT_ref, v_ref, og_ref, small_ref, smallT_ref, buf_ref, *, tiles_per_seq):
    i = pl.program_id(0)
    tm = x_ref.shape[0]
    hn = _rmsnorm_rows(x_ref[...], n1w_ref[...]).astype(BF16)

    @pl.when(i % tiles_per_seq == 0)
    def _():
        buf_ref[0:8, :] = jnp.zeros((8, 2 * M_QK), F32)

    qk = _dot(hn, w_ref[:, 0:2 * M_QK])
    buf_ref[8:8 + tm, :] = qk
    acc = cb_ref[...] + cw_ref[M_CONV - 1:M_CONV, :] * qk
    for k in range(M_CONV - 1):
        acc = acc + cw_ref[k:k + 1, :] * buf_ref[8 - (M_CONV - 1) + k:8 - (M_CONV - 1) + k + tm, :]
    buf_ref[0:8, :] = buf_ref[tm:tm + 8, :]
    act = acc * _sigmoid(acc)
    q_ref[...] = (act[:, 0:M_QK] * (M_DQK ** -0.5)).astype(BF16)
    kT_ref[...] = act[:, M_QK:2 * M_QK].T.astype(BF16)
    v_ref[...] = _dot(hn, w_ref[:, 2 * M_QK:2 * M_QK + M_V]).astype(BF16)
    og_ref[...] = _sigmoid(_dot(hn, w_ref[:, 2 * M_QK + M_V:2 * M_QK + 2 * M_V]))
    sm = _dot(hn, w_ref[:, 2 * M_QK + 2 * M_V:]) + sb_ref[...]
    lane = lax.broadcasted_iota(jnp.int32, sm.shape, 1)
    logsig = jnp.minimum(sm, 0.0) - jnp.log1p(jnp.exp(-jnp.abs(sm)))
    sm = jnp.where(lane < M_HEADS, sm, jnp.where(lane < 2 * M_HEADS, logsig, _sigmoid(sm)))
    small_ref[...] = sm
    smallT_ref[...] = sm.T[0:32, :]


def _proj_m(x2, n1w, w, cw, cb, sb, seq, tm):
    t = x2.shape[0]
    ncol = w.shape[1]
    kern = functools.partial(_proj_m_kernel, tiles_per_seq=seq // tm)
    return pl.pallas_call(
        kern,
        grid=(t // tm,),
        in_specs=[
            pl.BlockSpec((tm, D_MODEL), lambda i: (i, 0)),
            pl.BlockSpec((1, D_MODEL), lambda i: (0, 0)),
            pl.BlockSpec((D_MODEL, ncol), lambda i: (0, 0)),
            pl.BlockSpec((M_CONV, 2 * M_QK), lambda i: (0, 0)),
            pl.BlockSpec((1, 2 * M_QK), lambda i: (0, 0)),
            pl.BlockSpec((1, LANES), lambda i: (0, 0)),
        ],
        out_specs=[
            pl.BlockSpec((tm, M_QK), lambda i: (i, 0)),
            pl.BlockSpec((M_QK, tm), lambda i: (0, i)),
            pl.BlockSpec((tm, M_V), lambda i: (i, 0)),
            pl.BlockSpec((tm, M_V), lambda i: (i, 0)),
            pl.BlockSpec((tm, LANES), lambda i: (i, 0)),
            pl.BlockSpec((32, tm), lambda i: (0, i)),
        ],
        out_shape=[
            jax.ShapeDtypeStruct((t, M_QK), BF16),
            jax.ShapeDtypeStruct((M_QK, t), BF16),
            jax.ShapeDtypeStruct((t, M_V), BF16),
            jax.ShapeDtypeStruct((t, M_V), F32),
            jax.ShapeDtypeStruct((t, LANES), F32),
            jax.ShapeDtypeStruct((32, t), F32),
        ],
        scratch_shapes=[pltpu.VMEM((tm + 8, 2 * M_QK), F32)],
        compiler_params=_params(("arbitrary",)),
        name="proj_m",
    )(x2, n1w, w, cw, cb, sb)


def _proj_n_kernel(x_ref, n1w_ref, w_ref, qnw_ref, knw_ref, cosn_ref, sinn_ref, cosT_ref, sinT_ref,
                   qT_ref, kvc_ref, ks_ref, kw_ref, vsT_ref, vwT_ref):
    tm = x_ref.shape[0]
    hn = _rmsnorm_rows(x_ref[...], n1w_ref[...]).astype(BF16)
    qT = _dot(hn, w_ref[:, 0:N_Q]).T
    cosT = cosT_ref[...]
    sinT = sinT_ref[...]
    qnw = qnw_ref[...]
    half = N_DH // 2
    for h in range(N_HEADS):
        xh = qT[h * N_DH:(h + 1) * N_DH, :]
        xn = xh * lax.rsqrt(jnp.mean(xh * xh, axis=0, keepdims=True) + EPS) * qnw
        x1 = xn[0:half, :]
        x2 = xn[half:, :]
        o = jnp.concatenate([x1 * cosT - x2 * sinT, x2 * cosT + x1 * sinT], axis=0) * (LOG2E * N_DH ** -0.5)
        qT_ref[h * N_DH:(h + 1) * N_DH, :] = o.astype(BF16)

    cosn = cosn_ref[...]
    sinn = sinn_ref[...]
    li = lax.broadcasted_iota(jnp.int32, (N_KVW, N_KVW), 0)
    lj = lax.broadcasted_iota(jnp.int32, (N_KVW, N_KVW), 1)
    head_sum = (li // N_DH == lj // N_DH).astype(BF16)
    swap_half = (lj == li + jnp.where(li % N_DH < half, half, -half)).astype(BF16)

    def lane_map(v, m01):
        hi = v.astype(BF16)
        lo = (v - hi.astype(F32)).astype(BF16)
        return _dot(hi, m01) + _dot(lo, m01)

    c0 = N_Q
    k_raw = [_dot(hn, w_ref[:, c0 + 2 * i * N_KVW:c0 + (2 * i + 1) * N_KVW]) for i in range(3)]
    vc, vs, vw = [_dot(hn, w_ref[:, c0 + (2 * i + 1) * N_KVW:c0 + (2 * i + 2) * N_KVW]) for i in range(3)]
    ms = [lane_map(k * k, head_sum) * (1.0 / N_DH) for k in k_raw]
    kn = [k_raw[i] * lax.rsqrt(ms[i] + EPS) * knw_ref[i:i + 1, :] for i in range(3)]
    kc, ks, kw = [kn[i] * cosn + lane_map(kn[i], swap_half) * sinn for i in range(3)]
    for g in range(N_KV):
        kvc_ref[g] = kc[:, g * N_DH:(g + 1) * N_DH]
        kvc_ref[N_KV + g] = vc[:, g * N_DH:(g + 1) * N_DH]
    ks_ref[...] = ks.astype(BF16)
    kw_ref[...] = kw.astype(BF16)
    vsT = vs.T.astype(BF16)
    vwT = vw.T.astype(BF16)
    for j in range(tm // LANES):
        vsT_ref[j] = vsT[:, j * LANES:(j + 1) * LANES]
        vwT_ref[j] = vwT[:, j * LANES:(j + 1) * LANES]


def _proj_n(x2, n1w, w, qnw, knw, cosn, sinn, cosT, sinT, seq, tm):
    t = x2.shape[0]
    ncol = w.shape[1]
    tps = seq // tm
    half = N_DH // 2
    return pl.pallas_call(
        _proj_n_kernel,
        grid=(t // tm,),
        in_specs=[
            pl.BlockSpec((tm, D_MODEL), lambda i: (i, 0)),
            pl.BlockSpec((1, D_MODEL), lambda i: (0, 0)),
            pl.BlockSpec((D_MODEL, ncol), lambda i: (0, 0)),
            pl.BlockSpec((N_DH, 1), lambda i: (0, 0)),
            pl.BlockSpec((8, N_KVW), lambda i: (0, 0)),
            pl.BlockSpec((tm, N_KVW), lambda i: (i % tps, 0)),
            pl.BlockSpec((tm, N_KVW), lambda i: (i % tps, 0)),
            pl.BlockSpec((half, tm), lambda i: (0, i % tps)),
            pl.BlockSpec((half, tm), lambda i: (0, i % tps)),
        ],
        out_specs=[
            pl.BlockSpec((N_Q, tm), lambda i: (0, i)),
            pl.BlockSpec((2 * N_KV, tm, N_DH), lambda i: (0, i, 0)),
            pl.BlockSpec((tm, N_KVW), lambda i: (i, 0)),
            pl.BlockSpec((tm, N_KVW), lambda i: (i, 0)),
            pl.BlockSpec((tm // LANES, N_KVW, LANES), lambda i: (i, 0, 0)),
            pl.BlockSpec((tm // LANES, N_KVW, LANES), lambda i: (i, 0, 0)),
        ],
        out_shape=[
            jax.ShapeDtypeStruct((N_Q, t), BF16),
            jax.ShapeDtypeStruct((2 * N_KV, t, N_DH), F32),
            jax.ShapeDtypeStruct((t, N_KVW), BF16),
            jax.ShapeDtypeStruct((t, N_KVW), BF16),
            jax.ShapeDtypeStruct((t // LANES, N_KVW, LANES), BF16),
            jax.ShapeDtypeStruct((t // LANES, N_KVW, LANES), BF16),
        ],
        compiler_params=_params(("arbitrary",)),
        name="proj_n",
    )(x2, n1w, w, qnw, knw, cosn, sinn, cosT, sinT)


def _compress_kernel(x_ref, pe_ref, w1_ref, w2_ref, o_ref):
    nb = o_ref.shape[2]
    half = (CMP_LEN // 2) * N_DH
    pe = pe_ref[0]
    x = x_ref[0, 0]
    first = _dot((x + pe[:, 0:half]).astype(BF16), w1_ref[0, 0:half, :])
    second = _dot((x + pe[:, half:]).astype(BF16), w1_ref[0, half:, :])
    hid = first + pltpu.roll(second, nb - 1, 0)
    o_ref[0, 0] = _dot(_gelu(hid).astype(BF16), w2_ref[0])


def _compress(xkv, pe, w1, w2):
    na, batch, nb, width = xkv.shape
    return pl.pallas_call(
        _compress_kernel,
        grid=(na, batch),
        in_specs=[
            pl.BlockSpec((1, 1, nb, width), lambda a, b: (a, b, 0, 0)),
            pl.BlockSpec((1, 1, CMP_LEN * N_DH), lambda a, b: (a // N_KV, 0, 0)),
            pl.BlockSpec((1, CMP_LEN * N_DH, CMP_HIDDEN), lambda a, b: (a // N_KV, 0, 0)),
            pl.BlockSpec((1, CMP_HIDDEN, N_DH), lambda a, b: (a // N_KV, 0, 0)),
        ],
        out_specs=pl.BlockSpec((1, 1, nb, N_DH), lambda a, b: (a, b, 0, 0)),
        out_shape=jax.ShapeDtypeStruct((na, batch, nb, N_DH), F32),
        compiler_params=_params(("arbitrary", "arbitrary")),
        name="compress",
    )(xkv, pe, w1, w2)


def _mlstm_kernel(q_ref, kT_ref, v_ref, small_ref, smallT_ref, og_ref, onw_ref, o_ref, c_ref, m_ref):
    L = MLSTM_CHUNK
    tb = q_ref.shape[0]

    @pl.when(pl.program_id(1) == 0)
    def _():
        c_ref[...] = jnp.zeros(c_ref.shape, F32)
        m_ref[...] = jnp.zeros(m_ref.shape, F32)

    row = lax.broadcasted_iota(jnp.int32, (L, L), 0)
    col = lax.broadcasted_iota(jnp.int32, (L, L), 1)
    causal = col <= row
    tril = causal.astype(F32)
    triu = (row <= col).astype(F32)
    ones_col = (col == 0).astype(BF16)
    zeros_c = jnp.zeros((M_DQK, 2 * M_DV), BF16)

    for c in range(tb // L):
        sl = slice(c * L, (c + 1) * L)
        sm = small_ref[sl, :]
        smT = smallT_ref[0:8, sl]
        bcol_all = jnp.dot(tril, sm, preferred_element_type=F32, precision=HIGHEST)
        brow_all = jnp.dot(smT, triu, preferred_element_type=F32, precision=HIGHEST)
        for h in range(M_HEADS):
            b_col = bcol_all[:, M_HEADS + h:M_HEADS + h + 1]
            b_row = brow_all[M_HEADS + h:M_HEADS + h + 1, :]
            i_row = smT[h:h + 1, :]
            g = b_row[:, L - 1:L]
            m_prev = m_ref[h:h + 1, 0:1]
            d = jnp.where(causal, b_col - b_row + i_row, NEG)
            inter = b_col + m_prev
            m_t = jnp.maximum(inter, jnp.max(d, axis=1, keepdims=True))
            dw = jnp.exp(d - m_t)
            pair = h // 2
            q_pair = q_ref[sl, pair * LANES:(pair + 1) * LANES]
            qm = jnp.where((col // M_DQK) == (h % 2), q_pair, jnp.zeros_like(q_pair))
            kT_pair = kT_ref[pair * LANES:(pair + 1) * LANES, sl]
            s = _dot(qm, kT_pair)
            sw = (s * dw).astype(BF16)
            v_aug = jnp.concatenate([v_ref[sl, h * M_DV:(h + 1) * M_DV], ones_col], axis=1)
            c_h = c_ref[h]
            c_bf = c_h.astype(BF16)
            c_pad = jnp.concatenate([c_bf, zeros_c] if h % 2 == 0 else [zeros_c, c_bf], axis=0)
            isc = jnp.exp(inter - m_t)
            nd = isc * _dot(qm, c_pad) + _dot(sw, v_aug)
            num = nd[:, 0:M_DV]
            den = nd[:, M_DV:M_DV + 1]
            hh = num / jnp.maximum(jnp.abs(den), jnp.exp(-m_t))
            a_row = g - b_row + i_row
            a_max = jnp.max(a_row, axis=1, keepdims=True)
            w_row = jnp.exp(a_row - a_max)
            kT_h = kT_ref[h * M_DQK:(h + 1) * M_DQK, sl].astype(F32)
            dc = _dot((kT_h * w_row).astype(BF16), v_aug)
            m_new = jnp.maximum(g + m_prev, a_max)
            c_ref[h] = jnp.exp(g + m_prev - m_new) * c_h + jnp.exp(a_max - m_new) * dc
            m_ref[h:h + 1, :] = jnp.broadcast_to(m_new, (1, LANES))
            hn = _rmsnorm_rows(hh, onw_ref[h:h + 1, :])
            o_ref[sl, h * M_DV:(h + 1) * M_DV] = hn * og_ref[sl, h * M_DV:(h + 1) * M_DV]


def _mlstm(q, kT, v, small, smallT, og, onw, batch, seq, tb):
    t = q.shape[0]
    nblk = seq // tb
    return pl.pallas_call(
        _mlstm_kernel,
        grid=(batch, nblk),
        in_specs=[
            pl.BlockSpec((tb, M_QK), lambda b, j: (b * nblk + j, 0)),
            pl.BlockSpec((M_QK, tb), lambda b, j: (0, b * nblk + j)),
            pl.BlockSpec((tb, M_V), lambda b, j: (b * nblk + j, 0)),
            pl.BlockSpec((tb, LANES), lambda b, j: (b * nblk + j, 0)),
            pl.BlockSpec((32, tb), lambda b, j: (0, b * nblk + j)),
            pl.BlockSpec((tb, M_V), lambda b, j: (b * nblk + j, 0)),
            pl.BlockSpec((8, M_DV), lambda b, j: (0, 0)),
        ],
        out_specs=pl.BlockSpec((tb, M_V), lambda b, j: (b * nblk + j, 0)),
        out_shape=jax.ShapeDtypeStruct((t, M_V), F32),
        scratch_shapes=[pltpu.VMEM((M_HEADS, M_DQK, 2 * M_DV), F32), pltpu.VMEM((8, LANES), F32)],
        compiler_params=_params(("arbitrary", "arbitrary")),
        name="mlstm",
    )(q, kT, v, small, smallT, og, onw)


def _nsa_kernel(qT_ref, kcmp_ref, vcmpT_ref, ks_ref, e_ref, vsT_ref, kw_ref, vwT_ref, gT_ref,
                o_ref, rhs_ref, ps_ref, s0_ref, s1_ref, c0_ref, c1_ref, m_ref, acc_ref, *, nsel, n_rounds):
    qi = pl.program_id(1)
    tq = Q_TILE
    nb = kcmp_ref.shape[0]
    nselp = e_ref.shape[1]
    q0 = qi * tq
    wide = N_HG * tq
    lane_w = lax.broadcasted_iota(jnp.int32, (1, wide), 1)
    tpos_w = q0 + (lane_w % tq)
    tpos = q0 + lax.broadcasted_iota(jnp.int32, (1, tq), 1)
    zeros_q = jnp.zeros((N_DH, wide), BF16)
    ones_v = jnp.ones((16, LANES), BF16)
    grows = [slice(g * N_DH, (g + 1) * N_DH) for g in range(N_KV)]
    gcols = [slice(g * wide, (g + 1) * wide) for g in range(N_KV)]

    def values(v_ref, first, count, rows):
        return jnp.concatenate(
            [jnp.concatenate([v_ref[first + j, rows, :], ones_v], axis=0) for j in range(count)], axis=1)

    start_w = pl.multiple_of(jnp.maximum(q0 - WINDOW, 0), LANES)
    sc, sw = [], []
    for g in range(N_KV):
        q4 = jnp.concatenate(
            [qT_ref[(g * N_HG + h) * N_DH:(g * N_HG + h + 1) * N_DH, :] for h in range(N_HG)], axis=1)
        qpad = jnp.concatenate([q4, zeros_q] if g == 0 else [zeros_q, q4], axis=0)
        rhs_ref[0:2 * N_DH, gcols[g]] = qpad
        sc.append(_dot(kcmp_ref[...], qpad))
        sw.append(_dot(kw_ref[pl.ds(start_w, WIN_SPAN), :], qpad))

    cend = lax.broadcasted_iota(jnp.int32, (nb, 1), 0) * CMP_STRIDE + (CMP_LEN - 1)
    cmask = cend <= tpos_w
    any_visible = tpos_w >= CMP_LEN - 1
    ratio = SEL_BLOCK // CMP_STRIDE
    o_cmp, imp = [], []
    for g in range(N_KV):
        s = jnp.where(cmask, sc[g], NEG)
        pc = jnp.exp2(s - jnp.max(s, axis=0, keepdims=True))
        lc = jnp.sum(pc, axis=0, keepdims=True)
        pc = pc * jnp.where(any_visible, 1.0 / lc, 0.0)
        o_cmp.append(_dot(vcmpT_ref[grows[g], :], pc.astype(BF16)))
        psum = pc[:, 0:tq]
        for h in range(1, N_HG):
            psum = psum + pc[:, h * tq:(h + 1) * tq]
        ps_ref[g, 0:8, :] = jnp.zeros((8, tq), F32)
        ps_ref[g, 8:8 + nb, :] = psum
        acc = None
        for k in range(-((CMP_LEN - 1) // CMP_STRIDE), ratio):
            part = ps_ref[g, pl.ds(8 + k, nsel, stride=ratio), :]
            acc = part if acc is None else acc + part
        imp.append(acc)

    jblk = lax.broadcasted_iota(jnp.int32, (nsel, tq), 0)
    cur = tpos // SEL_BLOCK
    forced = (jblk == 0) | (jblk == cur) | (jblk == cur - 1)
    cand = (jblk >= 1) & (jblk <= cur - 2)
    jblk_f = jblk.astype(F32)
    val = [jnp.where(cand, imp[g], -jnp.inf) for g in range(N_KV)]
    sel = [forced] * N_KV
    for _ in range(n_rounds):
        for g in range(N_KV):
            mx = jnp.max(val[g], axis=0, keepdims=True)
            first = jnp.min(jnp.where(val[g] == mx, jblk_f, float(nsel)), axis=0, keepdims=True)
            hit = jblk_f == first
            sel[g] = sel[g] | hit
            val[g] = jnp.where(hit, -jnp.inf, val[g])
    for g in range(N_KV):
        bias = jnp.where(sel[g], 0.0, NEG).astype(BF16)
        if nselp > nsel:
            bias = jnp.concatenate([bias, jnp.zeros((nselp - nsel, tq), BF16)], axis=0)
        rhs_ref[2 * N_DH:, gcols[g]] = jnp.concatenate([bias] * N_HG, axis=1)

    dist = (tpos_w - start_w) - lax.broadcasted_iota(jnp.int32, (WIN_SPAN, 1), 0)
    wmask = lax.bitcast_convert_type(dist, jnp.uint32) < WINDOW

    def scores(kt, g):
        start = pl.multiple_of(kt * KEY_TILE, KEY_TILE)
        lhs = jnp.concatenate([ks_ref[pl.ds(start, KEY_TILE), :], e_ref[pl.ds(start, KEY_TILE), :]], axis=1)
        return _dot(lhs, rhs_ref[:, g * wide:(g + 1) * wide])

    def produce(kt, s_ref, c_ref):
        for g in range(N_KV):
            s = scores(kt, g)
            s_ref[g] = s
            c_ref[g] = jnp.max(s, axis=0, keepdims=True)

    def consume(kt, s_ref, c_ref, causal_tile=False):
        for g in range(N_KV):
            s = s_ref[g]
            if causal_tile:
                kpos = kt * KEY_TILE + lax.broadcasted_iota(jnp.int32, (KEY_TILE, 1), 0)
                s = jnp.where(kpos <= tpos_w, s, NEG)
                smax = jnp.max(s, axis=0, keepdims=True)
            else:
                smax = c_ref[g]
            m = m_ref[g]
            m_new = jnp.maximum(m, smax)
            p = jnp.exp2(s - m_new).astype(BF16)
            vt = values(vsT_ref, kt * (KEY_TILE // LANES), KEY_TILE // LANES, slice(g * N_DH, (g + 1) * N_DH))
            acc_ref[g] = jnp.exp2(m - m_new) * acc_ref[g] + _dot(vt, p)
            m_ref[g] = m_new

    n_full = qi // (KEY_TILE // tq)
    odd = n_full % 2
    m_ref[...] = jnp.full(m_ref.shape, NEG, F32)
    acc_ref[...] = jnp.zeros(acc_ref.shape, F32)
    produce(0, s0_ref, c0_ref)

    o_win = []
    for g in range(N_KV):
        s = jnp.where(wmask, sw[g], NEG)
        pw = jnp.exp2(s - jnp.max(s, axis=0, keepdims=True))
        ow = _dot(values(vwT_ref, start_w // LANES, WIN_SPAN // LANES, grows[g]), pw.astype(BF16))
        o_win.append(ow[0:N_DH, :] * (1.0 / ow[N_DH:N_DH + 1, :]))

    def pair(j, _):
        kt = 2 * j
        produce(kt + 1, s1_ref, c1_ref)
        consume(kt, s0_ref, c0_ref)
        produce(kt + 2, s0_ref, c0_ref)
        consume(kt + 1, s1_ref, c1_ref)
        return 0

    lax.fori_loop(0, n_full // 2, pair, 0)

    @pl.when(odd == 0)
    def _():
        consume(n_full, s0_ref, c0_ref, causal_tile=True)

    @pl.when(odd == 1)
    def _():
        produce(n_full, s1_ref, c1_ref)
        consume(n_full - 1, s0_ref, c0_ref)
        consume(n_full, s1_ref, c1_ref, causal_tile=True)

    outs = []
    for g in range(N_KV):
        o_slc = acc_ref[g, 0:N_DH, :] * (1.0 / acc_ref[g, N_DH:N_DH + 1, :])
        for h in range(N_HG):
            cs = slice(h * tq, (h + 1) * tq)
            r = 2 * M_HEADS + (g * N_HG + h) * 3
            outs.append(gT_ref[r:r + 1, :] * o_cmp[g][:, cs] + gT_ref[r + 1:r + 2, :] * o_slc[:, cs]
                        + gT_ref[r + 2:r + 3, :] * o_win[g][:, cs])

    o_ref[...] = jnp.concatenate(outs, axis=0).T


def _nsa(qT, kcmp, vcmpT, ks, emap, vsT, kw, vwT, smallT, batch, seq):
    t = qT.shape[1]
    nq = seq // Q_TILE
    nb = kcmp.shape[1]
    nsel = seq // SEL_BLOCK
    nselp = emap.shape[1]
    wide = N_HG * Q_TILE
    n_rounds = max(min(SEL_TOPN, nsel) - 3, 0)
    kern = functools.partial(_nsa_kernel, nsel=nsel, n_rounds=n_rounds)
    return pl.pallas_call(
        kern,
        grid=(batch, nq),
        in_specs=[
            pl.BlockSpec((N_Q, Q_TILE), lambda b, i: (0, b * nq + i)),
            pl.BlockSpec((None, nb, N_KVW), lambda b, i: (b, 0, 0)),
            pl.BlockSpec((None, N_KVW, nb), lambda b, i: (b, 0, 0)),
            pl.BlockSpec((seq, N_KVW), lambda b, i: (b, 0)),
            pl.BlockSpec((seq, nselp), lambda b, i: (0, 0)),
            pl.BlockSpec((seq // LANES, N_KVW, LANES), lambda b, i: (b, 0, 0)),
            pl.BlockSpec((seq, N_KVW), lambda b, i: (b, 0)),
            pl.BlockSpec((seq // LANES, N_KVW, LANES), lambda b, i: (b, 0, 0)),
            pl.BlockSpec((32, Q_TILE), lambda b, i: (0, b * nq + i)),
        ],
        out_specs=pl.BlockSpec((Q_TILE, N_Q), lambda b, i: (b * nq + i, 0)),
        out_shape=jax.ShapeDtypeStruct((t, N_Q), F32),
        scratch_shapes=[
            pltpu.VMEM((2 * N_DH + nselp, N_KV * wide), BF16),
            pltpu.VMEM((N_KV, nb + 8, Q_TILE), F32),
            pltpu.VMEM((N_KV, KEY_TILE, wide), F32),
            pltpu.VMEM((N_KV, KEY_TILE, wide), F32),
            pltpu.VMEM((N_KV, 1, wide), F32),
            pltpu.VMEM((N_KV, 1, wide), F32),
            pltpu.VMEM((N_KV, 1, wide), F32),
            pltpu.VMEM((N_KV, N_DH + 16, wide), F32),
        ],
        compiler_params=_params(("arbitrary", "arbitrary")),
        name="nsa",
    )(qT, kcmp, vcmpT, ks, emap, vsT, kw, vwT, smallT)


def _merge_kernel(x_ref, n1w_ref, wg_ref, gb_ref, hm_ref, on_ref, wm_ref, wn_ref, wo_ref, o_ref):
    x = x_ref[...]
    hn = _rmsnorm_rows(x, n1w_ref[...]).astype(BF16)
    gm = _sigmoid(_dot(hn, wg_ref[:, 0:D_MODEL]) + gb_ref[0:1, :])
    gn = _sigmoid(_dot(hn, wg_ref[:, D_MODEL:]) + gb_ref[1:2, :])
    y = gm * _dot(hm_ref[...].astype(BF16), wm_ref[...]) + gn * _dot(on_ref[...].astype(BF16), wn_ref[...])
    o_ref[...] = x + _dot(y.astype(BF16), wo_ref[...])


def _merge(x2, n1w, wg, gb, hm, on, wm, wn, wo, tm):
    t = x2.shape[0]
    const = lambda i: (0, 0)
    return pl.pallas_call(
        _merge_kernel,
        grid=(t // tm,),
        in_specs=[
            pl.BlockSpec((tm, D_MODEL), lambda i: (i, 0)),
            pl.BlockSpec((1, D_MODEL), const),
            pl.BlockSpec((D_MODEL, 2 * D_MODEL), const),
            pl.BlockSpec((2, D_MODEL), const),
            pl.BlockSpec((tm, M_V), lambda i: (i, 0)),
            pl.BlockSpec((tm, N_Q), lambda i: (i, 0)),
            pl.BlockSpec((M_V, D_MODEL), const),
            pl.BlockSpec((N_Q, D_MODEL), const),
            pl.BlockSpec((D_MODEL, D_MODEL), const),
        ],
        out_specs=pl.BlockSpec((tm, D_MODEL), lambda i: (i, 0)),
        out_shape=jax.ShapeDtypeStruct((t, D_MODEL), F32),
        compiler_params=_params(("arbitrary",)),
        name="merge",
    )(x2, n1w, wg, gb, hm, on, wm, wn, wo)


def _ffn_kernel(x_ref, n2w_ref, wup_ref, cw_ref, cb_ref, wdn_ref, o_ref, buf_ref, *, tiles_per_seq):
    i = pl.program_id(0)
    tm = x_ref.shape[0]
    x = x_ref[...]
    hn = _rmsnorm_rows(x, n2w_ref[...]).astype(BF16)

    @pl.when(i % tiles_per_seq == 0)
    def _():
        buf_ref[0:8, :] = jnp.zeros((8, D_FF), F32)

    a = _dot(hn, wup_ref[:, 0:D_FF])
    buf_ref[8:8 + tm, :] = a
    acc = cb_ref[...] + cw_ref[FFN_CONV - 1:FFN_CONV, :] * a
    for k in range(FFN_CONV - 1):
        acc = acc + cw_ref[k:k + 1, :] * buf_ref[8 - (FFN_CONV - 1) + k:8 - (FFN_CONV - 1) + k + tm, :]
    buf_ref[0:8, :] = buf_ref[tm:tm + 8, :]
    v = _dot(hn, wup_ref[:, D_FF:])
    o_ref[...] = x + _dot((_gelu(acc) * v).astype(BF16), wdn_ref[...])


def _ffn(x2, n2w, wup, cw, cb, wdn, seq, tm):
    t = x2.shape[0]
    const = lambda i: (0, 0)
    kern = functools.partial(_ffn_kernel, tiles_per_seq=seq // tm)
    return pl.pallas_call(
        kern,
        grid=(t // tm,),
        in_specs=[
            pl.BlockSpec((tm, D_MODEL), lambda i: (i, 0)),
            pl.BlockSpec((1, D_MODEL), const),
            pl.BlockSpec((D_MODEL, 2 * D_FF), const, pipeline_mode=pl.Buffered(1)),
            pl.BlockSpec((FFN_CONV, D_FF), const),
            pl.BlockSpec((1, D_FF), const),
            pl.BlockSpec((D_FF, D_MODEL), const, pipeline_mode=pl.Buffered(1)),
        ],
        out_specs=pl.BlockSpec((tm, D_MODEL), lambda i: (i, 0)),
        out_shape=jax.ShapeDtypeStruct((t, D_MODEL), F32),
        scratch_shapes=[pltpu.VMEM((tm + 8, D_FF), F32)],
        compiler_params=_params(("arbitrary",)),
        name="ffn",
    )(x2, n2w, wup, cw, cb, wdn)


def _cols(w, *names):
    return jnp.concatenate([w[:, _OFF[n][0]:_OFF[n][1]] for n in names], axis=1)


def _layer(x, n1w, w_in, m_conv_w, m_conv_b, m_igate_b, m_fgate_b, m_out_norm_w,
           q_norm_w, kcmp_norm_w, kslc_norm_w, kwin_norm_w,
           cmp_k_pe, cmp_k_w1, cmp_k_w2, cmp_v_pe, cmp_v_w1, cmp_v_w2,
           w_up_m, w_up_n, merge_gate_b, w_out, norm2_w, ffn_w_up, ffn_conv_w, ffn_conv_b, ffn_w_down):
    batch, seq, _ = x.shape
    t = batch * seq
    x2 = x.reshape(t, D_MODEL)
    n1w2 = n1w.reshape(1, D_MODEL)
    tm = 256
    tm_proj = 512

    small_pad = jnp.zeros((D_MODEL, LANES - 2 * M_HEADS - 3 * N_HEADS), F32)
    w_m = jnp.concatenate([_cols(w_in, "mq", "mk", "mv", "mo", "mi", "mf", "ng"), small_pad], axis=1).astype(BF16)
    w_n = _cols(w_in, "nq", "kc", "vc", "ks", "vs", "kw", "vw").astype(BF16)
    w_g = _cols(w_in, "gm", "gn").astype(BF16)
    sbias = jnp.concatenate([m_igate_b, m_fgate_b, jnp.zeros((LANES - 2 * M_HEADS,), F32)]).reshape(1, LANES)

    half = N_DH // 2
    pos = jnp.arange(seq, dtype=F32)
    inv = ROPE_THETA ** (-jnp.arange(0, N_DH, 2, dtype=F32) / N_DH)
    ang = pos[:, None] * inv[None, :]
    cos, sin = jnp.cos(ang), jnp.sin(ang)
    cosn = jnp.tile(cos, (1, N_KVW // half))
    sinn = jnp.tile(jnp.concatenate([-sin, sin], axis=1), (1, N_KV))
    cosT, sinT = cos.T, sin.T
    knw = jnp.zeros((8, N_KVW), F32).at[0:3].set(
        jnp.stack([jnp.tile(w, N_KV) for w in (kcmp_norm_w, kslc_norm_w, kwin_norm_w)]))

    q_m, kT_m, v_m, og, small, smallT = _proj_m(
        x2, n1w2, w_m, m_conv_w, m_conv_b.reshape(1, -1), sbias, seq, tm_proj)
    qT, kvc, ks, kw, vsT, vwT = _proj_n(
        x2, n1w2, w_n, q_norm_w.reshape(N_DH, 1), knw, cosn, sinn, cosT, sinT, seq, tm_proj)

    nb = seq // CMP_STRIDE
    xkv = kvc.reshape(2 * N_KV, batch, nb, CMP_STRIDE * N_DH)
    pe = jnp.stack([cmp_k_pe.reshape(1, -1), cmp_v_pe.reshape(1, -1)])
    w1 = jnp.stack([cmp_k_w1, cmp_v_w1]).astype(BF16)
    w2 = jnp.stack([cmp_k_w2, cmp_v_w2]).astype(BF16)
    cmp = _compress(xkv, pe, w1, w2)
    kcmp = cmp[0:N_KV].transpose(1, 2, 0, 3).reshape(batch, nb, N_KVW).astype(BF16)
    vcmpT = cmp[N_KV:].transpose(1, 0, 3, 2).reshape(batch, N_KVW, nb).astype(BF16)

    onw = jnp.zeros((8, M_DV), F32).at[0:M_HEADS].set(m_out_norm_w)
    hm = _mlstm(q_m, kT_m, v_m, small, smallT, og, onw, batch, seq, 4 * MLSTM_CHUNK)
    nselp = -(-(seq // SEL_BLOCK) // LANES) * LANES
    emap = (np.arange(seq)[:, None] // SEL_BLOCK == np.arange(nselp)[None, :]).astype(np.float32)
    on = _nsa(qT, kcmp, vcmpT, ks, jnp.asarray(emap, dtype=BF16), vsT, kw, vwT, smallT, batch, seq)

    x1 = _merge(x2, n1w2, w_g, merge_gate_b, hm, on, w_up_m.astype(BF16), w_up_n.astype(BF16),
                w_out.astype(BF16), tm_proj)
    out = _ffn(x1, norm2_w.reshape(1, D_MODEL), ffn_w_up.astype(BF16), ffn_conv_w, ffn_conv_b.reshape(1, -1),
               ffn_w_down.astype(BF16), seq, tm)
    return out.reshape(batch, seq, D_MODEL)


def kernel(x, norm1_w, w_in, m_conv_w, m_conv_b, m_igate_b, m_fgate_b, m_out_norm_w, q_norm_w, kcmp_norm_w,
           kslc_norm_w, kwin_norm_w, cmp_k_pe, cmp_k_w1, cmp_k_w2, cmp_v_pe, cmp_v_w1, cmp_v_w2, w_up_m, w_up_n,
           merge_gate_b, w_out, norm2_w, ffn_w_up, ffn_conv_w, ffn_conv_b, ffn_w_down):
    params = (norm1_w, w_in, m_conv_w, m_conv_b, m_igate_b, m_fgate_b, m_out_norm_w, q_norm_w, kcmp_norm_w,
              kslc_norm_w, kwin_norm_w, cmp_k_pe, cmp_k_w1, cmp_k_w2, cmp_v_pe, cmp_v_w1, cmp_v_w2, w_up_m, w_up_n,
              merge_gate_b, w_out, norm2_w, ffn_w_up, ffn_conv_w, ffn_conv_b, ffn_w_down)
    for layer in range(norm1_w.shape[0]):
        x = _layer(x, *[p[layer] for p in params])
    return x
```

```python
import functools
import math

import jax
import jax.numpy as jnp
import numpy as np
from jax import lax
from jax.experimental import pallas as pl
from jax.experimental.pallas import tpu as pltpu

D_MODEL = 1024
EPS = 1e-6
ROPE_THETA = 10000.0
NEG = -1e30
M_HEADS = 4
M_DQK = 64
M_DV = 128
M_CONV = 4
M_QK = M_HEADS * M_DQK
M_V = M_HEADS * M_DV
N_HEADS = 8
N_KV = 2
N_HG = N_HEADS // N_KV
N_DH = 64
N_Q = N_HEADS * N_DH
N_KVW = N_KV * N_DH
CMP_LEN = 32
CMP_STRIDE = 16
CMP_HIDDEN = 256
SEL_BLOCK = 64
SEL_TOPN = 16
WINDOW = 512
D_FF = 2816
FFN_CONV = 3

_OFF = {}
_o = 0
for _name, _size in (("mq", M_QK), ("mk", M_QK), ("mv", M_V), ("mo", M_V), ("mi", M_HEADS), ("mf", M_HEADS),
                     ("nq", N_Q), ("kc", N_KVW), ("vc", N_KVW), ("ks", N_KVW), ("vs", N_KVW), ("kw", N_KVW),
                     ("vw", N_KVW), ("ng", 3 * N_HEADS), ("gm", D_MODEL), ("gn", D_MODEL)):
    _OFF[_name] = (_o, _o + _size)
    _o += _size

LANES = 128
MLSTM_CHUNK = 128
SUB_ROWS = 256
Q_TILE = 128
KEY_TILE = 256
WIN_SPAN = WINDOW + Q_TILE
VMEM_LIMIT = 56 * 1024 * 1024

LOG2E = math.log2(math.e)

F32 = jnp.float32
BF16 = jnp.bfloat16
HIGHEST = lax.Precision.HIGHEST


def _dot(a, b):
    return jnp.dot(a, b, preferred_element_type=F32)


def _dot_nt(a, b):
    return lax.dot_general(a, b, (((1,), (1,)), ((), ())), preferred_element_type=F32)


def _rmsnorm_rows(x, w):
    return x * lax.rsqrt(jnp.mean(x * x, axis=-1, keepdims=True) + EPS) * w


def _sigmoid(x):
    return 1.0 / (1.0 + jnp.exp(-x))


def _gelu(x):
    return 0.5 * x * (1.0 + lax.erf(x * (1.0 / math.sqrt(2.0))))


def _params(sem):
    return pltpu.CompilerParams(dimension_semantics=sem, vmem_limit_bytes=VMEM_LIMIT)


def _proj_m_kernel(x_ref, n1w_ref, w_ref, wvT_ref, cw_ref, cb_ref, sb_ref,
                   qT_ref, k_ref, vT_ref, og_ref, small_ref, smallT_ref, buf_ref, *, tiles_per_seq):
    i = pl.program_id(0)
    tm = x_ref.shape[0]
    sub = SUB_ROWS

    @pl.when(i % tiles_per_seq == 0)
    def _():
        buf_ref[0:8, :] = jnp.zeros((8, 2 * M_QK), F32)

    hns = [_rmsnorm_rows(x_ref[r * sub:(r + 1) * sub, :], n1w_ref[...]).astype(BF16) for r in range(tm // sub)]
    for r in range(tm // sub):
        rows = slice(r * sub, (r + 1) * sub)
        hn = hns[r]
        qk = _dot(hn, w_ref[:, 0:2 * M_QK])
        buf_ref[8:8 + sub, :] = qk
        acc = cb_ref[...] + cw_ref[M_CONV - 1:M_CONV, :] * qk
        for k in range(M_CONV - 1):
            acc = acc + cw_ref[k:k + 1, :] * buf_ref[8 - (M_CONV - 1) + k:8 - (M_CONV - 1) + k + sub, :]
        buf_ref[0:8, :] = buf_ref[sub:sub + 8, :]
        act = acc * _sigmoid(acc)
        qT_ref[:, rows] = (act[:, 0:M_QK] * (M_DQK ** -0.5)).T.astype(BF16)
        k_ref[rows, :] = act[:, M_QK:2 * M_QK].astype(BF16)
        vT_ref[:, rows] = _dot_nt(wvT_ref[...], hn).astype(BF16)
        og_ref[rows, :] = _sigmoid(_dot(hn, w_ref[:, 2 * M_QK:2 * M_QK + M_V]))
        sm = _dot(hn, w_ref[:, 2 * M_QK + M_V:]) + sb_ref[...]
        lane = lax.broadcasted_iota(jnp.int32, sm.shape, 1)
        logsig = jnp.minimum(sm, 0.0) - jnp.log1p(jnp.exp(-jnp.abs(sm)))
        sm = jnp.where(lane < M_HEADS, sm, jnp.where(lane < 2 * M_HEADS, logsig, _sigmoid(sm)))
        small_ref[rows, :] = sm
        smallT_ref[:, rows] = sm.T[0:32, :]


def _proj_m(x2, n1w, w, wvT, cw, cb, sb, seq, tm):
    t = x2.shape[0]
    ncol = w.shape[1]
    kern = functools.partial(_proj_m_kernel, tiles_per_seq=seq // tm)
    return pl.pallas_call(
        kern,
        grid=(t // tm,),
        in_specs=[
            pl.BlockSpec((tm, D_MODEL), lambda i: (i, 0)),
            pl.BlockSpec((1, D_MODEL), lambda i: (0, 0)),
            pl.BlockSpec((D_MODEL, ncol), lambda i: (0, 0)),
            pl.BlockSpec((M_V, D_MODEL), lambda i: (0, 0)),
            pl.BlockSpec((M_CONV, 2 * M_QK), lambda i: (0, 0)),
            pl.BlockSpec((1, 2 * M_QK), lambda i: (0, 0)),
            pl.BlockSpec((1, LANES), lambda i: (0, 0)),
        ],
        out_specs=[
            pl.BlockSpec((M_QK, tm), lambda i: (0, i)),
            pl.BlockSpec((tm, M_QK), lambda i: (i, 0)),
            pl.BlockSpec((M_V, tm), lambda i: (0, i)),
            pl.BlockSpec((tm, M_V), lambda i: (i, 0)),
            pl.BlockSpec((tm, LANES), lambda i: (i, 0)),
            pl.BlockSpec((32, tm), lambda i: (0, i)),
        ],
        out_shape=[
            jax.ShapeDtypeStruct((M_QK, t), BF16),
            jax.ShapeDtypeStruct((t, M_QK), BF16),
            jax.ShapeDtypeStruct((M_V, t), BF16),
            jax.ShapeDtypeStruct((t, M_V), F32),
            jax.ShapeDtypeStruct((t, LANES), F32),
            jax.ShapeDtypeStruct((32, t), F32),
        ],
        scratch_shapes=[pltpu.VMEM((SUB_ROWS + 8, 2 * M_QK), F32)],
        compiler_params=_params(("arbitrary",)),
        name="proj_m",
    )(x2, n1w, w, wvT, cw, cb, sb)


def _proj_n_kernel(x_ref, n1w_ref, w_ref, qnw_ref, knw_ref, cosn_ref, sinn_ref, cosT_ref, sinT_ref,
                   qT_ref, kvc_ref, ks_ref, kw_ref, vsT_ref, vwT_ref):
    tm = x_ref.shape[0]
    hn = _rmsnorm_rows(x_ref[...], n1w_ref[...]).astype(BF16)
    qT = _dot(hn, w_ref[:, 0:N_Q]).T
    cosT = cosT_ref[...]
    sinT = sinT_ref[...]
    qnw = qnw_ref[...]
    half = N_DH // 2
    for h in range(N_HEADS):
        xh = qT[h * N_DH:(h + 1) * N_DH, :]
        xn = xh * lax.rsqrt(jnp.mean(xh * xh, axis=0, keepdims=True) + EPS) * qnw
        x1 = xn[0:half, :]
        x2 = xn[half:, :]
        o = jnp.concatenate([x1 * cosT - x2 * sinT, x2 * cosT + x1 * sinT], axis=0) * (LOG2E * N_DH ** -0.5)
        qT_ref[h * N_DH:(h + 1) * N_DH, :] = o.astype(BF16)

    cosn = cosn_ref[...]
    sinn = sinn_ref[...]
    li = lax.broadcasted_iota(jnp.int32, (N_KVW, N_KVW), 0)
    lj = lax.broadcasted_iota(jnp.int32, (N_KVW, N_KVW), 1)
    head_sum = (li // N_DH == lj // N_DH).astype(BF16)
    swap_half = (lj == li + jnp.where(li % N_DH < half, half, -half)).astype(BF16)

    def lane_map(v, m01):
        hi = v.astype(BF16)
        lo = (v - hi.astype(F32)).astype(BF16)
        return _dot(hi, m01) + _dot(lo, m01)

    c0 = N_Q
    k_raw = [_dot(hn, w_ref[:, c0 + 2 * i * N_KVW:c0 + (2 * i + 1) * N_KVW]) for i in range(3)]
    vc, vs, vw = [_dot(hn, w_ref[:, c0 + (2 * i + 1) * N_KVW:c0 + (2 * i + 2) * N_KVW]) for i in range(3)]
    ms = [lane_map(k * k, head_sum) * (1.0 / N_DH) for k in k_raw]
    kn = [k_raw[i] * lax.rsqrt(ms[i] + EPS) * knw_ref[i:i + 1, :] for i in range(3)]
    kc, ks, kw = [kn[i] * cosn + lane_map(kn[i], swap_half) * sinn for i in range(3)]
    for g in range(N_KV):
        kvc_ref[g] = kc[:, g * N_DH:(g + 1) * N_DH]
        kvc_ref[N_KV + g] = vc[:, g * N_DH:(g + 1) * N_DH]
    ks_ref[...] = ks.astype(BF16)
    kw_ref[...] = kw.astype(BF16)
    vsT = vs.T.astype(BF16)
    vwT = vw.T.astype(BF16)
    for j in range(tm // LANES):
        vsT_ref[j] = vsT[:, j * LANES:(j + 1) * LANES]
        vwT_ref[j] = vwT[:, j * LANES:(j + 1) * LANES]


def _proj_n(x2, n1w, w, qnw, knw, cosn, sinn, cosT, sinT, seq, tm):
    t = x2.shape[0]
    ncol = w.shape[1]
    tps = seq // tm
    half = N_DH // 2
    return pl.pallas_call(
        _proj_n_kernel,
        grid=(t // tm,),
        in_specs=[
            pl.BlockSpec((tm, D_MODEL), lambda i: (i, 0)),
            pl.BlockSpec((1, D_MODEL), lambda i: (0, 0)),
            pl.BlockSpec((D_MODEL, ncol), lambda i: (0, 0)),
            pl.BlockSpec((N_DH, 1), lambda i: (0, 0)),
            pl.BlockSpec((8, N_KVW), lambda i: (0, 0)),
            pl.BlockSpec((tm, N_KVW), lambda i: (i % tps, 0)),
            pl.BlockSpec((tm, N_KVW), lambda i: (i % tps, 0)),
            pl.BlockSpec((half, tm), lambda i: (0, i % tps)),
            pl.BlockSpec((half, tm), lambda i: (0, i % tps)),
        ],
        out_specs=[
            pl.BlockSpec((N_Q, tm), lambda i: (0, i)),
            pl.BlockSpec((2 * N_KV, tm, N_DH), lambda i: (0, i, 0)),
            pl.BlockSpec((tm, N_KVW), lambda i: (i, 0)),
            pl.BlockSpec((tm, N_KVW), lambda i: (i, 0)),
            pl.BlockSpec((tm // LANES, N_KVW, LANES), lambda i: (i, 0, 0)),
            pl.BlockSpec((tm // LANES, N_KVW, LANES), lambda i: (i, 0, 0)),
        ],
        out_shape=[
            jax.ShapeDtypeStruct((N_Q, t), BF16),
            jax.ShapeDtypeStruct((2 * N_KV, t, N_DH), F32),
            jax.ShapeDtypeStruct((t, N_KVW), BF16),
            jax.ShapeDtypeStruct((t, N_KVW), BF16),
            jax.ShapeDtypeStruct((t // LANES, N_KVW, LANES), BF16),
            jax.ShapeDtypeStruct((t // LANES, N_KVW, LANES), BF16),
        ],
        compiler_params=_params(("arbitrary",)),
        name="proj_n",
    )(x2, n1w, w, qnw, knw, cosn, sinn, cosT, sinT)


def _compress_kernel(x_ref, pe_ref, w1_ref, w2_ref, o_ref):
    nb = o_ref.shape[2]
    half = (CMP_LEN // 2) * N_DH
    pe = pe_ref[0]
    x = x_ref[0, 0]
    first = _dot((x + pe[:, 0:half]).astype(BF16), w1_ref[0, 0:half, :])
    second = _dot((x + pe[:, half:]).astype(BF16), w1_ref[0, half:, :])
    hid = first + pltpu.roll(second, nb - 1, 0)
    o_ref[0, 0] = _dot(_gelu(hid).astype(BF16), w2_ref[0])


def _compress(xkv, pe, w1, w2):
    na, batch, nb, width = xkv.shape
    return pl.pallas_call(
        _compress_kernel,
        grid=(na, batch),
        in_specs=[
            pl.BlockSpec((1, 1, nb, width), lambda a, b: (a, b, 0, 0)),
            pl.BlockSpec((1, 1, CMP_LEN * N_DH), lambda a, b: (a // N_KV, 0, 0)),
            pl.BlockSpec((1, CMP_LEN * N_DH, CMP_HIDDEN), lambda a, b: (a // N_KV, 0, 0)),
            pl.BlockSpec((1, CMP_HIDDEN, N_DH), lambda a, b: (a // N_KV, 0, 0)),
        ],
        out_specs=pl.BlockSpec((1, 1, nb, N_DH), lambda a, b: (a, b, 0, 0)),
        out_shape=jax.ShapeDtypeStruct((na, batch, nb, N_DH), F32),
        compiler_params=_params(("arbitrary", "arbitrary")),
        name="compress",
    )(xkv, pe, w1, w2)


def _mlstm_kernel(qT_ref, k_ref, vT_ref, small_ref, smallT_ref, og_ref, onw_ref, o_ref, c_ref, m_ref):
    L = MLSTM_CHUNK
    tb = k_ref.shape[0]

    @pl.when(pl.program_id(1) == 0)
    def _():
        c_ref[...] = jnp.zeros(c_ref.shape, F32)
        m_ref[...] = jnp.zeros(m_ref.shape, F32)

    row = lax.broadcasted_iota(jnp.int32, (L, L), 0)
    col = lax.broadcasted_iota(jnp.int32, (L, L), 1)
    causal = row <= col
    tril = (col <= row).astype(F32)
    triu = causal.astype(F32)
    ones_rows = jnp.ones((16, L), BF16)
    zeros_q = jnp.zeros((M_DQK, L), BF16)

    heads = range(M_HEADS)
    chunks = range(tb // L)
    sls = [slice(c * L, (c + 1) * L) for c in chunks]
    sms = [small_ref[sl, :] for sl in sls]
    bcols = [jnp.dot(tril, sm, preferred_element_type=F32, precision=HIGHEST) for sm in sms]
    brows = [jnp.dot(smallT_ref[0:8, sl], triu, preferred_element_type=F32, precision=HIGHEST) for sl in sls]
    c_state = [c_ref[h] for h in heads]
    m_state = [m_ref[h:h + 1, 0:1] for h in heads]
    for c in chunks:
        sl = sls[c]
        k_pairs = [k_ref[sl, p * LANES:(p + 1) * LANES] for p in range(M_HEADS // 2)]
        qT_pad, s, qc, vT_aug = [], [], [], []
        for h in heads:
            qT_h = qT_ref[h * M_DQK:(h + 1) * M_DQK, sl]
            qT_pad.append(jnp.concatenate([qT_h, zeros_q] if h % 2 == 0 else [zeros_q, qT_h], axis=0))
            s.append(_dot(k_pairs[h // 2], qT_pad[h]))
            qc.append(_dot(c_state[h].astype(BF16), qT_pad[h]))
            vT_aug.append(jnp.concatenate([vT_ref[h * M_DV:(h + 1) * M_DV, sl], ones_rows], axis=0))
        m_t, isc, sw, kw, m_new, decay, scale = [], [], [], [], [], [], []
        for h in heads:
            in_head = (col // M_DQK) == (h % 2)
            b_col = bcols[c][:, M_HEADS + h:M_HEADS + h + 1]
            i_col = sms[c][:, h:h + 1]
            b_row = brows[c][M_HEADS + h:M_HEADS + h + 1, :]
            g = b_row[:, L - 1:L]
            d = jnp.where(causal, b_row + (i_col - b_col), NEG)
            inter = b_row + m_state[h]
            m_t.append(jnp.maximum(inter, jnp.max(d, axis=0, keepdims=True)))
            isc.append(jnp.exp(inter - m_t[h]))
            sw.append((s[h] * jnp.exp(d - m_t[h])).astype(BF16))
            a_col = g - b_col + i_col
            a_max = jnp.max(a_col, axis=0, keepdims=True)
            kw.append(jnp.where(in_head, k_pairs[h // 2].astype(F32) * jnp.exp(a_col - a_max), 0.0).astype(BF16))
            m_new.append(jnp.maximum(g + m_state[h], a_max))
            decay.append(jnp.exp(g + m_state[h] - m_new[h]))
            scale.append(jnp.exp(a_max - m_new[h]))
        sv = [_dot(vT_aug[h], sw[h]) for h in heads]
        dc = [_dot(vT_aug[h], kw[h]) for h in heads]
        for h in heads:
            nd = isc[h] * qc[h] + sv[h]
            den = nd[M_DV:M_DV + 1, :]
            hh = nd[0:M_DV, :] / jnp.maximum(jnp.abs(den), jnp.exp(-m_t[h]))
            hn = hh * lax.rsqrt(jnp.mean(hh * hh, axis=0, keepdims=True) + EPS)
            o_ref[sl, h * M_DV:(h + 1) * M_DV] = hn.T * onw_ref[h:h + 1, :] * og_ref[sl, h * M_DV:(h + 1) * M_DV]
            c_state[h] = decay[h] * c_state[h] + scale[h] * dc[h]
            m_state[h] = m_new[h]
    for h in heads:
        c_ref[h] = c_state[h]
        m_ref[h:h + 1, :] = jnp.broadcast_to(m_state[h], (1, LANES))


def _mlstm(qT, k, vT, small, smallT, og, onw, batch, seq, tb):
    t = k.shape[0]
    nblk = seq // tb
    return pl.pallas_call(
        _mlstm_kernel,
        grid=(batch, nblk),
        in_specs=[
            pl.BlockSpec((M_QK, tb), lambda b, j: (0, b * nblk + j)),
            pl.BlockSpec((tb, M_QK), lambda b, j: (b * nblk + j, 0)),
            pl.BlockSpec((M_V, tb), lambda b, j: (0, b * nblk + j)),
            pl.BlockSpec((tb, LANES), lambda b, j: (b * nblk + j, 0)),
            pl.BlockSpec((32, tb), lambda b, j: (0, b * nblk + j)),
            pl.BlockSpec((tb, M_V), lambda b, j: (b * nblk + j, 0)),
            pl.BlockSpec((8, M_DV), lambda b, j: (0, 0)),
        ],
        out_specs=pl.BlockSpec((tb, M_V), lambda b, j: (b * nblk + j, 0)),
        out_shape=jax.ShapeDtypeStruct((t, M_V), F32),
        scratch_shapes=[pltpu.VMEM((M_HEADS, M_DV + 16, LANES), F32), pltpu.VMEM((8, LANES), F32)],
        compiler_params=_params(("arbitrary", "arbitrary")),
        name="mlstm",
    )(qT, k, vT, small, smallT, og, onw)


def _nsa_kernel(qT_ref, kcmp_ref, vcmpT_ref, ks_ref, e_ref, vsT_ref, kw_ref, vwT_ref, gT_ref,
                o_ref, rhs_ref, ps_ref, s0_ref, s1_ref, c0_ref, c1_ref, m_ref, acc_ref, *, nsel, n_rounds):
    qi = pl.program_id(1)
    tq = Q_TILE
    nb = kcmp_ref.shape[0]
    nselp = e_ref.shape[1]
    q0 = qi * tq
    wide = N_HG * tq
    lane_w = lax.broadcasted_iota(jnp.int32, (1, wide), 1)
    tpos_w = q0 + (lane_w % tq)
    tpos = q0 + lax.broadcasted_iota(jnp.int32, (1, tq), 1)
    zeros_q = jnp.zeros((N_DH, wide), BF16)
    ones_v = jnp.ones((16, LANES), BF16)
    grows = [slice(g * N_DH, (g + 1) * N_DH) for g in range(N_KV)]
    gcols = [slice(g * wide, (g + 1) * wide) for g in range(N_KV)]

    def values(v_ref, first, count, rows):
        return jnp.concatenate(
            [jnp.concatenate([v_ref[first + j, rows, :], ones_v], axis=0) for j in range(count)], axis=1)

    start_w = pl.multiple_of(jnp.maximum(q0 - WINDOW, 0), LANES)
    sc, sw = [], []
    for g in range(N_KV):
        q4 = jnp.concatenate(
            [qT_ref[(g * N_HG + h) * N_DH:(g * N_HG + h + 1) * N_DH, :] for h in range(N_HG)], axis=1)
        qpad = jnp.concatenate([q4, zeros_q] if g == 0 else [zeros_q, q4], axis=0)
        rhs_ref[0:2 * N_DH, gcols[g]] = qpad
        sc.append(_dot(kcmp_ref[...], qpad))
        sw.append(_dot(kw_ref[pl.ds(start_w, WIN_SPAN), :], qpad))

    cend = lax.broadcasted_iota(jnp.int32, (nb, 1), 0) * CMP_STRIDE + (CMP_LEN - 1)
    cmask = cend <= tpos_w
    any_visible = tpos_w >= CMP_LEN - 1
    ratio = SEL_BLOCK // CMP_STRIDE
    o_cmp, imp = [], []
    for g in range(N_KV):
        s = jnp.where(cmask, sc[g], NEG)
        pc = jnp.exp2(s - jnp.max(s, axis=0, keepdims=True))
        lc = jnp.sum(pc, axis=0, keepdims=True)
        pc = pc * jnp.where(any_visible, 1.0 / lc, 0.0)
        o_cmp.append(_dot(vcmpT_ref[grows[g], :], pc.astype(BF16)))
        psum = pc[:, 0:tq]
        for h in range(1, N_HG):
            psum = psum + pc[:, h * tq:(h + 1) * tq]
        parts = []
        for cchunk in range(tq // LANES):
            ps_ref[g, cchunk, 0:8, :] = jnp.zeros((8, LANES), F32)
            ps_ref[g, cchunk, 8:8 + nb, :] = psum[:, cchunk * LANES:(cchunk + 1) * LANES]
            acc = None
            for k in range(-((CMP_LEN - 1) // CMP_STRIDE), ratio):
                part = ps_ref[g, cchunk, pl.ds(8 + k, nsel, stride=ratio), :]
                acc = part if acc is None else acc + part
            parts.append(acc)
        imp.append(jnp.concatenate(parts, axis=1))

    def select_blocks(nrows):
        jblk = lax.broadcasted_iota(jnp.int32, (nrows, tq), 0)
        cur = tpos // SEL_BLOCK
        forced = (jblk == 0) | (jblk == cur) | (jblk == cur - 1)
        cand = (jblk >= 1) & (jblk <= cur - 2)
        jblk_f = jblk.astype(F32)
        val = [jnp.where(cand, imp[g][0:nrows, :], -jnp.inf) for g in range(N_KV)]
        for _ in range(n_rounds):
            for g in range(N_KV):
                mx = jnp.max(val[g], axis=0, keepdims=True)
                first = jnp.min(jnp.where(val[g] == mx, jblk_f, float(nrows)), axis=0, keepdims=True)
                val[g] = jnp.where(jblk_f == first, -jnp.inf, val[g])
        for g in range(N_KV):
            picked = cand & (val[g] == -jnp.inf)
            bias = jnp.where(forced | picked, 0.0, NEG).astype(BF16)
            if nselp > nrows:
                bias = jnp.concatenate([bias, jnp.zeros((nselp - nrows, tq), BF16)], axis=0)
            rhs_ref[2 * N_DH:, gcols[g]] = jnp.concatenate([bias] * N_HG, axis=1)

    n_var = 4 if nsel % 32 == 0 else 1
    if n_var == 1:
        select_blocks(nsel)
    else:
        tiles_per_var = (nsel * SEL_BLOCK // tq) // n_var
        for v in range(n_var):
            pl.when(qi // tiles_per_var == v)(functools.partial(select_blocks, (v + 1) * nsel // n_var))

    dist = (tpos_w - start_w) - lax.broadcasted_iota(jnp.int32, (WIN_SPAN, 1), 0)
    wmask = lax.bitcast_convert_type(dist, jnp.uint32) < WINDOW

    def scores(kt, g):
        start = pl.multiple_of(kt * KEY_TILE, KEY_TILE)
        lhs = jnp.concatenate([ks_ref[pl.ds(start, KEY_TILE), :], e_ref[pl.ds(start, KEY_TILE), :]], axis=1)
        return _dot(lhs, rhs_ref[:, g * wide:(g + 1) * wide])

    def produce(kt, s_ref, c_ref):
        for g in range(N_KV):
            s = scores(kt, g)
            s_ref[g] = s
            c_ref[g] = jnp.max(s, axis=0, keepdims=True)

    def consume(kt, s_ref, c_ref, causal_tile=False):
        for g in range(N_KV):
            s = s_ref[g]
            if causal_tile:
                kpos = kt * KEY_TILE + lax.broadcasted_iota(jnp.int32, (KEY_TILE, 1), 0)
                s = jnp.where(kpos <= tpos_w, s, NEG)
                smax = jnp.max(s, axis=0, keepdims=True)
            else:
                smax = c_ref[g]
            m = m_ref[g]
            m_new = jnp.maximum(m, smax)
            p = jnp.exp2(s - m_new).astype(BF16)
            vt = values(vsT_ref, kt * (KEY_TILE // LANES), KEY_TILE // LANES, slice(g * N_DH, (g + 1) * N_DH))
            acc_ref[g] = jnp.exp2(m - m_new) * acc_ref[g] + _dot(vt, p)
            m_ref[g] = m_new

    n_full = qi // (KEY_TILE // tq)
    odd = n_full % 2
    m_ref[...] = jnp.full(m_ref.shape, NEG, F32)
    acc_ref[...] = jnp.zeros(acc_ref.shape, F32)
    produce(0, s0_ref, c0_ref)

    o_win = []
    for g in range(N_KV):
        s = jnp.where(wmask, sw[g], NEG)
        pw = jnp.exp2(s - jnp.max(s, axis=0, keepdims=True))
        ow = _dot(values(vwT_ref, start_w // LANES, WIN_SPAN // LANES, grows[g]), pw.astype(BF16))
        o_win.append(ow[0:N_DH, :] * (1.0 / ow[N_DH:N_DH + 1, :]))

    def pair(kt):
        produce(kt + 1, s1_ref, c1_ref)
        consume(kt, s0_ref, c0_ref)
        produce(kt + 2, s0_ref, c0_ref)
        consume(kt + 1, s1_ref, c1_ref)

    def quad(j, _):
        pair(4 * j)
        pair(4 * j + 2)
        return 0

    n_quad = n_full // 4
    lax.fori_loop(0, n_quad, quad, 0)

    @pl.when(n_full - 4 * n_quad >= 2)
    def _():
        pair(4 * n_quad)

    @pl.when(odd == 0)
    def _():
        consume(n_full, s0_ref, c0_ref, causal_tile=True)

    @pl.when(odd == 1)
    def _():
        produce(n_full, s1_ref, c1_ref)
        consume(n_full - 1, s0_ref, c0_ref)
        consume(n_full, s1_ref, c1_ref, causal_tile=True)

    outs = []
    for g in range(N_KV):
        o_slc = acc_ref[g, 0:N_DH, :] * (1.0 / acc_ref[g, N_DH:N_DH + 1, :])
        for h in range(N_HG):
            cs = slice(h * tq, (h + 1) * tq)
            r = 2 * M_HEADS + (g * N_HG + h) * 3
            outs.append(gT_ref[r:r + 1, :] * o_cmp[g][:, cs] + gT_ref[r + 1:r + 2, :] * o_slc[:, cs]
                        + gT_ref[r + 2:r + 3, :] * o_win[g][:, cs])

    o_ref[...] = jnp.concatenate(outs, axis=0).T


def _nsa(qT, kcmp, vcmpT, ks, emap, vsT, kw, vwT, smallT, batch, seq):
    t = qT.shape[1]
    nq = seq // Q_TILE
    nb = kcmp.shape[1]
    nsel = seq // SEL_BLOCK
    nselp = emap.shape[1]
    wide = N_HG * Q_TILE
    n_rounds = max(min(SEL_TOPN, nsel) - 3, 0)
    kern = functools.partial(_nsa_kernel, nsel=nsel, n_rounds=n_rounds)
    return pl.pallas_call(
        kern,
        grid=(batch, nq),
        in_specs=[
            pl.BlockSpec((N_Q, Q_TILE), lambda b, i: (0, b * nq + i)),
            pl.BlockSpec((None, nb, N_KVW), lambda b, i: (b, 0, 0)),
            pl.BlockSpec((None, N_KVW, nb), lambda b, i: (b, 0, 0)),
            pl.BlockSpec((seq, N_KVW), lambda b, i: (b, 0)),
            pl.BlockSpec((seq, nselp), lambda b, i: (0, 0)),
            pl.BlockSpec((seq // LANES, N_KVW, LANES), lambda b, i: (b, 0, 0)),
            pl.BlockSpec((seq, N_KVW), lambda b, i: (b, 0)),
            pl.BlockSpec((seq // LANES, N_KVW, LANES), lambda b, i: (b, 0, 0)),
            pl.BlockSpec((32, Q_TILE), lambda b, i: (0, b * nq + i)),
        ],
        out_specs=pl.BlockSpec((Q_TILE, N_Q), lambda b, i: (b * nq + i, 0)),
        out_shape=jax.ShapeDtypeStruct((t, N_Q), F32),
        scratch_shapes=[
            pltpu.VMEM((2 * N_DH + nselp, N_KV * wide), BF16),
            pltpu.VMEM((N_KV, Q_TILE // LANES, nb + 8, LANES), F32),
            pltpu.VMEM((N_KV, KEY_TILE, wide), F32),
            pltpu.VMEM((N_KV, KEY_TILE, wide), F32),
            pltpu.VMEM((N_KV, 1, wide), F32),
            pltpu.VMEM((N_KV, 1, wide), F32),
            pltpu.VMEM((N_KV, 1, wide), F32),
            pltpu.VMEM((N_KV, N_DH + 16, wide), F32),
        ],
        compiler_params=_params(("arbitrary", "arbitrary")),
        name="nsa",
    )(qT, kcmp, vcmpT, ks, emap, vsT, kw, vwT, smallT)


def _merge_kernel(x_ref, n1w_ref, wg_ref, gb_ref, hm_ref, on_ref, wm_ref, wn_ref, wo_ref, o_ref):
    x = x_ref[...]
    hn = _rmsnorm_rows(x, n1w_ref[...]).astype(BF16)
    gm = _sigmoid(_dot(hn, wg_ref[:, 0:D_MODEL]) + gb_ref[0:1, :])
    gn = _sigmoid(_dot(hn, wg_ref[:, D_MODEL:]) + gb_ref[1:2, :])
    y = gm * _dot(hm_ref[...].astype(BF16), wm_ref[...]) + gn * _dot(on_ref[...].astype(BF16), wn_ref[...])
    o_ref[...] = x + _dot(y.astype(BF16), wo_ref[...])


def _merge(x2, n1w, wg, gb, hm, on, wm, wn, wo, tm):
    t = x2.shape[0]
    const = lambda i: (0, 0)
    return pl.pallas_call(
        _merge_kernel,
        grid=(t // tm,),
        in_specs=[
            pl.BlockSpec((tm, D_MODEL), lambda i: (i, 0)),
            pl.BlockSpec((1, D_MODEL), const),
            pl.BlockSpec((D_MODEL, 2 * D_MODEL), const),
            pl.BlockSpec((2, D_MODEL), const),
            pl.BlockSpec((tm, M_V), lambda i: (i, 0)),
            pl.BlockSpec((tm, N_Q), lambda i: (i, 0)),
            pl.BlockSpec((M_V, D_MODEL), const),
            pl.BlockSpec((N_Q, D_MODEL), const),
            pl.BlockSpec((D_MODEL, D_MODEL), const),
        ],
        out_specs=pl.BlockSpec((tm, D_MODEL), lambda i: (i, 0)),
        out_shape=jax.ShapeDtypeStruct((t, D_MODEL), F32),
        compiler_params=_params(("arbitrary",)),
        name="merge",
    )(x2, n1w, wg, gb, hm, on, wm, wn, wo)


def _ffn_kernel(x_ref, n2w_ref, wup_ref, cw_ref, cb_ref, wdn_ref, o_ref, buf_ref, *, tiles_per_seq):
    i = pl.program_id(0)
    tm = x_ref.shape[0]
    x = x_ref[...]
    hn = _rmsnorm_rows(x, n2w_ref[...]).astype(BF16)

    @pl.when(i % tiles_per_seq == 0)
    def _():
        buf_ref[0:8, :] = jnp.zeros((8, D_FF), F32)

    a = _dot(hn, wup_ref[:, 0:D_FF])
    buf_ref[8:8 + tm, :] = a
    acc = cb_ref[...] + cw_ref[FFN_CONV - 1:FFN_CONV, :] * a
    for k in range(FFN_CONV - 1):
        acc = acc + cw_ref[k:k + 1, :] * buf_ref[8 - (FFN_CONV - 1) + k:8 - (FFN_CONV - 1) + k + tm, :]
    buf_ref[0:8, :] = buf_ref[tm:tm + 8, :]
    v = _dot(hn, wup_ref[:, D_FF:])
    o_ref[...] = x + _dot((_gelu(acc) * v).astype(BF16), wdn_ref[...])


def _ffn(x2, n2w, wup, cw, cb, wdn, seq, tm):
    t = x2.shape[0]
    const = lambda i: (0, 0)
    kern = functools.partial(_ffn_kernel, tiles_per_seq=seq // tm)
    return pl.pallas_call(
        kern,
        grid=(t // tm,),
        in_specs=[
            pl.BlockSpec((tm, D_MODEL), lambda i: (i, 0)),
            pl.BlockSpec((1, D_MODEL), const),
            pl.BlockSpec((D_MODEL, 2 * D_FF), const, pipeline_mode=pl.Buffered(1)),
            pl.BlockSpec((FFN_CONV, D_FF), const),
            pl.BlockSpec((1, D_FF), const),
            pl.BlockSpec((D_FF, D_MODEL), const, pipeline_mode=pl.Buffered(1)),
        ],
        out_specs=pl.BlockSpec((tm, D_MODEL), lambda i: (i, 0)),
        out_shape=jax.ShapeDtypeStruct((t, D_MODEL), F32),
        scratch_shapes=[pltpu.VMEM((tm + 8, D_FF), F32)],
        compiler_params=_params(("arbitrary",)),
        name="ffn",
    )(x2, n2w, wup, cw, cb, wdn)


def _cols(w, *names):
    return jnp.concatenate([w[:, _OFF[n][0]:_OFF[n][1]] for n in names], axis=1)


def _layer(x, n1w, w_in, m_conv_w, m_conv_b, m_igate_b, m_fgate_b, m_out_norm_w,
           q_norm_w, kcmp_norm_w, kslc_norm_w, kwin_norm_w,
           cmp_k_pe, cmp_k_w1, cmp_k_w2, cmp_v_pe, cmp_v_w1, cmp_v_w2,
           w_up_m, w_up_n, merge_gate_b, w_out, norm2_w, ffn_w_up, ffn_conv_w, ffn_conv_b, ffn_w_down):
    batch, seq, _ = x.shape
    t = batch * seq
    x2 = x.reshape(t, D_MODEL)
    n1w2 = n1w.reshape(1, D_MODEL)
    tm = 256
    tm_proj = 512

    small_pad = jnp.zeros((D_MODEL, LANES - 2 * M_HEADS - 3 * N_HEADS), F32)
    w_m = jnp.concatenate([_cols(w_in, "mq", "mk", "mo", "mi", "mf", "ng"), small_pad], axis=1).astype(BF16)
    w_mvT = _cols(w_in, "mv").T.astype(BF16)
    w_n = _cols(w_in, "nq", "kc", "vc", "ks", "vs", "kw", "vw").astype(BF16)
    w_g = _cols(w_in, "gm", "gn").astype(BF16)
    sbias = jnp.concatenate([m_igate_b, m_fgate_b, jnp.zeros((LANES - 2 * M_HEADS,), F32)]).reshape(1, LANES)

    half = N_DH // 2
    pos = jnp.arange(seq, dtype=F32)
    inv = ROPE_THETA ** (-jnp.arange(0, N_DH, 2, dtype=F32) / N_DH)
    ang = pos[:, None] * inv[None, :]
    cos, sin = jnp.cos(ang), jnp.sin(ang)
    cosn = jnp.tile(cos, (1, N_KVW // half))
    sinn = jnp.tile(jnp.concatenate([-sin, sin], axis=1), (1, N_KV))
    cosT, sinT = cos.T, sin.T
    knw = jnp.zeros((8, N_KVW), F32).at[0:3].set(
        jnp.stack([jnp.tile(w, N_KV) for w in (kcmp_norm_w, kslc_norm_w, kwin_norm_w)]))

    qT_m, k_m, vT_m, og, small, smallT = _proj_m(
        x2, n1w2, w_m, w_mvT, m_conv_w, m_conv_b.reshape(1, -1), sbias, seq, tm_proj)
    qT, kvc, ks, kw, vsT, vwT = _proj_n(
        x2, n1w2, w_n, q_norm_w.reshape(N_DH, 1), knw, cosn, sinn, cosT, sinT, seq, tm_proj)

    nb = seq // CMP_STRIDE
    xkv = kvc.reshape(2 * N_KV, batch, nb, CMP_STRIDE * N_DH)
    pe = jnp.stack([cmp_k_pe.reshape(1, -1), cmp_v_pe.reshape(1, -1)])
    w1 = jnp.stack([cmp_k_w1, cmp_v_w1]).astype(BF16)
    w2 = jnp.stack([cmp_k_w2, cmp_v_w2]).astype(BF16)
    cmp = _compress(xkv, pe, w1, w2)
    kcmp = cmp[0:N_KV].transpose(1, 2, 0, 3).reshape(batch, nb, N_KVW).astype(BF16)
    vcmpT = cmp[N_KV:].transpose(1, 0, 3, 2).reshape(batch, N_KVW, nb).astype(BF16)

    onw = jnp.zeros((8, M_DV), F32).at[0:M_HEADS].set(m_out_norm_w)
    hm = _mlstm(qT_m, k_m, vT_m, small, smallT, og, onw, batch, seq, 4 * MLSTM_CHUNK)
    nselp = -(-(seq // SEL_BLOCK) // LANES) * LANES
    emap = (np.arange(seq)[:, None] // SEL_BLOCK == np.arange(nselp)[None, :]).astype(np.float32)
    on = _nsa(qT, kcmp, vcmpT, ks, jnp.asarray(emap, dtype=BF16), vsT, kw, vwT, smallT, batch, seq)

    x1 = _merge(x2, n1w2, w_g, merge_gate_b, hm, on, w_up_m.astype(BF16), w_up_n.astype(BF16),
                w_out.astype(BF16), tm_proj)
    out = _ffn(x1, norm2_w.reshape(1, D_MODEL), ffn_w_up.astype(BF16), ffn_conv_w, ffn_conv_b.reshape(1, -1),
               ffn_w_down.astype(BF16), seq, tm)
    return out.reshape(batch, seq, D_MODEL)


def kernel(x, norm1_w, w_in, m_conv_w, m_conv_b, m_igate_b, m_fgate_b, m_out_norm_w, q_norm_w, kcmp_norm_w,
           kslc_norm_w, kwin_norm_w, cmp_k_pe, cmp_k_w1, cmp_k_w2, cmp_v_pe, cmp_v_w1, cmp_v_w2, w_up_m, w_up_n,
           merge_gate_b, w_out, norm2_w, ffn_w_up, ffn_conv_w, ffn_conv_b, ffn_w_down):
    params = (norm1_w, w_in, m_conv_w, m_conv_b, m_igate_b, m_fgate_b, m_out_norm_w, q_norm_w, kcmp_norm_w,
              kslc_norm_w, kwin_norm_w, cmp_k_pe, cmp_k_w1, cmp_k_w2, cmp_v_pe, cmp_v_w1, cmp_v_w2, w_up_m, w_up_n,
              merge_gate_b, w_out, norm2_w, ffn_w_up, ffn_conv_w, ffn_conv_b, ffn_w_down)
    for layer in range(norm1_w.shape[0]):
        x = _layer(x, *[p[layer] for p in params])
    return x
```

```python
import functools
import math

import jax
import jax.numpy as jnp
import numpy as np
from jax import lax
from jax.experimental import pallas as pl
from jax.experimental.pallas import tpu as pltpu

D_MODEL = 1024
EPS = 1e-6
ROPE_THETA = 10000.0
NEG = -1e30
M_HEADS = 4
M_DQK = 64
M_DV = 128
M_CONV = 4
M_QK = M_HEADS * M_DQK
M_V = M_HEADS * M_DV
N_HEADS = 8
N_KV = 2
N_HG = N_HEADS // N_KV
N_DH = 64
N_Q = N_HEADS * N_DH
N_KVW = N_KV * N_DH
CMP_LEN = 32
CMP_STRIDE = 16
CMP_HIDDEN = 256
SEL_BLOCK = 64
SEL_TOPN = 16
WINDOW = 512
D_FF = 2816
FFN_CONV = 3

_OFF = {}
_o = 0
for _name, _size in (("mq", M_QK), ("mk", M_QK), ("mv", M_V), ("mo", M_V), ("mi", M_HEADS), ("mf", M_HEADS),
                     ("nq", N_Q), ("kc", N_KVW), ("vc", N_KVW), ("ks", N_KVW), ("vs", N_KVW), ("kw", N_KVW),
                     ("vw", N_KVW), ("ng", 3 * N_HEADS), ("gm", D_MODEL), ("gn", D_MODEL)):
    _OFF[_name] = (_o, _o + _size)
    _o += _size

LANES = 128
MLSTM_CHUNK = 128
SUB_ROWS = 256
Q_TILE = 128
KEY_TILE = 256
WIN_SPAN = WINDOW + Q_TILE
VMEM_LIMIT = 56 * 1024 * 1024

LOG2E = math.log2(math.e)

F32 = jnp.float32
BF16 = jnp.bfloat16
HIGHEST = lax.Precision.HIGHEST


def _dot(a, b):
    return jnp.dot(a, b, preferred_element_type=F32)


def _dot_nt(a, b):
    return lax.dot_general(a, b, (((1,), (1,)), ((), ())), preferred_element_type=F32)


def _rmsnorm_rows(x, w):
    return x * lax.rsqrt(jnp.mean(x * x, axis=-1, keepdims=True) + EPS) * w


def _sigmoid(x):
    return 1.0 / (1.0 + jnp.exp(-x))


def _gelu(x):
    return 0.5 * x * (1.0 + lax.erf(x * (1.0 / math.sqrt(2.0))))


def _params(sem):
    return pltpu.CompilerParams(dimension_semantics=sem, vmem_limit_bytes=VMEM_LIMIT)


def _proj_m_kernel(x_ref, n1w_ref, w_ref, wvT_ref, cw_ref, cb_ref, sb_ref,
                   qT_ref, k_ref, vT_ref, og_ref, small_ref, smallT_ref, buf_ref, *, tiles_per_seq):
    i = pl.program_id(0)
    tm = x_ref.shape[0]
    sub = SUB_ROWS

    @pl.when(i % tiles_per_seq == 0)
    def _():
        buf_ref[0:8, :] = jnp.zeros((8, 2 * M_QK), F32)

    hns = [_rmsnorm_rows(x_ref[r * sub:(r + 1) * sub, :], n1w_ref[...]).astype(BF16) for r in range(tm // sub)]
    for r in range(tm // sub):
        rows = slice(r * sub, (r + 1) * sub)
        hn = hns[r]
        qk = _dot(hn, w_ref[:, 0:2 * M_QK])
        buf_ref[8:8 + sub, :] = qk
        acc = cb_ref[...] + cw_ref[M_CONV - 1:M_CONV, :] * qk
        for k in range(M_CONV - 1):
            acc = acc + cw_ref[k:k + 1, :] * buf_ref[8 - (M_CONV - 1) + k:8 - (M_CONV - 1) + k + sub, :]
        buf_ref[0:8, :] = buf_ref[sub:sub + 8, :]
        act = acc * _sigmoid(acc)
        qT_ref[:, rows] = (act[:, 0:M_QK] * (M_DQK ** -0.5)).T.astype(BF16)
        k_ref[rows, :] = act[:, M_QK:2 * M_QK].astype(BF16)
        vs = _dot_nt(wvT_ref[...], hn)
        vT_ref[:, rows] = vs[0:M_V, :].astype(BF16)
        og_ref[rows, :] = _sigmoid(_dot(hn, w_ref[:, 2 * M_QK:2 * M_QK + M_V]))
        smT = vs[M_V:, :] + sb_ref[...]
        rowi = lax.broadcasted_iota(jnp.int32, smT.shape, 0)
        logsig = jnp.minimum(smT, 0.0) - jnp.log1p(jnp.exp(-jnp.abs(smT)))
        smT = jnp.where(rowi < M_HEADS, smT, jnp.where(rowi < 2 * M_HEADS, logsig, _sigmoid(smT)))
        smallT_ref[:, rows] = smT
        small_ref[rows, :] = jnp.concatenate([smT, jnp.zeros((LANES - 32, sub), F32)], axis=0).T


def _proj_m(x2, n1w, w, wvT, cw, cb, sb, seq, tm):
    t = x2.shape[0]
    ncol = w.shape[1]
    kern = functools.partial(_proj_m_kernel, tiles_per_seq=seq // tm)
    return pl.pallas_call(
        kern,
        grid=(t // tm,),
        in_specs=[
            pl.BlockSpec((tm, D_MODEL), lambda i: (i, 0)),
            pl.BlockSpec((1, D_MODEL), lambda i: (0, 0)),
            pl.BlockSpec((D_MODEL, ncol), lambda i: (0, 0)),
            pl.BlockSpec((M_V + 32, D_MODEL), lambda i: (0, 0)),
            pl.BlockSpec((M_CONV, 2 * M_QK), lambda i: (0, 0)),
            pl.BlockSpec((1, 2 * M_QK), lambda i: (0, 0)),
            pl.BlockSpec((32, 1), lambda i: (0, 0)),
        ],
        out_specs=[
            pl.BlockSpec((M_QK, tm), lambda i: (0, i)),
            pl.BlockSpec((tm, M_QK), lambda i: (i, 0)),
            pl.BlockSpec((M_V, tm), lambda i: (0, i)),
            pl.BlockSpec((tm, M_V), lambda i: (i, 0)),
            pl.BlockSpec((tm, LANES), lambda i: (i, 0)),
            pl.BlockSpec((32, tm), lambda i: (0, i)),
        ],
        out_shape=[
            jax.ShapeDtypeStruct((M_QK, t), BF16),
            jax.ShapeDtypeStruct((t, M_QK), BF16),
            jax.ShapeDtypeStruct((M_V, t), BF16),
            jax.ShapeDtypeStruct((t, M_V), F32),
            jax.ShapeDtypeStruct((t, LANES), F32),
            jax.ShapeDtypeStruct((32, t), F32),
        ],
        scratch_shapes=[pltpu.VMEM((SUB_ROWS + 8, 2 * M_QK), F32)],
        compiler_params=_params(("arbitrary",)),
        name="proj_m",
    )(x2, n1w, w, wvT, cw, cb, sb)


def _proj_n_kernel(x_ref, n1w_ref, w_ref, qnw_ref, knw_ref, cosn_ref, sinn_ref, cosT_ref, sinT_ref,
                   qT_ref, kvc_ref, ks_ref, kw_ref, vsT_ref, vwT_ref):
    tm = x_ref.shape[0]
    hn = _rmsnorm_rows(x_ref[...], n1w_ref[...]).astype(BF16)
    qT = _dot(hn, w_ref[:, 0:N_Q]).T
    cosT = cosT_ref[...]
    sinT = sinT_ref[...]
    qnw = qnw_ref[...]
    half = N_DH // 2
    for h in range(N_HEADS):
        xh = qT[h * N_DH:(h + 1) * N_DH, :]
        xn = xh * lax.rsqrt(jnp.mean(xh * xh, axis=0, keepdims=True) + EPS) * qnw
        x1 = xn[0:half, :]
        x2 = xn[half:, :]
        o = jnp.concatenate([x1 * cosT - x2 * sinT, x2 * cosT + x1 * sinT], axis=0) * (LOG2E * N_DH ** -0.5)
        qT_ref[h * N_DH:(h + 1) * N_DH, :] = o.astype(BF16)

    cosn = cosn_ref[...]
    sinn = sinn_ref[...]
    li = lax.broadcasted_iota(jnp.int32, (N_KVW, N_KVW), 0)
    lj = lax.broadcasted_iota(jnp.int32, (N_KVW, N_KVW), 1)
    head_sum = (li // N_DH == lj // N_DH).astype(BF16)
    swap_half = (lj == li + jnp.where(li % N_DH < half, half, -half)).astype(BF16)
    head_sum = jnp.concatenate([head_sum, head_sum], axis=0)
    swap_half = jnp.concatenate([swap_half, swap_half], axis=0)

    def lane_map(v, m01x2):
        hi = v.astype(BF16)
        lo = (v - hi.astype(F32)).astype(BF16)
        return _dot(jnp.concatenate([hi, lo], axis=1), m01x2)

    c0 = N_Q
    k_raw = [_dot(hn, w_ref[:, c0 + 2 * i * N_KVW:c0 + (2 * i + 1) * N_KVW]) for i in range(3)]
    vc, vs, vw = [_dot(hn, w_ref[:, c0 + (2 * i + 1) * N_KVW:c0 + (2 * i + 2) * N_KVW]) for i in range(3)]
    ms = [lane_map(k * k, head_sum) * (1.0 / N_DH) for k in k_raw]
    kn = [k_raw[i] * lax.rsqrt(ms[i] + EPS) * knw_ref[i:i + 1, :] for i in range(3)]
    kc, ks, kw = [kn[i] * cosn + lane_map(kn[i], swap_half) * sinn for i in range(3)]
    for g in range(N_KV):
        kvc_ref[g] = kc[:, g * N_DH:(g + 1) * N_DH]
        kvc_ref[N_KV + g] = vc[:, g * N_DH:(g + 1) * N_DH]
    ks_ref[...] = ks.astype(BF16)
    kw_ref[...] = kw.astype(BF16)
    vsT = vs.T.astype(BF16)
    vwT = vw.T.astype(BF16)
    for j in range(tm // LANES):
        vsT_ref[j] = vsT[:, j * LANES:(j + 1) * LANES]
        vwT_ref[j] = vwT[:, j * LANES:(j + 1) * LANES]


def _proj_n(x2, n1w, w, qnw, knw, cosn, sinn, cosT, sinT, seq, tm):
    t = x2.shape[0]
    ncol = w.shape[1]
    tps = seq // tm
    half = N_DH // 2
    return pl.pallas_call(
        _proj_n_kernel,
        grid=(t // tm,),
        in_specs=[
            pl.BlockSpec((tm, D_MODEL), lambda i: (i, 0)),
            pl.BlockSpec((1, D_MODEL), lambda i: (0, 0)),
            pl.BlockSpec((D_MODEL, ncol), lambda i: (0, 0)),
            pl.BlockSpec((N_DH, 1), lambda i: (0, 0)),
            pl.BlockSpec((8, N_KVW), lambda i: (0, 0)),
            pl.BlockSpec((tm, N_KVW), lambda i: (i % tps, 0)),
            pl.BlockSpec((tm, N_KVW), lambda i: (i % tps, 0)),
            pl.BlockSpec((half, tm), lambda i: (0, i % tps)),
            pl.BlockSpec((half, tm), lambda i: (0, i % tps)),
        ],
        out_specs=[
            pl.BlockSpec((N_Q, tm), lambda i: (0, i)),
            pl.BlockSpec((2 * N_KV, tm, N_DH), lambda i: (0, i, 0)),
            pl.BlockSpec((tm, N_KVW), lambda i: (i, 0)),
            pl.BlockSpec((tm, N_KVW), lambda i: (i, 0)),
            pl.BlockSpec((tm // LANES, N_KVW, LANES), lambda i: (i, 0, 0)),
            pl.BlockSpec((tm // LANES, N_KVW, LANES), lambda i: (i, 0, 0)),
        ],
        out_shape=[
            jax.ShapeDtypeStruct((N_Q, t), BF16),
            jax.ShapeDtypeStruct((2 * N_KV, t, N_DH), F32),
            jax.ShapeDtypeStruct((t, N_KVW), BF16),
            jax.ShapeDtypeStruct((t, N_KVW), BF16),
            jax.ShapeDtypeStruct((t // LANES, N_KVW, LANES), BF16),
            jax.ShapeDtypeStruct((t // LANES, N_KVW, LANES), BF16),
        ],
        compiler_params=_params(("arbitrary",)),
        name="proj_n",
    )(x2, n1w, w, qnw, knw, cosn, sinn, cosT, sinT)


def _compress_kernel(x_ref, pe_ref, w1_ref, w2_ref, kcmp_ref, vcmpT_ref):
    nb = x_ref.shape[2]
    half = (CMP_LEN // 2) * N_DH

    def mlp(a, kv):
        pe = pe_ref[kv]
        x = x_ref[a, 0]
        first = _dot((x + pe[:, 0:half]).astype(BF16), w1_ref[kv, 0:half, :])
        second = _dot((x + pe[:, half:]).astype(BF16), w1_ref[kv, half:, :])
        hid = first + pltpu.roll(second, nb - 1, 0)
        return _dot(_gelu(hid).astype(BF16), w2_ref[kv])

    kcmp_ref[...] = jnp.concatenate([mlp(g, 0) for g in range(N_KV)], axis=1).astype(BF16)
    vcmpT_ref[...] = jnp.concatenate([mlp(N_KV + g, 1) for g in range(N_KV)], axis=1).T.astype(BF16)


def _compress(xkv, pe, w1, w2):
    na, batch, nb, width = xkv.shape
    const3 = lambda b: (0, 0, 0)
    return pl.pallas_call(
        _compress_kernel,
        grid=(batch,),
        in_specs=[
            pl.BlockSpec((na, 1, nb, width), lambda b: (0, b, 0, 0)),
            pl.BlockSpec((2, 1, CMP_LEN * N_DH), const3),
            pl.BlockSpec((2, CMP_LEN * N_DH, CMP_HIDDEN), const3),
            pl.BlockSpec((2, CMP_HIDDEN, N_DH), const3),
        ],
        out_specs=[
            pl.BlockSpec((None, nb, N_KVW), lambda b: (b, 0, 0)),
            pl.BlockSpec((None, N_KVW, nb), lambda b: (b, 0, 0)),
        ],
        out_shape=[
            jax.ShapeDtypeStruct((batch, nb, N_KVW), BF16),
            jax.ShapeDtypeStruct((batch, N_KVW, nb), BF16),
        ],
        compiler_params=_params(("arbitrary",)),
        name="compress",
    )(xkv, pe, w1, w2)


def _mlstm_kernel(qT_ref, k_ref, vT_ref, small_ref, smallT_ref, og_ref, onw_ref, o_ref, c_ref, m_ref):
    L = MLSTM_CHUNK
    tb = k_ref.shape[0]

    @pl.when(pl.program_id(1) == 0)
    def _():
        c_ref[...] = jnp.zeros(c_ref.shape, F32)
        m_ref[...] = jnp.zeros(m_ref.shape, F32)

    row = lax.broadcasted_iota(jnp.int32, (L, L), 0)
    col = lax.broadcasted_iota(jnp.int32, (L, L), 1)
    causal = row <= col
    tril = (col <= row).astype(F32)
    triu = causal.astype(F32)
    ones_rows = jnp.ones((16, L), BF16)
    zeros_q = jnp.zeros((M_DQK, L), BF16)

    heads = range(M_HEADS)
    chunks = range(tb // L)
    sls = [slice(c * L, (c + 1) * L) for c in chunks]
    sms = [small_ref[sl, :] for sl in sls]
    bcols = [jnp.dot(tril, sm, preferred_element_type=F32, precision=HIGHEST) for sm in sms]
    brows = [jnp.dot(smallT_ref[0:8, sl], triu, preferred_element_type=F32, precision=HIGHEST) for sl in sls]
    c_state = [c_ref[h] for h in heads]
    m_state = [m_ref[h:h + 1, 0:1] for h in heads]
    for c in chunks:
        sl = sls[c]
        k_pairs = [k_ref[sl, p * LANES:(p + 1) * LANES] for p in range(M_HEADS // 2)]
        qT_pad, s, qc, vT_aug = [], [], [], []
        for h in heads:
            qT_h = qT_ref[h * M_DQK:(h + 1) * M_DQK, sl]
            qT_pad.append(jnp.concatenate([qT_h, zeros_q] if h % 2 == 0 else [zeros_q, qT_h], axis=0))
            s.append(_dot(k_pairs[h // 2], qT_pad[h]))
            qc.append(_dot(c_state[h].astype(BF16), qT_pad[h]))
            vT_aug.append(jnp.concatenate([vT_ref[h * M_DV:(h + 1) * M_DV, sl], ones_rows], axis=0))
        m_t, isc, sw, kw, m_new, decay, scale = [], [], [], [], [], [], []
        for h in heads:
            in_head = (col // M_DQK) == (h % 2)
            b_col = bcols[c][:, M_HEADS + h:M_HEADS + h + 1]
            i_col = sms[c][:, h:h + 1]
            b_row = brows[c][M_HEADS + h:M_HEADS + h + 1, :]
            g = b_row[:, L - 1:L]
            d = jnp.where(causal, b_row + (i_col - b_col), NEG)
            inter = b_row + m_state[h]
            m_t.append(jnp.maximum(inter, jnp.max(d, axis=0, keepdims=True)))
            isc.append(jnp.exp(inter - m_t[h]))
            sw.append((s[h] * jnp.exp(d - m_t[h])).astype(BF16))
            a_col = g - b_col + i_col
            a_max = jnp.max(a_col, axis=0, keepdims=True)
            kw.append(jnp.where(in_head, k_pairs[h // 2].astype(F32) * jnp.exp(a_col - a_max), 0.0).astype(BF16))
            m_new.append(jnp.maximum(g + m_state[h], a_max))
            decay.append(jnp.exp(g + m_state[h] - m_new[h]))
            scale.append(jnp.exp(a_max - m_new[h]))
        sv = [_dot(vT_aug[h], sw[h]) for h in heads]
        dc = [_dot(vT_aug[h], kw[h]) for h in heads]
        for h in heads:
            nd = isc[h] * qc[h] + sv[h]
            den = nd[M_DV:M_DV + 1, :]
            hh = nd[0:M_DV, :] / jnp.maximum(jnp.abs(den), jnp.exp(-m_t[h]))
            hn = hh * lax.rsqrt(jnp.mean(hh * hh, axis=0, keepdims=True) + EPS)
            o_ref[sl, h * M_DV:(h + 1) * M_DV] = hn.T * onw_ref[h:h + 1, :] * og_ref[sl, h * M_DV:(h + 1) * M_DV]
            c_state[h] = decay[h] * c_state[h] + scale[h] * dc[h]
            m_state[h] = m_new[h]
    for h in heads:
        c_ref[h] = c_state[h]
        m_ref[h:h + 1, :] = jnp.broadcast_to(m_state[h], (1, LANES))


def _mlstm(qT, k, vT, small, smallT, og, onw, batch, seq, tb):
    t = k.shape[0]
    nblk = seq // tb
    return pl.pallas_call(
        _mlstm_kernel,
        grid=(batch, nblk),
        in_specs=[
            pl.BlockSpec((M_QK, tb), lambda b, j: (0, b * nblk + j)),
            pl.BlockSpec((tb, M_QK), lambda b, j: (b * nblk + j, 0)),
            pl.BlockSpec((M_V, tb), lambda b, j: (0, b * nblk + j)),
            pl.BlockSpec((tb, LANES), lambda b, j: (b * nblk + j, 0)),
            pl.BlockSpec((32, tb), lambda b, j: (0, b * nblk + j)),
            pl.BlockSpec((tb, M_V), lambda b, j: (b * nblk + j, 0)),
            pl.BlockSpec((8, M_DV), lambda b, j: (0, 0)),
        ],
        out_specs=pl.BlockSpec((tb, M_V), lambda b, j: (b * nblk + j, 0)),
        out_shape=jax.ShapeDtypeStruct((t, M_V), F32),
        scratch_shapes=[pltpu.VMEM((M_HEADS, M_DV + 16, LANES), F32), pltpu.VMEM((8, LANES), F32)],
        compiler_params=_params(("arbitrary", "arbitrary")),
        name="mlstm",
    )(qT, k, vT, small, smallT, og, onw)


def _nsa_kernel(qT_ref, kcmp_ref, vcmpT_ref, ks_ref, e_ref, vsT_ref, kw_ref, vwT_ref, gT_ref,
                o_ref, rhs_ref, ps_ref, ocmp_ref, s0_ref, s1_ref, c0_ref, c1_ref, m_ref, acc_ref, *, nsel, n_rounds):
    qi = pl.program_id(1)
    tq = Q_TILE
    nb = kcmp_ref.shape[0]
    nselp = e_ref.shape[1]
    q0 = qi * tq
    wide = N_HG * tq
    lane_w = lax.broadcasted_iota(jnp.int32, (1, wide), 1)
    tpos_w = q0 + (lane_w % tq)
    tpos = q0 + lax.broadcasted_iota(jnp.int32, (1, tq), 1)
    zeros_q = jnp.zeros((N_DH, wide), BF16)
    ones_v = jnp.ones((16, LANES), BF16)
    grows = [slice(g * N_DH, (g + 1) * N_DH) for g in range(N_KV)]
    gcols = [slice(g * wide, (g + 1) * wide) for g in range(N_KV)]

    def values(v_ref, first, count, rows):
        return jnp.concatenate(
            [jnp.concatenate([v_ref[first + j, rows, :], ones_v], axis=0) for j in range(count)], axis=1)

    start_w = pl.multiple_of(jnp.maximum(q0 - WINDOW, 0), LANES)
    qpads, sw = [], []
    for g in range(N_KV):
        q4 = jnp.concatenate(
            [qT_ref[(g * N_HG + h) * N_DH:(g * N_HG + h + 1) * N_DH, :] for h in range(N_HG)], axis=1)
        qpads.append(jnp.concatenate([q4, zeros_q] if g == 0 else [zeros_q, q4], axis=0))
        rhs_ref[0:2 * N_DH, gcols[g]] = qpads[g]
        sw.append(_dot(kw_ref[pl.ds(start_w, WIN_SPAN), :], qpads[g]))

    ratio = SEL_BLOCK // CMP_STRIDE

    def compressed_and_select(nrows):
        nbv = nrows * ratio
        cend = lax.broadcasted_iota(jnp.int32, (nbv, 1), 0) * CMP_STRIDE + (CMP_LEN - 1)
        cmask = cend <= tpos_w
        any_visible = tpos_w >= CMP_LEN - 1
        sc = [_dot(kcmp_ref[0:nbv, :], qpads[g]) for g in range(N_KV)]
        imp = []
        for g in range(N_KV):
            s = jnp.where(cmask, sc[g], NEG)
            pc = jnp.exp2(s - jnp.max(s, axis=0, keepdims=True))
            lc = jnp.sum(pc, axis=0, keepdims=True)
            pc = pc * jnp.where(any_visible, 1.0 / lc, 0.0)
            ocmp_ref[g] = _dot(vcmpT_ref[grows[g], 0:nbv], pc.astype(BF16))
            psum = pc[:, 0:tq]
            for h in range(1, N_HG):
                psum = psum + pc[:, h * tq:(h + 1) * tq]
            parts = []
            for cchunk in range(tq // LANES):
                ps_ref[g, cchunk, 0:8, :] = jnp.zeros((8, LANES), F32)
                ps_ref[g, cchunk, 8:8 + nbv, :] = psum[:, cchunk * LANES:(cchunk + 1) * LANES]
                acc = None
                for k in range(-((CMP_LEN - 1) // CMP_STRIDE), ratio):
                    part = ps_ref[g, cchunk, pl.ds(8 + k, nrows, stride=ratio), :]
                    acc = part if acc is None else acc + part
                parts.append(acc)
            imp.append(jnp.concatenate(parts, axis=1))

        jblk = lax.broadcasted_iota(jnp.int32, (nrows, tq), 0)
        cur = tpos // SEL_BLOCK
        forced = (jblk == 0) | (jblk == cur) | (jblk == cur - 1)
        cand = (jblk >= 1) & (jblk <= cur - 2)
        jblk_f = jblk.astype(F32)
        val = [jnp.where(cand, imp[g], -jnp.inf) for g in range(N_KV)]
        for _ in range(n_rounds):
            for g in range(N_KV):
                mx = jnp.max(val[g], axis=0, keepdims=True)
                first = jnp.min(jnp.where(val[g] == mx, jblk_f, float(nrows)), axis=0, keepdims=True)
                val[g] = jnp.where(jblk_f == first, -jnp.inf, val[g])
        for g in range(N_KV):
            picked = cand & (val[g] == -jnp.inf)
            bias = jnp.where(forced | picked, 0.0, NEG).astype(BF16)
            if nselp > nrows:
                bias = jnp.concatenate([bias, jnp.zeros((nselp - nrows, tq), BF16)], axis=0)
            rhs_ref[2 * N_DH:, gcols[g]] = jnp.concatenate([bias] * N_HG, axis=1)

    n_var = 4 if nsel % 32 == 0 else 1
    if n_var == 1:
        compressed_and_select(nsel)
    else:
        tiles_per_var = (nsel * SEL_BLOCK // tq) // n_var
        for v in range(n_var):
            pl.when(qi // tiles_per_var == v)(functools.partial(compressed_and_select, (v + 1) * nsel // n_var))

    dist = (tpos_w - start_w) - lax.broadcasted_iota(jnp.int32, (WIN_SPAN, 1), 0)
    wmask = lax.bitcast_convert_type(dist, jnp.uint32) < WINDOW

    def scores(kt, g):
        start = pl.multiple_of(kt * KEY_TILE, KEY_TILE)
        lhs = jnp.concatenate([ks_ref[pl.ds(start, KEY_TILE), :], e_ref[pl.ds(start, KEY_TILE), :]], axis=1)
        return _dot(lhs, rhs_ref[:, g * wide:(g + 1) * wide])

    def produce(kt, s_ref, c_ref):
        for g in range(N_KV):
            s = scores(kt, g)
            s_ref[g] = s
            c_ref[g] = jnp.max(s, axis=0, keepdims=True)

    def consume(kt, s_ref, c_ref, causal_tile=False):
        for g in range(N_KV):
            s = s_ref[g]
            if causal_tile:
                kpos = kt * KEY_TILE + lax.broadcasted_iota(jnp.int32, (KEY_TILE, 1), 0)
                s = jnp.where(kpos <= tpos_w, s, NEG)
                smax = jnp.max(s, axis=0, keepdims=True)
            else:
                smax = c_ref[g]
            m = m_ref[g]
            m_new = jnp.maximum(m, smax)
            p = jnp.exp2(s - m_new).astype(BF16)
            vt = values(vsT_ref, kt * (KEY_TILE // LANES), KEY_TILE // LANES, slice(g * N_DH, (g + 1) * N_DH))
            acc_ref[g] = jnp.exp2(m - m_new) * acc_ref[g] + _dot(vt, p)
            m_ref[g] = m_new

    n_full = qi // (KEY_TILE // tq)
    odd = n_full % 2
    m_ref[...] = jnp.full(m_ref.shape, NEG, F32)
    acc_ref[...] = jnp.zeros(acc_ref.shape, F32)
    produce(0, s0_ref, c0_ref)

    o_win = []
    for g in range(N_KV):
        s = jnp.where(wmask, sw[g], NEG)
        pw = jnp.exp2(s - jnp.max(s, axis=0, keepdims=True))
        ow = _dot(values(vwT_ref, start_w // LANES, WIN_SPAN // LANES, grows[g]), pw.astype(BF16))
        o_win.append(ow[0:N_DH, :] * (1.0 / ow[N_DH:N_DH + 1, :]))

    def pair(kt):
        produce(kt + 1, s1_ref, c1_ref)
        consume(kt, s0_ref, c0_ref)
        produce(kt + 2, s0_ref, c0_ref)
        consume(kt + 1, s1_ref, c1_ref)

    def quad(j, _):
        pair(4 * j)
        pair(4 * j + 2)
        return 0

    n_quad = n_full // 4
    lax.fori_loop(0, n_quad, quad, 0)

    @pl.when(n_full - 4 * n_quad >= 2)
    def _():
        pair(4 * n_quad)

    @pl.when(odd == 0)
    def _():
        consume(n_full, s0_ref, c0_ref, causal_tile=True)

    @pl.when(odd == 1)
    def _():
        produce(n_full, s1_ref, c1_ref)
        consume(n_full - 1, s0_ref, c0_ref)
        consume(n_full, s1_ref, c1_ref, causal_tile=True)

    outs = []
    for g in range(N_KV):
        o_slc = acc_ref[g, 0:N_DH, :] * (1.0 / acc_ref[g, N_DH:N_DH + 1, :])
        for h in range(N_HG):
            cs = slice(h * tq, (h + 1) * tq)
            r = 2 * M_HEADS + (g * N_HG + h) * 3
            outs.append(gT_ref[r:r + 1, :] * ocmp_ref[g, :, cs] + gT_ref[r + 1:r + 2, :] * o_slc[:, cs]
                        + gT_ref[r + 2:r + 3, :] * o_win[g][:, cs])

    o_ref[...] = jnp.concatenate(outs, axis=0).T


def _nsa(qT, kcmp, vcmpT, ks, emap, vsT, kw, vwT, smallT, batch, seq):
    t = qT.shape[1]
    nq = seq // Q_TILE
    nb = kcmp.shape[1]
    nsel = seq // SEL_BLOCK
    nselp = emap.shape[1]
    wide = N_HG * Q_TILE
    n_rounds = max(min(SEL_TOPN, nsel) - 3, 0)
    kern = functools.partial(_nsa_kernel, nsel=nsel, n_rounds=n_rounds)
    return pl.pallas_call(
        kern,
        grid=(batch, nq),
        in_specs=[
            pl.BlockSpec((N_Q, Q_TILE), lambda b, i: (0, b * nq + i)),
            pl.BlockSpec((None, nb, N_KVW), lambda b, i: (b, 0, 0)),
            pl.BlockSpec((None, N_KVW, nb), lambda b, i: (b, 0, 0)),
            pl.BlockSpec((seq, N_KVW), lambda b, i: (b, 0)),
            pl.BlockSpec((seq, nselp), lambda b, i: (0, 0)),
            pl.BlockSpec((seq // LANES, N_KVW, LANES), lambda b, i: (b, 0, 0)),
            pl.BlockSpec((seq, N_KVW), lambda b, i: (b, 0)),
            pl.BlockSpec((seq // LANES, N_KVW, LANES), lambda b, i: (b, 0, 0)),
            pl.BlockSpec((32, Q_TILE), lambda b, i: (0, b * nq + i)),
        ],
        out_specs=pl.BlockSpec((Q_TILE, N_Q), lambda b, i: (b * nq + i, 0)),
        out_shape=jax.ShapeDtypeStruct((t, N_Q), F32),
        scratch_shapes=[
            pltpu.VMEM((2 * N_DH + nselp, N_KV * wide), BF16),
            pltpu.VMEM((N_KV, Q_TILE // LANES, nb + 8, LANES), F32),
            pltpu.VMEM((N_KV, N_DH, wide), F32),
            pltpu.VMEM((N_KV, KEY_TILE, wide), F32),
            pltpu.VMEM((N_KV, KEY_TILE, wide), F32),
            pltpu.VMEM((N_KV, 1, wide), F32),
            pltpu.VMEM((N_KV, 1, wide), F32),
            pltpu.VMEM((N_KV, 1, wide), F32),
            pltpu.VMEM((N_KV, N_DH + 16, wide), F32),
        ],
        compiler_params=_params(("arbitrary", "arbitrary")),
        name="nsa",
    )(qT, kcmp, vcmpT, ks, emap, vsT, kw, vwT, smallT)


def _merge_kernel(x_ref, n1w_ref, wg_ref, gb_ref, hm_ref, on_ref, wm_ref, wn_ref, wo_ref, o_ref):
    x = x_ref[...]
    hn = _rmsnorm_rows(x, n1w_ref[...]).astype(BF16)
    gm = _sigmoid(_dot(hn, wg_ref[:, 0:D_MODEL]) + gb_ref[0:1, :])
    gn = _sigmoid(_dot(hn, wg_ref[:, D_MODEL:]) + gb_ref[1:2, :])
    y = gm * _dot(hm_ref[...].astype(BF16), wm_ref[...]) + gn * _dot(on_ref[...].astype(BF16), wn_ref[...])
    o_ref[...] = x + _dot(y.astype(BF16), wo_ref[...])


def _merge(x2, n1w, wg, gb, hm, on, wm, wn, wo, tm):
    t = x2.shape[0]
    const = lambda i: (0, 0)
    return pl.pallas_call(
        _merge_kernel,
        grid=(t // tm,),
        in_specs=[
            pl.BlockSpec((tm, D_MODEL), lambda i: (i, 0)),
            pl.BlockSpec((1, D_MODEL), const),
            pl.BlockSpec((D_MODEL, 2 * D_MODEL), const),
            pl.BlockSpec((2, D_MODEL), const),
            pl.BlockSpec((tm, M_V), lambda i: (i, 0)),
            pl.BlockSpec((tm, N_Q), lambda i: (i, 0)),
            pl.BlockSpec((M_V, D_MODEL), const),
            pl.BlockSpec((N_Q, D_MODEL), const),
            pl.BlockSpec((D_MODEL, D_MODEL), const),
        ],
        out_specs=pl.BlockSpec((tm, D_MODEL), lambda i: (i, 0)),
        out_shape=jax.ShapeDtypeStruct((t, D_MODEL), F32),
        compiler_params=_params(("arbitrary",)),
        name="merge",
    )(x2, n1w, wg, gb, hm, on, wm, wn, wo)


def _ffn_kernel(x_ref, n2w_ref, wup_ref, cw_ref, cb_ref, wdn_ref, o_ref, buf_ref, *, tiles_per_seq):
    i = pl.program_id(0)
    tm = x_ref.shape[0]
    x = x_ref[...]
    hn = _rmsnorm_rows(x, n2w_ref[...]).astype(BF16)

    @pl.when(i % tiles_per_seq == 0)
    def _():
        buf_ref[0:8, :] = jnp.zeros((8, D_FF), F32)

    a = _dot(hn, wup_ref[:, 0:D_FF])
    buf_ref[8:8 + tm, :] = a
    acc = cb_ref[...] + cw_ref[FFN_CONV - 1:FFN_CONV, :] * a
    for k in range(FFN_CONV - 1):
        acc = acc + cw_ref[k:k + 1, :] * buf_ref[8 - (FFN_CONV - 1) + k:8 - (FFN_CONV - 1) + k + tm, :]
    buf_ref[0:8, :] = buf_ref[tm:tm + 8, :]
    v = _dot(hn, wup_ref[:, D_FF:])
    o_ref[...] = x + _dot((_gelu(acc) * v).astype(BF16), wdn_ref[...])


def _ffn(x2, n2w, wup, cw, cb, wdn, seq, tm):
    t = x2.shape[0]
    const = lambda i: (0, 0)
    kern = functools.partial(_ffn_kernel, tiles_per_seq=seq // tm)
    return pl.pallas_call(
        kern,
        grid=(t // tm,),
        in_specs=[
            pl.BlockSpec((tm, D_MODEL), lambda i: (i, 0)),
            pl.BlockSpec((1, D_MODEL), const),
            pl.BlockSpec((D_MODEL, 2 * D_FF), const, pipeline_mode=pl.Buffered(1)),
            pl.BlockSpec((FFN_CONV, D_FF), const),
            pl.BlockSpec((1, D_FF), const),
            pl.BlockSpec((D_FF, D_MODEL), const, pipeline_mode=pl.Buffered(1)),
        ],
        out_specs=pl.BlockSpec((tm, D_MODEL), lambda i: (i, 0)),
        out_shape=jax.ShapeDtypeStruct((t, D_MODEL), F32),
        scratch_shapes=[pltpu.VMEM((tm + 8, D_FF), F32)],
        compiler_params=_params(("arbitrary",)),
        name="ffn",
    )(x2, n2w, wup, cw, cb, wdn)


def _cols(w, *names):
    return jnp.concatenate([w[:, _OFF[n][0]:_OFF[n][1]] for n in names], axis=1)


def _layer(x, n1w, w_in, m_conv_w, m_conv_b, m_igate_b, m_fgate_b, m_out_norm_w,
           q_norm_w, kcmp_norm_w, kslc_norm_w, kwin_norm_w,
           cmp_k_pe, cmp_k_w1, cmp_k_w2, cmp_v_pe, cmp_v_w1, cmp_v_w2,
           w_up_m, w_up_n, merge_gate_b, w_out, norm2_w, ffn_w_up, ffn_conv_w, ffn_conv_b, ffn_w_down):
    batch, seq, _ = x.shape
    t = batch * seq
    x2 = x.reshape(t, D_MODEL)
    n1w2 = n1w.reshape(1, D_MODEL)
    tm = 256
    tm_proj = 512

    w_m = _cols(w_in, "mq", "mk", "mo").astype(BF16)
    w_mvT = _cols(w_in, "mv", "mi", "mf", "ng").T.astype(BF16)
    w_n = _cols(w_in, "nq", "kc", "vc", "ks", "vs", "kw", "vw").astype(BF16)
    w_g = _cols(w_in, "gm", "gn").astype(BF16)
    sbias = jnp.concatenate([m_igate_b, m_fgate_b, jnp.zeros((3 * N_HEADS,), F32)]).reshape(32, 1)

    half = N_DH // 2
    pos = jnp.arange(seq, dtype=F32)
    inv = ROPE_THETA ** (-jnp.arange(0, N_DH, 2, dtype=F32) / N_DH)
    ang = pos[:, None] * inv[None, :]
    cos, sin = jnp.cos(ang), jnp.sin(ang)
    cosn = jnp.tile(cos, (1, N_KVW // half))
    sinn = jnp.tile(jnp.concatenate([-sin, sin], axis=1), (1, N_KV))
    cosT, sinT = cos.T, sin.T
    knw = jnp.zeros((8, N_KVW), F32).at[0:3].set(
        jnp.stack([jnp.tile(w, N_KV) for w in (kcmp_norm_w, kslc_norm_w, kwin_norm_w)]))

    qT_m, k_m, vT_m, og, small, smallT = _proj_m(
        x2, n1w2, w_m, w_mvT, m_conv_w, m_conv_b.reshape(1, -1), sbias, seq, tm_proj)
    qT, kvc, ks, kw, vsT, vwT = _proj_n(
        x2, n1w2, w_n, q_norm_w.reshape(N_DH, 1), knw, cosn, sinn, cosT, sinT, seq, tm_proj)

    nb = seq // CMP_STRIDE
    xkv = kvc.reshape(2 * N_KV, batch, nb, CMP_STRIDE * N_DH)
    pe = jnp.stack([cmp_k_pe.reshape(1, -1), cmp_v_pe.reshape(1, -1)])
    w1 = jnp.stack([cmp_k_w1, cmp_v_w1]).astype(BF16)
    w2 = jnp.stack([cmp_k_w2, cmp_v_w2]).astype(BF16)
    kcmp, vcmpT = _compress(xkv, pe, w1, w2)

    onw = jnp.zeros((8, M_DV), F32).at[0:M_HEADS].set(m_out_norm_w)
    hm = _mlstm(qT_m, k_m, vT_m, small, smallT, og, onw, batch, seq, 4 * MLSTM_CHUNK)
    nselp = -(-(seq // SEL_BLOCK) // LANES) * LANES
    emap = (np.arange(seq)[:, None] // SEL_BLOCK == np.arange(nselp)[None, :]).astype(np.float32)
    on = _nsa(qT, kcmp, vcmpT, ks, jnp.asarray(emap, dtype=BF16), vsT, kw, vwT, smallT, batch, seq)

    x1 = _merge(x2, n1w2, w_g, merge_gate_b, hm, on, w_up_m.astype(BF16), w_up_n.astype(BF16),
                w_out.astype(BF16), tm_proj)
    out = _ffn(x1, norm2_w.reshape(1, D_MODEL), ffn_w_up.astype(BF16), ffn_conv_w, ffn_conv_b.reshape(1, -1),
               ffn_w_down.astype(BF16), seq, tm)
    return out.reshape(batch, seq, D_MODEL)


def kernel(x, norm1_w, w_in, m_conv_w, m_conv_b, m_igate_b, m_fgate_b, m_out_norm_w, q_norm_w, kcmp_norm_w,
           kslc_norm_w, kwin_norm_w, cmp_k_pe, cmp_k_w1, cmp_k_w2, cmp_v_pe, cmp_v_w1, cmp_v_w2, w_up_m, w_up_n,
           merge_gate_b, w_out, norm2_w, ffn_w_up, ffn_conv_w, ffn_conv_b, ffn_w_down):
    params = (norm1_w, w_in, m_conv_w, m_conv_b, m_igate_b, m_fgate_b, m_out_norm_w, q_norm_w, kcmp_norm_w,
              kslc_norm_w, kwin_norm_w, cmp_k_pe, cmp_k_w1, cmp_k_w2, cmp_v_pe, cmp_v_w1, cmp_v_w2, w_up_m, w_up_n,
              merge_gate_b, w_out, norm2_w, ffn_w_up, ffn_conv_w, ffn_conv_b, ffn_w_down)
    for layer in range(norm1_w.shape[0]):
        x = _layer(x, *[p[layer] for p in params])
    return x
```

```python
import functools
import math

import jax
import jax.numpy as jnp
import numpy as np
from jax import lax
from jax.experimental import pallas as pl
from jax.experimental.pallas import tpu as pltpu

D_MODEL = 1024
EPS = 1e-6
ROPE_THETA = 10000.0
NEG = -1e30
M_HEADS = 4
M_DQK = 64
M_DV = 128
M_CONV = 4
M_QK = M_HEADS * M_DQK
M_V = M_HEADS * M_DV
N_HEADS = 8
N_KV = 2
N_HG = N_HEADS // N_KV
N_DH = 64
N_Q = N_HEADS * N_DH
N_KVW = N_KV * N_DH
CMP_LEN = 32
CMP_STRIDE = 16
CMP_HIDDEN = 256
SEL_BLOCK = 64
SEL_TOPN = 16
WINDOW = 512
D_FF = 2816
FFN_CONV = 3

_OFF = {}
_o = 0
for _name, _size in (("mq", M_QK), ("mk", M_QK), ("mv", M_V), ("mo", M_V), ("mi", M_HEADS), ("mf", M_HEADS),
                     ("nq", N_Q), ("kc", N_KVW), ("vc", N_KVW), ("ks", N_KVW), ("vs", N_KVW), ("kw", N_KVW),
                     ("vw", N_KVW), ("ng", 3 * N_HEADS), ("gm", D_MODEL), ("gn", D_MODEL)):
    _OFF[_name] = (_o, _o + _size)
    _o += _size

LANES = 128
MLSTM_CHUNK = 128
SUB_ROWS = 256
Q_TILE = 128
KEY_TILE = 256
WIN_SPAN = WINDOW + Q_TILE
VMEM_LIMIT = 56 * 1024 * 1024

LOG2E = math.log2(math.e)

F32 = jnp.float32
BF16 = jnp.bfloat16
HIGHEST = lax.Precision.HIGHEST


def _dot(a, b):
    return jnp.dot(a, b, preferred_element_type=F32)


def _dot_nt(a, b):
    return lax.dot_general(a, b, (((1,), (1,)), ((), ())), preferred_element_type=F32)


def _rmsnorm_rows(x, w):
    return x * lax.rsqrt(jnp.mean(x * x, axis=-1, keepdims=True) + EPS) * w


def _sigmoid(x):
    return 1.0 / (1.0 + jnp.exp(-x))


def _gelu(x):
    return 0.5 * x * (1.0 + lax.erf(x * (1.0 / math.sqrt(2.0))))


def _params(sem):
    return pltpu.CompilerParams(dimension_semantics=sem, vmem_limit_bytes=VMEM_LIMIT)


def _proj_m_kernel(x_ref, n1w_ref, w_ref, wvT_ref, cw_ref, cb_ref, sb_ref,
                   qT_ref, k_ref, vT_ref, og_ref, small_ref, smallT_ref, buf_ref, *, tiles_per_seq):
    i = pl.program_id(0)
    tm = x_ref.shape[0]
    sub = SUB_ROWS

    @pl.when(i % tiles_per_seq == 0)
    def _():
        buf_ref[0:8, :] = jnp.zeros((8, 2 * M_QK), F32)

    hns = [_rmsnorm_rows(x_ref[r * sub:(r + 1) * sub, :], n1w_ref[...]).astype(BF16) for r in range(tm // sub)]
    for r in range(tm // sub):
        rows = slice(r * sub, (r + 1) * sub)
        hn = hns[r]
        qk = _dot(hn, w_ref[:, 0:2 * M_QK])
        buf_ref[8:8 + sub, :] = qk
        acc = cb_ref[...] + cw_ref[M_CONV - 1:M_CONV, :] * qk
        for k in range(M_CONV - 1):
            acc = acc + cw_ref[k:k + 1, :] * buf_ref[8 - (M_CONV - 1) + k:8 - (M_CONV - 1) + k + sub, :]
        buf_ref[0:8, :] = buf_ref[sub:sub + 8, :]
        act = acc * _sigmoid(acc)
        qT_ref[:, rows] = (act[:, 0:M_QK] * (M_DQK ** -0.5)).T.astype(BF16)
        k_ref[rows, :] = act[:, M_QK:2 * M_QK].astype(BF16)
        vs = _dot_nt(wvT_ref[...], hn)
        vT_ref[:, rows] = vs[0:M_V, :].astype(BF16)
        og_ref[rows, :] = _sigmoid(_dot(hn, w_ref[:, 2 * M_QK:2 * M_QK + M_V]))
        smT = vs[M_V:, :] + sb_ref[...]
        rowi = lax.broadcasted_iota(jnp.int32, smT.shape, 0)
        logsig = jnp.minimum(smT, 0.0) - jnp.log1p(jnp.exp(-jnp.abs(smT)))
        smT = jnp.where(rowi < M_HEADS, smT, jnp.where(rowi < 2 * M_HEADS, logsig, _sigmoid(smT)))
        smallT_ref[:, rows] = smT
        small_ref[rows, :] = jnp.concatenate([smT, jnp.zeros((LANES - 32, sub), F32)], axis=0).T


def _proj_m(x2, n1w, w, wvT, cw, cb, sb, seq, tm):
    t = x2.shape[0]
    ncol = w.shape[1]
    kern = functools.partial(_proj_m_kernel, tiles_per_seq=seq // tm)
    return pl.pallas_call(
        kern,
        grid=(t // tm,),
        in_specs=[
            pl.BlockSpec((tm, D_MODEL), lambda i: (i, 0)),
            pl.BlockSpec((1, D_MODEL), lambda i: (0, 0)),
            pl.BlockSpec((D_MODEL, ncol), lambda i: (0, 0)),
            pl.BlockSpec((M_V + 32, D_MODEL), lambda i: (0, 0)),
            pl.BlockSpec((M_CONV, 2 * M_QK), lambda i: (0, 0)),
            pl.BlockSpec((1, 2 * M_QK), lambda i: (0, 0)),
            pl.BlockSpec((32, 1), lambda i: (0, 0)),
        ],
        out_specs=[
            pl.BlockSpec((M_QK, tm), lambda i: (0, i)),
            pl.BlockSpec((tm, M_QK), lambda i: (i, 0)),
            pl.BlockSpec((M_V, tm), lambda i: (0, i)),
            pl.BlockSpec((tm, M_V), lambda i: (i, 0)),
            pl.BlockSpec((tm, LANES), lambda i: (i, 0)),
            pl.BlockSpec((32, tm), lambda i: (0, i)),
        ],
        out_shape=[
            jax.ShapeDtypeStruct((M_QK, t), BF16),
            jax.ShapeDtypeStruct((t, M_QK), BF16),
            jax.ShapeDtypeStruct((M_V, t), BF16),
            jax.ShapeDtypeStruct((t, M_V), F32),
            jax.ShapeDtypeStruct((t, LANES), F32),
            jax.ShapeDtypeStruct((32, t), F32),
        ],
        scratch_shapes=[pltpu.VMEM((SUB_ROWS + 8, 2 * M_QK), F32)],
        compiler_params=_params(("arbitrary",)),
        name="proj_m",
    )(x2, n1w, w, wvT, cw, cb, sb)


def _proj_n_kernel(x_ref, n1w_ref, w_ref, qnw_ref, knw_ref, cosn_ref, sinn_ref, cosT_ref, sinT_ref,
                   qT_ref, kvb_ref, ks_ref, kw_ref, vsT_ref, vwT_ref, regroup_ref):
    tm = x_ref.shape[0]
    hn = _rmsnorm_rows(x_ref[...], n1w_ref[...]).astype(BF16)
    qT = _dot(hn, w_ref[:, 0:N_Q]).T
    cosT = cosT_ref[...]
    sinT = sinT_ref[...]
    qnw = qnw_ref[...]
    half = N_DH // 2
    for h in range(N_HEADS):
        xh = qT[h * N_DH:(h + 1) * N_DH, :]
        xn = xh * lax.rsqrt(jnp.mean(xh * xh, axis=0, keepdims=True) + EPS) * qnw
        x1 = xn[0:half, :]
        x2 = xn[half:, :]
        o = jnp.concatenate([x1 * cosT - x2 * sinT, x2 * cosT + x1 * sinT], axis=0) * (LOG2E * N_DH ** -0.5)
        qT_ref[h * N_DH:(h + 1) * N_DH, :] = o.astype(BF16)

    cosn = cosn_ref[...]
    sinn = sinn_ref[...]
    li = lax.broadcasted_iota(jnp.int32, (N_KVW, N_KVW), 0)
    lj = lax.broadcasted_iota(jnp.int32, (N_KVW, N_KVW), 1)
    head_sum = (li // N_DH == lj // N_DH).astype(BF16)
    swap_half = (lj == li + jnp.where(li % N_DH < half, half, -half)).astype(BF16)
    head_sum = jnp.concatenate([head_sum, head_sum], axis=0)
    swap_half = jnp.concatenate([swap_half, swap_half], axis=0)

    def lane_map(v, m01x2):
        hi = v.astype(BF16)
        lo = (v - hi.astype(F32)).astype(BF16)
        return _dot(jnp.concatenate([hi, lo], axis=1), m01x2)

    c0 = N_Q
    k_raw = [_dot(hn, w_ref[:, c0 + 2 * i * N_KVW:c0 + (2 * i + 1) * N_KVW]) for i in range(3)]
    vc, vs, vw = [_dot(hn, w_ref[:, c0 + (2 * i + 1) * N_KVW:c0 + (2 * i + 2) * N_KVW]) for i in range(3)]
    ms = [lane_map(k * k, head_sum) * (1.0 / N_DH) for k in k_raw]
    kn = [k_raw[i] * lax.rsqrt(ms[i] + EPS) * knw_ref[i:i + 1, :] for i in range(3)]
    kc, ks, kw = [kn[i] * cosn + lane_map(kn[i], swap_half) * sinn for i in range(3)]
    lane_kv = lax.broadcasted_iota(jnp.int32, (tm // CMP_STRIDE, N_KVW), 1)
    for a, arr in enumerate((kc, vc)):
        regroup_ref[...] = arr
        for l in range(0, CMP_STRIDE, 2):
            even = regroup_ref[pl.ds(l, tm // CMP_STRIDE, stride=CMP_STRIDE), :]
            odd = regroup_ref[pl.ds(l + 1, tm // CMP_STRIDE, stride=CMP_STRIDE), :]
            cols = slice((l // 2) * N_KVW, (l // 2 + 1) * N_KVW)
            kvb_ref[a * N_KV, :, cols] = jnp.where(lane_kv < N_DH, even, pltpu.roll(odd, N_DH, 1))
            kvb_ref[a * N_KV + 1, :, cols] = jnp.where(lane_kv < N_DH, pltpu.roll(even, N_DH, 1), odd)
    ks_ref[...] = ks.astype(BF16)
    kw_ref[...] = kw.astype(BF16)
    vsT = vs.T.astype(BF16)
    vwT = vw.T.astype(BF16)
    for j in range(tm // LANES):
        vsT_ref[j] = vsT[:, j * LANES:(j + 1) * LANES]
        vwT_ref[j] = vwT[:, j * LANES:(j + 1) * LANES]


def _proj_n(x2, n1w, w, qnw, knw, cosn, sinn, cosT, sinT, seq, tm):
    t = x2.shape[0]
    ncol = w.shape[1]
    tps = seq // tm
    half = N_DH // 2
    return pl.pallas_call(
        _proj_n_kernel,
        grid=(t // tm,),
        in_specs=[
            pl.BlockSpec((tm, D_MODEL), lambda i: (i, 0)),
            pl.BlockSpec((1, D_MODEL), lambda i: (0, 0)),
            pl.BlockSpec((D_MODEL, ncol), lambda i: (0, 0)),
            pl.BlockSpec((N_DH, 1), lambda i: (0, 0)),
            pl.BlockSpec((8, N_KVW), lambda i: (0, 0)),
            pl.BlockSpec((tm, N_KVW), lambda i: (i % tps, 0)),
            pl.BlockSpec((tm, N_KVW), lambda i: (i % tps, 0)),
            pl.BlockSpec((half, tm), lambda i: (0, i % tps)),
            pl.BlockSpec((half, tm), lambda i: (0, i % tps)),
        ],
        out_specs=[
            pl.BlockSpec((N_Q, tm), lambda i: (0, i)),
            pl.BlockSpec((2 * N_KV, tm // CMP_STRIDE, CMP_STRIDE * N_DH), lambda i: (0, i, 0)),
            pl.BlockSpec((tm, N_KVW), lambda i: (i, 0)),
            pl.BlockSpec((tm, N_KVW), lambda i: (i, 0)),
            pl.BlockSpec((tm // LANES, N_KVW, LANES), lambda i: (i, 0, 0)),
            pl.BlockSpec((tm // LANES, N_KVW, LANES), lambda i: (i, 0, 0)),
        ],
        out_shape=[
            jax.ShapeDtypeStruct((N_Q, t), BF16),
            jax.ShapeDtypeStruct((2 * N_KV, t // CMP_STRIDE, CMP_STRIDE * N_DH), F32),
            jax.ShapeDtypeStruct((t, N_KVW), BF16),
            jax.ShapeDtypeStruct((t, N_KVW), BF16),
            jax.ShapeDtypeStruct((t // LANES, N_KVW, LANES), BF16),
            jax.ShapeDtypeStruct((t // LANES, N_KVW, LANES), BF16),
        ],
        scratch_shapes=[pltpu.VMEM((tm, N_KVW), F32)],
        compiler_params=_params(("arbitrary",)),
        name="proj_n",
    )(x2, n1w, w, qnw, knw, cosn, sinn, cosT, sinT)


def _compress_kernel(x_ref, pe_ref, w1_ref, w2_ref, kcmp_ref, vcmpT_ref):
    nb = x_ref.shape[2]
    half = (CMP_LEN // 2) * N_DH

    def mlp(a, kv):
        pe = pe_ref[kv]
        x = x_ref[a, 0]
        first = _dot((x + pe[:, 0:half]).astype(BF16), w1_ref[kv, 0:half, :])
        second = _dot((x + pe[:, half:]).astype(BF16), w1_ref[kv, half:, :])
        hid = first + pltpu.roll(second, nb - 1, 0)
        return _dot(_gelu(hid).astype(BF16), w2_ref[kv])

    kcmp_ref[...] = jnp.concatenate([mlp(g, 0) for g in range(N_KV)], axis=1).astype(BF16)
    vcmpT_ref[...] = jnp.concatenate([mlp(N_KV + g, 1) for g in range(N_KV)], axis=1).T.astype(BF16)


def _compress(xkv, pe, w1, w2):
    na, batch, nb, width = xkv.shape
    const3 = lambda b: (0, 0, 0)
    return pl.pallas_call(
        _compress_kernel,
        grid=(batch,),
        in_specs=[
            pl.BlockSpec((na, 1, nb, width), lambda b: (0, b, 0, 0)),
            pl.BlockSpec((2, 1, CMP_LEN * N_DH), const3),
            pl.BlockSpec((2, CMP_LEN * N_DH, CMP_HIDDEN), const3),
            pl.BlockSpec((2, CMP_HIDDEN, N_DH), const3),
        ],
        out_specs=[
            pl.BlockSpec((None, nb, N_KVW), lambda b: (b, 0, 0)),
            pl.BlockSpec((None, N_KVW, nb), lambda b: (b, 0, 0)),
        ],
        out_shape=[
            jax.ShapeDtypeStruct((batch, nb, N_KVW), BF16),
            jax.ShapeDtypeStruct((batch, N_KVW, nb), BF16),
        ],
        compiler_params=_params(("arbitrary",)),
        name="compress",
    )(xkv, pe, w1, w2)


def _mlstm_kernel(qT_ref, k_ref, vT_ref, small_ref, smallT_ref, og_ref, onw_ref, o_ref, c_ref, m_ref):
    L = MLSTM_CHUNK
    tb = k_ref.shape[0]

    @pl.when(pl.program_id(1) == 0)
    def _():
        c_ref[...] = jnp.zeros(c_ref.shape, F32)
        m_ref[...] = jnp.zeros(m_ref.shape, F32)

    row = lax.broadcasted_iota(jnp.int32, (L, L), 0)
    col = lax.broadcasted_iota(jnp.int32, (L, L), 1)
    causal = row <= col
    tril = (col <= row).astype(F32)
    triu = causal.astype(F32)
    ones_rows = jnp.ones((16, L), BF16)
    zeros_q = jnp.zeros((M_DQK, L), BF16)

    heads = range(M_HEADS)
    chunks = range(tb // L)
    sls = [slice(c * L, (c + 1) * L) for c in chunks]
    sms = [small_ref[sl, :] for sl in sls]
    bcols = [jnp.dot(tril, sm, preferred_element_type=F32, precision=HIGHEST) for sm in sms]
    brows = [jnp.dot(smallT_ref[0:8, sl], triu, preferred_element_type=F32, precision=HIGHEST) for sl in sls]
    c_state = [c_ref[h] for h in heads]
    m_state = [m_ref[h:h + 1, 0:1] for h in heads]
    for c in chunks:
        sl = sls[c]
        k_pairs = [k_ref[sl, p * LANES:(p + 1) * LANES] for p in range(M_HEADS // 2)]
        qT_pad, s, qc, vT_aug = [], [], [], []
        for h in heads:
            qT_h = qT_ref[h * M_DQK:(h + 1) * M_DQK, sl]
            qT_pad.append(jnp.concatenate([qT_h, zeros_q] if h % 2 == 0 else [zeros_q, qT_h], axis=0))
            s.append(_dot(k_pairs[h // 2], qT_pad[h]))
            qc.append(_dot(c_state[h].astype(BF16), qT_pad[h]))
            vT_aug.append(jnp.concatenate([vT_ref[h * M_DV:(h + 1) * M_DV, sl], ones_rows], axis=0))
        m_t, isc, sw, kw, m_new, decay, scale = [], [], [], [], [], [], []
        for h in heads:
            in_head = (col // M_DQK) == (h % 2)
            b_col = bcols[c][:, M_HEADS + h:M_HEADS + h + 1]
            i_col = sms[c][:, h:h + 1]
            b_row = brows[c][M_HEADS + h:M_HEADS + h + 1, :]
            g = b_row[:, L - 1:L]
            d = jnp.where(causal, b_row + (i_col - b_col), NEG)
            inter = b_row + m_state[h]
            m_t.append(jnp.maximum(inter, jnp.max(d, axis=0, keepdims=True)))
            isc.append(jnp.exp(inter - m_t[h]))
            sw.append((s[h] * jnp.exp(d - m_t[h])).astype(BF16))
            a_col = g - b_col + i_col
            a_max = jnp.max(a_col, axis=0, keepdims=True)
            kw.append(jnp.where(in_head, k_pairs[h // 2].astype(F32) * jnp.exp(a_col - a_max), 0.0).astype(BF16))
            m_new.append(jnp.maximum(g + m_state[h], a_max))
            decay.append(jnp.exp(g + m_state[h] - m_new[h]))
            scale.append(jnp.exp(a_max - m_new[h]))
        sv = [_dot(vT_aug[h], sw[h]) for h in heads]
        dc = [_dot(vT_aug[h], kw[h]) for h in heads]
        for h in heads:
            nd = isc[h] * qc[h] + sv[h]
            den = nd[M_DV:M_DV + 1, :]
            hh = nd[0:M_DV, :] / jnp.maximum(jnp.abs(den), jnp.exp(-m_t[h]))
            hn = hh * lax.rsqrt(jnp.mean(hh * hh, axis=0, keepdims=True) + EPS)
            o_ref[sl, h * M_DV:(h + 1) * M_DV] = hn.T * onw_ref[h:h + 1, :] * og_ref[sl, h * M_DV:(h + 1) * M_DV]
            c_state[h] = decay[h] * c_state[h] + scale[h] * dc[h]
            m_state[h] = m_new[h]
    for h in heads:
        c_ref[h] = c_state[h]
        m_ref[h:h + 1, :] = jnp.broadcast_to(m_state[h], (1, LANES))


def _mlstm(qT, k, vT, small, smallT, og, onw, batch, seq, tb):
    t = k.shape[0]
    nblk = seq // tb
    return pl.pallas_call(
        _mlstm_kernel,
        grid=(batch, nblk),
        in_specs=[
            pl.BlockSpec((M_QK, tb), lambda b, j: (0, b * nblk + j)),
            pl.BlockSpec((tb, M_QK), lambda b, j: (b * nblk + j, 0)),
            pl.BlockSpec((M_V, tb), lambda b, j: (0, b * nblk + j)),
            pl.BlockSpec((tb, LANES), lambda b, j: (b * nblk + j, 0)),
            pl.BlockSpec((32, tb), lambda b, j: (0, b * nblk + j)),
            pl.BlockSpec((tb, M_V), lambda b, j: (b * nblk + j, 0)),
            pl.BlockSpec((8, M_DV), lambda b, j: (0, 0)),
        ],
        out_specs=pl.BlockSpec((tb, M_V), lambda b, j: (b * nblk + j, 0)),
        out_shape=jax.ShapeDtypeStruct((t, M_V), F32),
        scratch_shapes=[pltpu.VMEM((M_HEADS, M_DV + 16, LANES), F32), pltpu.VMEM((8, LANES), F32)],
        compiler_params=_params(("arbitrary", "arbitrary")),
        name="mlstm",
    )(qT, k, vT, small, smallT, og, onw)


def _nsa_kernel(qT_ref, kcmp_ref, vcmpT_ref, ks_ref, e_ref, vsT_ref, kw_ref, vwT_ref, gT_ref,
                o_ref, rhs_ref, ps_ref, ocmp_ref, s0_ref, s1_ref, c0_ref, c1_ref, m_ref, acc_ref, *, nsel, n_rounds):
    qi = pl.program_id(1)
    tq = Q_TILE
    nb = kcmp_ref.shape[0]
    nselp = e_ref.shape[1]
    q0 = qi * tq
    wide = N_HG * tq
    lane_w = lax.broadcasted_iota(jnp.int32, (1, wide), 1)
    tpos_w = q0 + (lane_w % tq)
    tpos = q0 + lax.broadcasted_iota(jnp.int32, (1, tq), 1)
    zeros_q = jnp.zeros((N_DH, wide), BF16)
    ones_v = jnp.ones((16, LANES), BF16)
    grows = [slice(g * N_DH, (g + 1) * N_DH) for g in range(N_KV)]
    gcols = [slice(g * wide, (g + 1) * wide) for g in range(N_KV)]

    def values(v_ref, first, count, rows):
        return jnp.concatenate(
            [jnp.concatenate([v_ref[first + j, rows, :], ones_v], axis=0) for j in range(count)], axis=1)

    start_w = pl.multiple_of(jnp.maximum(q0 - WINDOW, 0), LANES)
    qpads, sw = [], []
    for g in range(N_KV):
        q4 = jnp.concatenate(
            [qT_ref[(g * N_HG + h) * N_DH:(g * N_HG + h + 1) * N_DH, :] for h in range(N_HG)], axis=1)
        qpads.append(jnp.concatenate([q4, zeros_q] if g == 0 else [zeros_q, q4], axis=0))
        rhs_ref[0:2 * N_DH, gcols[g]] = qpads[g]
        sw.append(_dot(kw_ref[pl.ds(start_w, WIN_SPAN), :], qpads[g]))

    ratio = SEL_BLOCK // CMP_STRIDE

    def compressed_and_select(nrows):
        nbv = nrows * ratio
        cend = lax.broadcasted_iota(jnp.int32, (nbv, 1), 0) * CMP_STRIDE + (CMP_LEN - 1)
        cmask = cend <= tpos_w
        any_visible = tpos_w >= CMP_LEN - 1
        sc = [_dot(kcmp_ref[0:nbv, :], qpads[g]) for g in range(N_KV)]
        imp = []
        for g in range(N_KV):
            s = jnp.where(cmask, sc[g], NEG)
            pc = jnp.exp2(s - jnp.max(s, axis=0, keepdims=True))
            lc = jnp.sum(pc, axis=0, keepdims=True)
            pc = pc * jnp.where(any_visible, 1.0 / lc, 0.0)
            ocmp_ref[g] = _dot(vcmpT_ref[grows[g], 0:nbv], pc.astype(BF16))
            psum = pc[:, 0:tq]
            for h in range(1, N_HG):
                psum = psum + pc[:, h * tq:(h + 1) * tq]
            parts = []
            for cchunk in range(tq // LANES):
                ps_ref[g, cchunk, 0:8, :] = jnp.zeros((8, LANES), F32)
                ps_ref[g, cchunk, 8:8 + nbv, :] = psum[:, cchunk * LANES:(cchunk + 1) * LANES]
                acc = None
                for k in range(-((CMP_LEN - 1) // CMP_STRIDE), ratio):
                    part = ps_ref[g, cchunk, pl.ds(8 + k, nrows, stride=ratio), :]
                    acc = part if acc is None else acc + part
                parts.append(acc)
            imp.append(jnp.concatenate(parts, axis=1))

        jblk = lax.broadcasted_iota(jnp.int32, (nrows, tq), 0)
        cur = tpos // SEL_BLOCK
        forced = (jblk == 0) | (jblk == cur) | (jblk == cur - 1)
        cand = (jblk >= 1) & (jblk <= cur - 2)
        jblk_f = jblk.astype(F32)
        val = [jnp.where(cand, imp[g], -jnp.inf) for g in range(N_KV)]
        for _ in range(n_rounds):
            for g in range(N_KV):
                mx = jnp.max(val[g], axis=0, keepdims=True)
                first = jnp.min(jnp.where(val[g] == mx, jblk_f, float(nrows)), axis=0, keepdims=True)
                val[g] = jnp.where(jblk_f == first, -jnp.inf, val[g])
        for g in range(N_KV):
            picked = cand & (val[g] == -jnp.inf)
            bias = jnp.where(forced | picked, 0.0, NEG).astype(BF16)
            if nselp > nrows:
                bias = jnp.concatenate([bias, jnp.zeros((nselp - nrows, tq), BF16)], axis=0)
            rhs_ref[2 * N_DH:, gcols[g]] = jnp.concatenate([bias] * N_HG, axis=1)

    n_var = 4 if nsel % 32 == 0 else 1
    if n_var == 1:
        compressed_and_select(nsel)
    else:
        tiles_per_var = (nsel * SEL_BLOCK // tq) // n_var
        for v in range(n_var):
            pl.when(qi // tiles_per_var == v)(functools.partial(compressed_and_select, (v + 1) * nsel // n_var))

    dist = (tpos_w - start_w) - lax.broadcasted_iota(jnp.int32, (WIN_SPAN, 1), 0)
    wmask = lax.bitcast_convert_type(dist, jnp.uint32) < WINDOW

    def scores(kt, g):
        start = pl.multiple_of(kt * KEY_TILE, KEY_TILE)
        lhs = jnp.concatenate([ks_ref[pl.ds(start, KEY_TILE), :], e_ref[pl.ds(start, KEY_TILE), :]], axis=1)
        return _dot(lhs, rhs_ref[:, g * wide:(g + 1) * wide])

    def produce(kt, s_ref, c_ref):
        for g in range(N_KV):
            s = scores(kt, g)
            s_ref[g] = s
            c_ref[g] = jnp.max(s, axis=0, keepdims=True)

    def consume(kt, s_ref, c_ref, causal_tile=False):
        for g in range(N_KV):
            s = s_ref[g]
            if causal_tile:
                kpos = kt * KEY_TILE + lax.broadcasted_iota(jnp.int32, (KEY_TILE, 1), 0)
                s = jnp.where(kpos <= tpos_w, s, NEG)
                smax = jnp.max(s, axis=0, keepdims=True)
            else:
                smax = c_ref[g]
            m = m_ref[g]
            m_new = jnp.maximum(m, smax)
            p = jnp.exp2(s - m_new).astype(BF16)
            vt = values(vsT_ref, kt * (KEY_TILE // LANES), KEY_TILE // LANES, slice(g * N_DH, (g + 1) * N_DH))
            acc_ref[g] = jnp.exp2(m - m_new) * acc_ref[g] + _dot(vt, p)
            m_ref[g] = m_new

    n_full = qi // (KEY_TILE // tq)
    odd = n_full % 2
    m_ref[...] = jnp.full(m_ref.shape, NEG, F32)
    acc_ref[...] = jnp.zeros(acc_ref.shape, F32)
    produce(0, s0_ref, c0_ref)

    o_win = []
    for g in range(N_KV):
        s = jnp.where(wmask, sw[g], NEG)
        pw = jnp.exp2(s - jnp.max(s, axis=0, keepdims=True))
        ow = _dot(values(vwT_ref, start_w // LANES, WIN_SPAN // LANES, grows[g]), pw.astype(BF16))
        o_win.append(ow[0:N_DH, :] * (1.0 / ow[N_DH:N_DH + 1, :]))

    def pair(kt):
        produce(kt + 1, s1_ref, c1_ref)
        consume(kt, s0_ref, c0_ref)
        produce(kt + 2, s0_ref, c0_ref)
        consume(kt + 1, s1_ref, c1_ref)

    def quad(j, _):
        pair(4 * j)
        pair(4 * j + 2)
        return 0

    n_quad = n_full // 4
    lax.fori_loop(0, n_quad, quad, 0)

    @pl.when(n_full - 4 * n_quad >= 2)
    def _():
        pair(4 * n_quad)

    @pl.when(odd == 0)
    def _():
        consume(n_full, s0_ref, c0_ref, causal_tile=True)

    @pl.when(odd == 1)
    def _():
        produce(n_full, s1_ref, c1_ref)
        consume(n_full - 1, s0_ref, c0_ref)
        consume(n_full, s1_ref, c1_ref, causal_tile=True)

    outs = []
    for g in range(N_KV):
        o_slc = acc_ref[g, 0:N_DH, :] * (1.0 / acc_ref[g, N_DH:N_DH + 1, :])
        for h in range(N_HG):
            cs = slice(h * tq, (h + 1) * tq)
            r = 2 * M_HEADS + (g * N_HG + h) * 3
            outs.append(gT_ref[r:r + 1, :] * ocmp_ref[g, :, cs] + gT_ref[r + 1:r + 2, :] * o_slc[:, cs]
                        + gT_ref[r + 2:r + 3, :] * o_win[g][:, cs])

    o_ref[...] = jnp.concatenate(outs, axis=0).T


def _nsa(qT, kcmp, vcmpT, ks, emap, vsT, kw, vwT, smallT, batch, seq):
    t = qT.shape[1]
    nq = seq // Q_TILE
    nb = kcmp.shape[1]
    nsel = seq // SEL_BLOCK
    nselp = emap.shape[1]
    wide = N_HG * Q_TILE
    n_rounds = max(min(SEL_TOPN, nsel) - 3, 0)
    kern = functools.partial(_nsa_kernel, nsel=nsel, n_rounds=n_rounds)
    return pl.pallas_call(
        kern,
        grid=(batch, nq),
        in_specs=[
            pl.BlockSpec((N_Q, Q_TILE), lambda b, i: (0, b * nq + i)),
            pl.BlockSpec((None, nb, N_KVW), lambda b, i: (b, 0, 0)),
            pl.BlockSpec((None, N_KVW, nb), lambda b, i: (b, 0, 0)),
            pl.BlockSpec((seq, N_KVW), lambda b, i: (b, 0)),
            pl.BlockSpec((seq, nselp), lambda b, i: (0, 0)),
            pl.BlockSpec((seq // LANES, N_KVW, LANES), lambda b, i: (b, 0, 0)),
            pl.BlockSpec((seq, N_KVW), lambda b, i: (b, 0)),
            pl.BlockSpec((seq // LANES, N_KVW, LANES), lambda b, i: (b, 0, 0)),
            pl.BlockSpec((32, Q_TILE), lambda b, i: (0, b * nq + i)),
        ],
        out_specs=pl.BlockSpec((Q_TILE, N_Q), lambda b, i: (b * nq + i, 0)),
        out_shape=jax.ShapeDtypeStruct((t, N_Q), F32),
        scratch_shapes=[
            pltpu.VMEM((2 * N_DH + nselp, N_KV * wide), BF16),
            pltpu.VMEM((N_KV, Q_TILE // LANES, nb + 8, LANES), F32),
            pltpu.VMEM((N_KV, N_DH, wide), F32),
            pltpu.VMEM((N_KV, KEY_TILE, wide), F32),
            pltpu.VMEM((N_KV, KEY_TILE, wide), F32),
            pltpu.VMEM((N_KV, 1, wide), F32),
            pltpu.VMEM((N_KV, 1, wide), F32),
            pltpu.VMEM((N_KV, 1, wide), F32),
            pltpu.VMEM((N_KV, N_DH + 16, wide), F32),
        ],
        compiler_params=_params(("arbitrary", "arbitrary")),
        name="nsa",
    )(qT, kcmp, vcmpT, ks, emap, vsT, kw, vwT, smallT)


def _merge_kernel(x_ref, n1w_ref, wg_ref, gb_ref, hm_ref, on_ref, wm_ref, wn_ref, wo_ref, o_ref):
    x = x_ref[...]
    hn = _rmsnorm_rows(x, n1w_ref[...]).astype(BF16)
    gm = _sigmoid(_dot(hn, wg_ref[:, 0:D_MODEL]) + gb_ref[0:1, :])
    gn = _sigmoid(_dot(hn, wg_ref[:, D_MODEL:]) + gb_ref[1:2, :])
    y = gm * _dot(hm_ref[...].astype(BF16), wm_ref[...]) + gn * _dot(on_ref[...].astype(BF16), wn_ref[...])
    o_ref[...] = x + _dot(y.astype(BF16), wo_ref[...])


def _merge(x2, n1w, wg, gb, hm, on, wm, wn, wo, tm):
    t = x2.shape[0]
    const = lambda i: (0, 0)
    return pl.pallas_call(
        _merge_kernel,
        grid=(t // tm,),
        in_specs=[
            pl.BlockSpec((tm, D_MODEL), lambda i: (i, 0)),
            pl.BlockSpec((1, D_MODEL), const),
            pl.BlockSpec((D_MODEL, 2 * D_MODEL), const),
            pl.BlockSpec((2, D_MODEL), const),
            pl.BlockSpec((tm, M_V), lambda i: (i, 0)),
            pl.BlockSpec((tm, N_Q), lambda i: (i, 0)),
            pl.BlockSpec((M_V, D_MODEL), const),
            pl.BlockSpec((N_Q, D_MODEL), const),
            pl.BlockSpec((D_MODEL, D_MODEL), const),
        ],
        out_specs=pl.BlockSpec((tm, D_MODEL), lambda i: (i, 0)),
        out_shape=jax.ShapeDtypeStruct((t, D_MODEL), F32),
        compiler_params=_params(("arbitrary",)),
        name="merge",
    )(x2, n1w, wg, gb, hm, on, wm, wn, wo)


def _ffn_kernel(x_ref, n2w_ref, wup_ref, cw_ref, cb_ref, wdn_ref, o_ref, buf_ref, *, tiles_per_seq):
    i = pl.program_id(0)
    tm = x_ref.shape[0]
    x = x_ref[...]
    hn = _rmsnorm_rows(x, n2w_ref[...]).astype(BF16)

    @pl.when(i % tiles_per_seq == 0)
    def _():
        buf_ref[0:8, :] = jnp.zeros((8, D_FF), F32)

    a = _dot(hn, wup_ref[:, 0:D_FF])
    buf_ref[8:8 + tm, :] = a
    acc = cb_ref[...] + cw_ref[FFN_CONV - 1:FFN_CONV, :] * a
    for k in range(FFN_CONV - 1):
        acc = acc + cw_ref[k:k + 1, :] * buf_ref[8 - (FFN_CONV - 1) + k:8 - (FFN_CONV - 1) + k + tm, :]
    buf_ref[0:8, :] = buf_ref[tm:tm + 8, :]
    v = _dot(hn, wup_ref[:, D_FF:])
    o_ref[...] = x + _dot((_gelu(acc) * v).astype(BF16), wdn_ref[...])


def _ffn(x2, n2w, wup, cw, cb, wdn, seq, tm):
    t = x2.shape[0]
    const = lambda i: (0, 0)
    kern = functools.partial(_ffn_kernel, tiles_per_seq=seq // tm)
    return pl.pallas_call(
        kern,
        grid=(t // tm,),
        in_specs=[
            pl.BlockSpec((tm, D_MODEL), lambda i: (i, 0)),
            pl.BlockSpec((1, D_MODEL), const),
            pl.BlockSpec((D_MODEL, 2 * D_FF), const, pipeline_mode=pl.Buffered(1)),
            pl.BlockSpec((FFN_CONV, D_FF), const),
            pl.BlockSpec((1, D_FF), const),
            pl.BlockSpec((D_FF, D_MODEL), const, pipeline_mode=pl.Buffered(1)),
        ],
        out_specs=pl.BlockSpec((tm, D_MODEL), lambda i: (i, 0)),
        out_shape=jax.ShapeDtypeStruct((t, D_MODEL), F32),
        scratch_shapes=[pltpu.VMEM((tm + 8, D_FF), F32)],
        compiler_params=_params(("arbitrary",)),
        name="ffn",
    )(x2, n2w, wup, cw, cb, wdn)


def _cols(w, *names):
    return jnp.concatenate([w[:, _OFF[n][0]:_OFF[n][1]] for n in names], axis=1)


def _layer(x, n1w, w_in, m_conv_w, m_conv_b, m_igate_b, m_fgate_b, m_out_norm_w,
           q_norm_w, kcmp_norm_w, kslc_norm_w, kwin_norm_w,
           cmp_k_pe, cmp_k_w1, cmp_k_w2, cmp_v_pe, cmp_v_w1, cmp_v_w2,
           w_up_m, w_up_n, merge_gate_b, w_out, norm2_w, ffn_w_up, ffn_conv_w, ffn_conv_b, ffn_w_down):
    batch, seq, _ = x.shape
    t = batch * seq
    x2 = x.reshape(t, D_MODEL)
    n1w2 = n1w.reshape(1, D_MODEL)
    tm = 256
    tm_proj = 512

    w_m = _cols(w_in, "mq", "mk", "mo").astype(BF16)
    w_mvT = _cols(w_in, "mv", "mi", "mf", "ng").T.astype(BF16)
    w_n = _cols(w_in, "nq", "kc", "vc", "ks", "vs", "kw", "vw").astype(BF16)
    w_g = _cols(w_in, "gm", "gn").astype(BF16)
    sbias = jnp.concatenate([m_igate_b, m_fgate_b, jnp.zeros((3 * N_HEADS,), F32)]).reshape(32, 1)

    half = N_DH // 2
    pos = jnp.arange(seq, dtype=F32)
    inv = ROPE_THETA ** (-jnp.arange(0, N_DH, 2, dtype=F32) / N_DH)
    ang = pos[:, None] * inv[None, :]
    cos, sin = jnp.cos(ang), jnp.sin(ang)
    cosn = jnp.tile(cos, (1, N_KVW // half))
    sinn = jnp.tile(jnp.concatenate([-sin, sin], axis=1), (1, N_KV))
    cosT, sinT = cos.T, sin.T
    knw = jnp.zeros((8, N_KVW), F32).at[0:3].set(
        jnp.stack([jnp.tile(w, N_KV) for w in (kcmp_norm_w, kslc_norm_w, kwin_norm_w)]))

    qT_m, k_m, vT_m, og, small, smallT = _proj_m(
        x2, n1w2, w_m, w_mvT, m_conv_w, m_conv_b.reshape(1, -1), sbias, seq, tm_proj)
    qT, kvb, ks, kw, vsT, vwT = _proj_n(
        x2, n1w2, w_n, q_norm_w.reshape(N_DH, 1), knw, cosn, sinn, cosT, sinT, seq, tm_proj)

    nb = seq // CMP_STRIDE
    xkv = kvb.reshape(2 * N_KV, batch, nb, CMP_STRIDE * N_DH)
    pe = jnp.stack([cmp_k_pe.reshape(1, -1), cmp_v_pe.reshape(1, -1)])
    w1 = jnp.stack([cmp_k_w1, cmp_v_w1]).astype(BF16)
    w2 = jnp.stack([cmp_k_w2, cmp_v_w2]).astype(BF16)
    kcmp, vcmpT = _compress(xkv, pe, w1, w2)

    onw = jnp.zeros((8, M_DV), F32).at[0:M_HEADS].set(m_out_norm_w)
    hm = _mlstm(qT_m, k_m, vT_m, small, smallT, og, onw, batch, seq, 4 * MLSTM_CHUNK)
    nselp = -(-(seq // SEL_BLOCK) // LANES) * LANES
    emap = (np.arange(seq)[:, None] // SEL_BLOCK == np.arange(nselp)[None, :]).astype(np.float32)
    on = _nsa(qT, kcmp, vcmpT, ks, jnp.asarray(emap, dtype=BF16), vsT, kw, vwT, smallT, batch, seq)

    x1 = _merge(x2, n1w2, w_g, merge_gate_b, hm, on, w_up_m.astype(BF16), w_up_n.astype(BF16),
                w_out.astype(BF16), tm_proj)
    out = _ffn(x1, norm2_w.reshape(1, D_MODEL), ffn_w_up.astype(BF16), ffn_conv_w, ffn_conv_b.reshape(1, -1),
               ffn_w_down.astype(BF16), seq, tm)
    return out.reshape(batch, seq, D_MODEL)


def kernel(x, norm1_w, w_in, m_conv_w, m_conv_b, m_igate_b, m_fgate_b, m_out_norm_w, q_norm_w, kcmp_norm_w,
           kslc_norm_w, kwin_norm_w, cmp_k_pe, cmp_k_w1, cmp_k_w2, cmp_v_pe, cmp_v_w1, cmp_v_w2, w_up_m, w_up_n,
           merge_gate_b, w_out, norm2_w, ffn_w_up, ffn_conv_w, ffn_conv_b, ffn_w_down):
    params = (norm1_w, w_in, m_conv_w, m_conv_b, m_igate_b, m_fgate_b, m_out_norm_w, q_norm_w, kcmp_norm_w,
              kslc_norm_w, kwin_norm_w, cmp_k_pe, cmp_k_w1, cmp_k_w2, cmp_v_pe, cmp_v_w1, cmp_v_w2, w_up_m, w_up_n,
              merge_gate_b, w_out, norm2_w, ffn_w_up, ffn_conv_w, ffn_conv_b, ffn_w_down)
    for layer in range(norm1_w.shape[0]):
        x = _layer(x, *[p[layer] for p in params])
    return x
```

```python
import functools
import math

import jax
import jax.numpy as jnp
import numpy as np
from jax import lax
from jax.experimental import pallas as pl
from jax.experimental.pallas import tpu as pltpu

D_MODEL = 1024
EPS = 1e-6
ROPE_THETA = 10000.0
NEG = -1e30
M_HEADS = 4
M_DQK = 64
M_DV = 128
M_CONV = 4
M_QK = M_HEADS * M_DQK
M_V = M_HEADS * M_DV
N_HEADS = 8
N_KV = 2
N_HG = N_HEADS // N_KV
N_DH = 64
N_Q = N_HEADS * N_DH
N_KVW = N_KV * N_DH
CMP_LEN = 32
CMP_STRIDE = 16
CMP_HIDDEN = 256
SEL_BLOCK = 64
SEL_TOPN = 16
WINDOW = 512
D_FF = 2816
FFN_CONV = 3

_OFF = {}
_o = 0
for _name, _size in (("mq", M_QK), ("mk", M_QK), ("mv", M_V), ("mo", M_V), ("mi", M_HEADS), ("mf", M_HEADS),
                     ("nq", N_Q), ("kc", N_KVW), ("vc", N_KVW), ("ks", N_KVW), ("vs", N_KVW), ("kw", N_KVW),
                     ("vw", N_KVW), ("ng", 3 * N_HEADS), ("gm", D_MODEL), ("gn", D_MODEL)):
    _OFF[_name] = (_o, _o + _size)
    _o += _size

LANES = 128
MLSTM_CHUNK = 128
SUB_ROWS = 256
Q_TILE = 128
KEY_TILE = 256
WIN_SPAN = WINDOW + Q_TILE
VMEM_LIMIT = 56 * 1024 * 1024

LOG2E = math.log2(math.e)

F32 = jnp.float32
BF16 = jnp.bfloat16
HIGHEST = lax.Precision.HIGHEST


def _dot(a, b):
    return jnp.dot(a, b, preferred_element_type=F32)


def _dot_nt(a, b):
    return lax.dot_general(a, b, (((1,), (1,)), ((), ())), preferred_element_type=F32)


def _rmsnorm_rows(x, w):
    return x * lax.rsqrt(jnp.mean(x * x, axis=-1, keepdims=True) + EPS) * w


def _sigmoid(x):
    return 1.0 / (1.0 + jnp.exp(-x))


def _gelu(x):
    return 0.5 * x * (1.0 + lax.erf(x * (1.0 / math.sqrt(2.0))))


def _params(sem):
    return pltpu.CompilerParams(dimension_semantics=sem, vmem_limit_bytes=VMEM_LIMIT)


def _proj_m_kernel(x_ref, n1w_ref, w_ref, wvT_ref, cw_ref, cb_ref, sb_ref,
                   qT_ref, k_ref, vT_ref, og_ref, small_ref, smallT_ref, buf_ref, *, tiles_per_seq):
    i = pl.program_id(0)
    tm = x_ref.shape[0]
    sub = SUB_ROWS

    @pl.when(i % tiles_per_seq == 0)
    def _():
        buf_ref[0:8, :] = jnp.zeros((8, 2 * M_QK), F32)

    hns = [_rmsnorm_rows(x_ref[r * sub:(r + 1) * sub, :], n1w_ref[...]).astype(BF16) for r in range(tm // sub)]
    for r in range(tm // sub):
        rows = slice(r * sub, (r + 1) * sub)
        hn = hns[r]
        qk = _dot(hn, w_ref[:, 0:2 * M_QK])
        buf_ref[8:8 + sub, :] = qk
        acc = cb_ref[...] + cw_ref[M_CONV - 1:M_CONV, :] * qk
        for k in range(M_CONV - 1):
            acc = acc + cw_ref[k:k + 1, :] * buf_ref[8 - (M_CONV - 1) + k:8 - (M_CONV - 1) + k + sub, :]
        buf_ref[0:8, :] = buf_ref[sub:sub + 8, :]
        act = acc * _sigmoid(acc)
        qT_ref[:, rows] = (act[:, 0:M_QK] * (M_DQK ** -0.5)).T.astype(BF16)
        k_ref[rows, :] = act[:, M_QK:2 * M_QK].astype(BF16)
        vs = _dot_nt(wvT_ref[...], hn)
        vT_ref[:, rows] = vs[0:M_V, :].astype(BF16)
        og_ref[rows, :] = _sigmoid(_dot(hn, w_ref[:, 2 * M_QK:2 * M_QK + M_V]))
        smT = vs[M_V:, :] + sb_ref[...]
        rowi = lax.broadcasted_iota(jnp.int32, smT.shape, 0)
        logsig = jnp.minimum(smT, 0.0) - jnp.log1p(jnp.exp(-jnp.abs(smT)))
        smT = jnp.where(rowi < M_HEADS, smT, jnp.where(rowi < 2 * M_HEADS, logsig, _sigmoid(smT)))
        smallT_ref[:, rows] = smT
        small_ref[rows, :] = jnp.concatenate([smT, jnp.zeros((LANES - 32, sub), F32)], axis=0).T


def _proj_m(x2, n1w, w, wvT, cw, cb, sb, seq, tm):
    t = x2.shape[0]
    ncol = w.shape[1]
    kern = functools.partial(_proj_m_kernel, tiles_per_seq=seq // tm)
    return pl.pallas_call(
        kern,
        grid=(t // tm,),
        in_specs=[
            pl.BlockSpec((tm, D_MODEL), lambda i: (i, 0)),
            pl.BlockSpec((1, D_MODEL), lambda i: (0, 0)),
            pl.BlockSpec((D_MODEL, ncol), lambda i: (0, 0)),
            pl.BlockSpec((M_V + 32, D_MODEL), lambda i: (0, 0)),
            pl.BlockSpec((M_CONV, 2 * M_QK), lambda i: (0, 0)),
            pl.BlockSpec((1, 2 * M_QK), lambda i: (0, 0)),
            pl.BlockSpec((32, 1), lambda i: (0, 0)),
        ],
        out_specs=[
            pl.BlockSpec((M_QK, tm), lambda i: (0, i)),
            pl.BlockSpec((tm, M_QK), lambda i: (i, 0)),
            pl.BlockSpec((M_V, tm), lambda i: (0, i)),
            pl.BlockSpec((tm, M_V), lambda i: (i, 0)),
            pl.BlockSpec((tm, LANES), lambda i: (i, 0)),
            pl.BlockSpec((32, tm), lambda i: (0, i)),
        ],
        out_shape=[
            jax.ShapeDtypeStruct((M_QK, t), BF16),
            jax.ShapeDtypeStruct((t, M_QK), BF16),
            jax.ShapeDtypeStruct((M_V, t), BF16),
            jax.ShapeDtypeStruct((t, M_V), F32),
            jax.ShapeDtypeStruct((t, LANES), F32),
            jax.ShapeDtypeStruct((32, t), F32),
        ],
        scratch_shapes=[pltpu.VMEM((SUB_ROWS + 8, 2 * M_QK), F32)],
        compiler_params=_params(("arbitrary",)),
        name="proj_m",
    )(x2, n1w, w, wvT, cw, cb, sb)


def _proj_n_kernel(x_ref, n1w_ref, w_ref, qnw_ref, knw_ref, cosn_ref, sinn_ref, cosT_ref, sinT_ref,
                   qT_ref, kvb_ref, ks_ref, kw_ref, vsT_ref, vwT_ref, regroup_ref):
    tm = x_ref.shape[0]
    hn = _rmsnorm_rows(x_ref[...], n1w_ref[...]).astype(BF16)
    qT = _dot(hn, w_ref[:, 0:N_Q]).T
    cosT = cosT_ref[...]
    sinT = sinT_ref[...]
    qnw = qnw_ref[...]
    half = N_DH // 2
    for h in range(N_HEADS):
        xh = qT[h * N_DH:(h + 1) * N_DH, :]
        xn = xh * lax.rsqrt(jnp.mean(xh * xh, axis=0, keepdims=True) + EPS) * qnw
        x1 = xn[0:half, :]
        x2 = xn[half:, :]
        o = jnp.concatenate([x1 * cosT - x2 * sinT, x2 * cosT + x1 * sinT], axis=0) * (LOG2E * N_DH ** -0.5)
        qT_ref[h * N_DH:(h + 1) * N_DH, :] = o.astype(BF16)

    cosn = cosn_ref[...]
    sinn = sinn_ref[...]
    li = lax.broadcasted_iota(jnp.int32, (N_KVW, N_KVW), 0)
    lj = lax.broadcasted_iota(jnp.int32, (N_KVW, N_KVW), 1)
    head_sum = (li // N_DH == lj // N_DH).astype(BF16)
    swap_half = (lj == li + jnp.where(li % N_DH < half, half, -half)).astype(BF16)
    head_sum = jnp.concatenate([head_sum, head_sum], axis=0)
    swap_half = jnp.concatenate([swap_half, swap_half], axis=0)

    def lane_map(v, m01x2):
        hi = v.astype(BF16)
        lo = (v - hi.astype(F32)).astype(BF16)
        return _dot(jnp.concatenate([hi, lo], axis=1), m01x2)

    c0 = N_Q
    k_raw = [_dot(hn, w_ref[:, c0 + 2 * i * N_KVW:c0 + (2 * i + 1) * N_KVW]) for i in range(3)]
    vc, vs, vw = [_dot(hn, w_ref[:, c0 + (2 * i + 1) * N_KVW:c0 + (2 * i + 2) * N_KVW]) for i in range(3)]
    ms = [lane_map(k * k, head_sum) * (1.0 / N_DH) for k in k_raw]
    kn = [k_raw[i] * lax.rsqrt(ms[i] + EPS) * knw_ref[i:i + 1, :] for i in range(3)]
    kc, ks, kw = [kn[i] * cosn + lane_map(kn[i], swap_half) * sinn for i in range(3)]
    lane_kv = lax.broadcasted_iota(jnp.int32, (tm // CMP_STRIDE, N_KVW), 1)
    for a, arr in enumerate((kc, vc)):
        regroup_ref[...] = arr
        for l in range(0, CMP_STRIDE, 2):
            even = regroup_ref[pl.ds(l, tm // CMP_STRIDE, stride=CMP_STRIDE), :]
            odd = regroup_ref[pl.ds(l + 1, tm // CMP_STRIDE, stride=CMP_STRIDE), :]
            cols = slice((l // 2) * N_KVW, (l // 2 + 1) * N_KVW)
            kvb_ref[a * N_KV, :, cols] = jnp.where(lane_kv < N_DH, even, pltpu.roll(odd, N_DH, 1))
            kvb_ref[a * N_KV + 1, :, cols] = jnp.where(lane_kv < N_DH, pltpu.roll(even, N_DH, 1), odd)
    ks_ref[...] = ks.astype(BF16)
    kw_ref[...] = kw.astype(BF16)
    vsT = vs.T.astype(BF16)
    vwT = vw.T.astype(BF16)
    for j in range(tm // LANES):
        vsT_ref[j] = vsT[:, j * LANES:(j + 1) * LANES]
        vwT_ref[j] = vwT[:, j * LANES:(j + 1) * LANES]


def _proj_n(x2, n1w, w, qnw, knw, cosn, sinn, cosT, sinT, seq, tm):
    t = x2.shape[0]
    ncol = w.shape[1]
    tps = seq // tm
    half = N_DH // 2
    return pl.pallas_call(
        _proj_n_kernel,
        grid=(t // tm,),
        in_specs=[
            pl.BlockSpec((tm, D_MODEL), lambda i: (i, 0)),
            pl.BlockSpec((1, D_MODEL), lambda i: (0, 0)),
            pl.BlockSpec((D_MODEL, ncol), lambda i: (0, 0)),
            pl.BlockSpec((N_DH, 1), lambda i: (0, 0)),
            pl.BlockSpec((8, N_KVW), lambda i: (0, 0)),
            pl.BlockSpec((tm, N_KVW), lambda i: (i % tps, 0)),
            pl.BlockSpec((tm, N_KVW), lambda i: (i % tps, 0)),
            pl.BlockSpec((half, tm), lambda i: (0, i % tps)),
            pl.BlockSpec((half, tm), lambda i: (0, i % tps)),
        ],
        out_specs=[
            pl.BlockSpec((N_Q, tm), lambda i: (0, i)),
            pl.BlockSpec((2 * N_KV, tm // CMP_STRIDE, CMP_STRIDE * N_DH), lambda i: (0, i, 0)),
            pl.BlockSpec((tm, N_KVW), lambda i: (i, 0)),
            pl.BlockSpec((tm, N_KVW), lambda i: (i, 0)),
            pl.BlockSpec((tm // LANES, N_KVW, LANES), lambda i: (i, 0, 0)),
            pl.BlockSpec((tm // LANES, N_KVW, LANES), lambda i: (i, 0, 0)),
        ],
        out_shape=[
            jax.ShapeDtypeStruct((N_Q, t), BF16),
            jax.ShapeDtypeStruct((2 * N_KV, t // CMP_STRIDE, CMP_STRIDE * N_DH), F32),
            jax.ShapeDtypeStruct((t, N_KVW), BF16),
            jax.ShapeDtypeStruct((t, N_KVW), BF16),
            jax.ShapeDtypeStruct((t // LANES, N_KVW, LANES), BF16),
            jax.ShapeDtypeStruct((t // LANES, N_KVW, LANES), BF16),
        ],
        scratch_shapes=[pltpu.VMEM((tm, N_KVW), F32)],
        compiler_params=_params(("arbitrary",)),
        name="proj_n",
    )(x2, n1w, w, qnw, knw, cosn, sinn, cosT, sinT)


def _compress_kernel(x_ref, pe_ref, w1_ref, w2_ref, kcmp_ref, vcmpT_ref):
    nb = x_ref.shape[2]
    half = (CMP_LEN // 2) * N_DH

    def mlp(a, kv):
        pe = pe_ref[kv]
        x = x_ref[a, 0]
        first = _dot((x + pe[:, 0:half]).astype(BF16), w1_ref[kv, 0:half, :])
        second = _dot((x + pe[:, half:]).astype(BF16), w1_ref[kv, half:, :])
        hid = first + pltpu.roll(second, nb - 1, 0)
        return _dot(_gelu(hid).astype(BF16), w2_ref[kv])

    kcmp_ref[...] = jnp.concatenate([mlp(g, 0) for g in range(N_KV)], axis=1).astype(BF16)
    vcmpT_ref[...] = jnp.concatenate([mlp(N_KV + g, 1) for g in range(N_KV)], axis=1).T.astype(BF16)


def _compress(xkv, pe, w1, w2):
    na, batch, nb, width = xkv.shape
    const3 = lambda b: (0, 0, 0)
    return pl.pallas_call(
        _compress_kernel,
        grid=(batch,),
        in_specs=[
            pl.BlockSpec((na, 1, nb, width), lambda b: (0, b, 0, 0)),
            pl.BlockSpec((2, 1, CMP_LEN * N_DH), const3),
            pl.BlockSpec((2, CMP_LEN * N_DH, CMP_HIDDEN), const3),
            pl.BlockSpec((2, CMP_HIDDEN, N_DH), const3),
        ],
        out_specs=[
            pl.BlockSpec((None, nb, N_KVW), lambda b: (b, 0, 0)),
            pl.BlockSpec((None, N_KVW, nb), lambda b: (b, 0, 0)),
        ],
        out_shape=[
            jax.ShapeDtypeStruct((batch, nb, N_KVW), BF16),
            jax.ShapeDtypeStruct((batch, N_KVW, nb), BF16),
        ],
        compiler_params=_params(("arbitrary",)),
        name="compress",
    )(xkv, pe, w1, w2)


def _mlstm_kernel(qT_ref, k_ref, vT_ref, small_ref, smallT_ref, og_ref, onw_ref, o_ref, c_ref, m_ref):
    L = MLSTM_CHUNK
    tb = k_ref.shape[0]

    @pl.when(pl.program_id(1) == 0)
    def _():
        c_ref[...] = jnp.zeros(c_ref.shape, F32)
        m_ref[...] = jnp.zeros(m_ref.shape, F32)

    row = lax.broadcasted_iota(jnp.int32, (L, L), 0)
    col = lax.broadcasted_iota(jnp.int32, (L, L), 1)
    causal = row <= col
    tril = (col <= row).astype(F32)
    triu = causal.astype(F32)
    ones_rows = jnp.ones((16, L), BF16)
    zeros_q = jnp.zeros((M_DQK, L), BF16)

    heads = range(M_HEADS)
    chunks = range(tb // L)
    sls = [slice(c * L, (c + 1) * L) for c in chunks]
    sms = [small_ref[sl, :] for sl in sls]
    bcols = [jnp.dot(tril, sm, preferred_element_type=F32, precision=HIGHEST) for sm in sms]
    brows = [jnp.dot(smallT_ref[0:8, sl], triu, preferred_element_type=F32, precision=HIGHEST) for sl in sls]
    c_state = [c_ref[h] for h in heads]
    m_state = [m_ref[h:h + 1, 0:1] for h in heads]
    for c in chunks:
        sl = sls[c]
        k_pairs = [k_ref[sl, p * LANES:(p + 1) * LANES] for p in range(M_HEADS // 2)]
        qT_pad, s, qc, vT_aug = [], [], [], []
        for h in heads:
            qT_h = qT_ref[h * M_DQK:(h + 1) * M_DQK, sl]
            qT_pad.append(jnp.concatenate([qT_h, zeros_q] if h % 2 == 0 else [zeros_q, qT_h], axis=0))
            s.append(_dot(k_pairs[h // 2], qT_pad[h]))
            qc.append(_dot(c_state[h].astype(BF16), qT_pad[h]))
            vT_aug.append(jnp.concatenate([vT_ref[h * M_DV:(h + 1) * M_DV, sl], ones_rows], axis=0))
        m_t, isc, sw, kw, m_new, decay, scale = [], [], [], [], [], [], []
        for h in heads:
            in_head = (col // M_DQK) == (h % 2)
            b_col = bcols[c][:, M_HEADS + h:M_HEADS + h + 1]
            i_col = sms[c][:, h:h + 1]
            b_row = brows[c][M_HEADS + h:M_HEADS + h + 1, :]
            g = b_row[:, L - 1:L]
            d = jnp.where(causal, b_row + (i_col - b_col), NEG)
            inter = b_row + m_state[h]
            m_t.append(jnp.maximum(inter, jnp.max(d, axis=0, keepdims=True)))
            isc.append(jnp.exp(inter - m_t[h]))
            sw.append((s[h] * jnp.exp(d - m_t[h])).astype(BF16))
            a_col = g - b_col + i_col
            a_max = jnp.max(a_col, axis=0, keepdims=True)
            kw.append(jnp.where(in_head, k_pairs[h // 2].astype(F32) * jnp.exp(a_col - a_max), 0.0).astype(BF16))
            m_new.append(jnp.maximum(g + m_state[h], a_max))
            decay.append(jnp.exp(g + m_state[h] - m_new[h]))
            scale.append(jnp.exp(a_max - m_new[h]))
        sv = [_dot(vT_aug[h], sw[h]) for h in heads]
        dc = [_dot(vT_aug[h], kw[h]) for h in heads]
        for h in heads:
            nd = isc[h] * qc[h] + sv[h]
            den = nd[M_DV:M_DV + 1, :]
            hh = nd[0:M_DV, :] / jnp.maximum(jnp.abs(den), jnp.exp(-m_t[h]))
            hn = hh * lax.rsqrt(jnp.mean(hh * hh, axis=0, keepdims=True) + EPS)
            o_ref[sl, h * M_DV:(h + 1) * M_DV] = hn.T * onw_ref[h:h + 1, :] * og_ref[sl, h * M_DV:(h + 1) * M_DV]
            c_state[h] = decay[h] * c_state[h] + scale[h] * dc[h]
            m_state[h] = m_new[h]
    for h in heads:
        c_ref[h] = c_state[h]
        m_ref[h:h + 1, :] = jnp.broadcast_to(m_state[h], (1, LANES))


def _mlstm(qT, k, vT, small, smallT, og, onw, batch, seq, tb):
    t = k.shape[0]
    nblk = seq // tb
    return pl.pallas_call(
        _mlstm_kernel,
        grid=(batch, nblk),
        in_specs=[
            pl.BlockSpec((M_QK, tb), lambda b, j: (0, b * nblk + j)),
            pl.BlockSpec((tb, M_QK), lambda b, j: (b * nblk + j, 0)),
            pl.BlockSpec((M_V, tb), lambda b, j: (0, b * nblk + j)),
            pl.BlockSpec((tb, LANES), lambda b, j: (b * nblk + j, 0)),
            pl.BlockSpec((32, tb), lambda b, j: (0, b * nblk + j)),
            pl.BlockSpec((tb, M_V), lambda b, j: (b * nblk + j, 0)),
            pl.BlockSpec((8, M_DV), lambda b, j: (0, 0)),
        ],
        out_specs=pl.BlockSpec((tb, M_V), lambda b, j: (b * nblk + j, 0)),
        out_shape=jax.ShapeDtypeStruct((t, M_V), F32),
        scratch_shapes=[pltpu.VMEM((M_HEADS, M_DV + 16, LANES), F32), pltpu.VMEM((8, LANES), F32)],
        compiler_params=_params(("arbitrary", "arbitrary")),
        name="mlstm",
    )(qT, k, vT, small, smallT, og, onw)


def _nsa_kernel(qT_ref, kcmp_ref, vcmpT_ref, ks_ref, e_ref, vsT_ref, kw_ref, vwT_ref, gT_ref,
                o_ref, rhs_ref, ps_ref, ocmp_ref, owin_ref, s0_ref, s1_ref, c0_ref, c1_ref, m_ref, acc_ref,
                *, nsel, n_rounds):
    qi = pl.program_id(1)
    tq = Q_TILE
    nb = kcmp_ref.shape[0]
    nselp = e_ref.shape[1]
    q0 = qi * tq
    wide = N_HG * tq
    lane_w = lax.broadcasted_iota(jnp.int32, (1, wide), 1)
    tpos_w = q0 + (lane_w % tq)
    tpos = q0 + lax.broadcasted_iota(jnp.int32, (1, tq), 1)
    zeros_q = jnp.zeros((N_DH, wide), BF16)
    ones_v = jnp.ones((16, LANES), BF16)
    grows = [slice(g * N_DH, (g + 1) * N_DH) for g in range(N_KV)]
    gcols = [slice(g * wide, (g + 1) * wide) for g in range(N_KV)]

    def values(v_ref, first, count, rows):
        return jnp.concatenate(
            [jnp.concatenate([v_ref[first + j, rows, :], ones_v], axis=0) for j in range(count)], axis=1)

    start_w = pl.multiple_of(jnp.maximum(q0 - WINDOW, 0), LANES)
    qpads = []
    for g in range(N_KV):
        q4 = jnp.concatenate(
            [qT_ref[(g * N_HG + h) * N_DH:(g * N_HG + h + 1) * N_DH, :] for h in range(N_HG)], axis=1)
        qpads.append(jnp.concatenate([q4, zeros_q] if g == 0 else [zeros_q, q4], axis=0))
        rhs_ref[0:2 * N_DH, gcols[g]] = qpads[g]

    def scores(kt, g):
        start = pl.multiple_of(kt * KEY_TILE, KEY_TILE)
        lhs = jnp.concatenate([ks_ref[pl.ds(start, KEY_TILE), :], e_ref[pl.ds(start, KEY_TILE), :]], axis=1)
        return _dot(lhs, rhs_ref[:, g * wide:(g + 1) * wide])

    def produce(kt, s_ref, c_ref):
        for g in range(N_KV):
            s = scores(kt, g)
            s_ref[g] = s
            c_ref[g] = jnp.max(s, axis=0, keepdims=True)

    def consume(kt, s_ref, c_ref, causal_tile=False):
        for g in range(N_KV):
            s = s_ref[g]
            if causal_tile:
                kpos = kt * KEY_TILE + lax.broadcasted_iota(jnp.int32, (KEY_TILE, 1), 0)
                s = jnp.where(kpos <= tpos_w, s, NEG)
                smax = jnp.max(s, axis=0, keepdims=True)
            else:
                smax = c_ref[g]
            m = m_ref[g]
            m_new = jnp.maximum(m, smax)
            p = jnp.exp2(s - m_new).astype(BF16)
            vt = values(vsT_ref, kt * (KEY_TILE // LANES), KEY_TILE // LANES, slice(g * N_DH, (g + 1) * N_DH))
            acc_ref[g] = jnp.exp2(m - m_new) * acc_ref[g] + _dot(vt, p)
            m_ref[g] = m_new

    ratio = SEL_BLOCK // CMP_STRIDE

    def front(nrows):
        nbv = nrows * ratio
        cend = lax.broadcasted_iota(jnp.int32, (nbv, 1), 0) * CMP_STRIDE + (CMP_LEN - 1)
        cmask = cend <= tpos_w
        any_visible = tpos_w >= CMP_LEN - 1
        sc = [_dot(kcmp_ref[0:nbv, :], qpads[g]) for g in range(N_KV)]
        sw = [_dot(kw_ref[pl.ds(start_w, WIN_SPAN), :], qpads[g]) for g in range(N_KV)]
        imp = []
        for g in range(N_KV):
            s = jnp.where(cmask, sc[g], NEG)
            pc = jnp.exp2(s - jnp.max(s, axis=0, keepdims=True))
            lc = jnp.sum(pc, axis=0, keepdims=True)
            pc = pc * jnp.where(any_visible, 1.0 / lc, 0.0)
            ocmp_ref[g] = _dot(vcmpT_ref[grows[g], 0:nbv], pc.astype(BF16))
            psum = pc[:, 0:tq]
            for h in range(1, N_HG):
                psum = psum + pc[:, h * tq:(h + 1) * tq]
            parts = []
            for cchunk in range(tq // LANES):
                ps_ref[g, cchunk, 0:8, :] = jnp.zeros((8, LANES), F32)
                ps_ref[g, cchunk, 8:8 + nbv, :] = psum[:, cchunk * LANES:(cchunk + 1) * LANES]
                acc = None
                for k in range(-((CMP_LEN - 1) // CMP_STRIDE), ratio):
                    part = ps_ref[g, cchunk, pl.ds(8 + k, nrows, stride=ratio), :]
                    acc = part if acc is None else acc + part
                parts.append(acc)
            imp.append(jnp.concatenate(parts, axis=1))

        jblk = lax.broadcasted_iota(jnp.int32, (nrows, tq), 0)
        cur = tpos // SEL_BLOCK
        forced = (jblk == 0) | (jblk == cur) | (jblk == cur - 1)
        cand = (jblk >= 1) & (jblk <= cur - 2)
        jblk_f = jblk.astype(F32)
        val = [jnp.where(cand, imp[g], -jnp.inf) for g in range(N_KV)]
        for _ in range(n_rounds):
            for g in range(N_KV):
                mx = jnp.max(val[g], axis=0, keepdims=True)
                first = jnp.min(jnp.where(val[g] == mx, jblk_f, float(nrows)), axis=0, keepdims=True)
                val[g] = jnp.where(jblk_f == first, -jnp.inf, val[g])
        for g in range(N_KV):
            picked = cand & (val[g] == -jnp.inf)
            bias = jnp.where(forced | picked, 0.0, NEG).astype(BF16)
            if nselp > nrows:
                bias = jnp.concatenate([bias, jnp.zeros((nselp - nrows, tq), BF16)], axis=0)
            rhs_ref[2 * N_DH:, gcols[g]] = jnp.concatenate([bias] * N_HG, axis=1)

        produce(0, s0_ref, c0_ref)

        dist = (tpos_w - start_w) - lax.broadcasted_iota(jnp.int32, (WIN_SPAN, 1), 0)
        wmask = lax.bitcast_convert_type(dist, jnp.uint32) < WINDOW
        for g in range(N_KV):
            s = jnp.where(wmask, sw[g], NEG)
            pw = jnp.exp2(s - jnp.max(s, axis=0, keepdims=True))
            ow = _dot(values(vwT_ref, start_w // LANES, WIN_SPAN // LANES, grows[g]), pw.astype(BF16))
            owin_ref[g] = ow[0:N_DH, :] * (1.0 / ow[N_DH:N_DH + 1, :])

    n_full = qi // (KEY_TILE // tq)
    odd = n_full % 2
    m_ref[...] = jnp.full(m_ref.shape, NEG, F32)
    acc_ref[...] = jnp.zeros(acc_ref.shape, F32)
    n_var = 4 if nsel % 32 == 0 else 1
    if n_var == 1:
        front(nsel)
    else:
        tiles_per_var = (nsel * SEL_BLOCK // tq) // n_var
        for v in range(n_var):
            pl.when(qi // tiles_per_var == v)(functools.partial(front, (v + 1) * nsel // n_var))

    def pair(kt):
        produce(kt + 1, s1_ref, c1_ref)
        consume(kt, s0_ref, c0_ref)
        produce(kt + 2, s0_ref, c0_ref)
        consume(kt + 1, s1_ref, c1_ref)

    def quad(j, _):
        pair(4 * j)
        pair(4 * j + 2)
        return 0

    n_quad = n_full // 4
    lax.fori_loop(0, n_quad, quad, 0)

    @pl.when(n_full - 4 * n_quad >= 2)
    def _():
        pair(4 * n_quad)

    @pl.when(odd == 0)
    def _():
        consume(n_full, s0_ref, c0_ref, causal_tile=True)

    @pl.when(odd == 1)
    def _():
        produce(n_full, s1_ref, c1_ref)
        consume(n_full - 1, s0_ref, c0_ref)
        consume(n_full, s1_ref, c1_ref, causal_tile=True)

    outs = []
    for g in range(N_KV):
        o_slc = acc_ref[g, 0:N_DH, :] * (1.0 / acc_ref[g, N_DH:N_DH + 1, :])
        for h in range(N_HG):
            cs = slice(h * tq, (h + 1) * tq)
            r = 2 * M_HEADS + (g * N_HG + h) * 3
            outs.append(gT_ref[r:r + 1, :] * ocmp_ref[g, :, cs] + gT_ref[r + 1:r + 2, :] * o_slc[:, cs]
                        + gT_ref[r + 2:r + 3, :] * owin_ref[g, :, cs])

    o_ref[...] = jnp.concatenate(outs, axis=0).T


def _nsa(qT, kcmp, vcmpT, ks, emap, vsT, kw, vwT, smallT, batch, seq):
    t = qT.shape[1]
    nq = seq // Q_TILE
    nb = kcmp.shape[1]
    nsel = seq // SEL_BLOCK
    nselp = emap.shape[1]
    wide = N_HG * Q_TILE
    n_rounds = max(min(SEL_TOPN, nsel) - 3, 0)
    kern = functools.partial(_nsa_kernel, nsel=nsel, n_rounds=n_rounds)
    return pl.pallas_call(
        kern,
        grid=(batch, nq),
        in_specs=[
            pl.BlockSpec((N_Q, Q_TILE), lambda b, i: (0, b * nq + i)),
            pl.BlockSpec((None, nb, N_KVW), lambda b, i: (b, 0, 0)),
            pl.BlockSpec((None, N_KVW, nb), lambda b, i: (b, 0, 0)),
            pl.BlockSpec((seq, N_KVW), lambda b, i: (b, 0)),
            pl.BlockSpec((seq, nselp), lambda b, i: (0, 0)),
            pl.BlockSpec((seq // LANES, N_KVW, LANES), lambda b, i: (b, 0, 0)),
            pl.BlockSpec((seq, N_KVW), lambda b, i: (b, 0)),
            pl.BlockSpec((seq // LANES, N_KVW, LANES), lambda b, i: (b, 0, 0)),
            pl.BlockSpec((32, Q_TILE), lambda b, i: (0, b * nq + i)),
        ],
        out_specs=pl.BlockSpec((Q_TILE, N_Q), lambda b, i: (b * nq + i, 0)),
        out_shape=jax.ShapeDtypeStruct((t, N_Q), F32),
        scratch_shapes=[
            pltpu.VMEM((2 * N_DH + nselp, N_KV * wide), BF16),
            pltpu.VMEM((N_KV, Q_TILE // LANES, nb + 8, LANES), F32),
            pltpu.VMEM((N_KV, N_DH, wide), F32),
            pltpu.VMEM((N_KV, N_DH, wide), F32),
            pltpu.VMEM((N_KV, KEY_TILE, wide), F32),
            pltpu.VMEM((N_KV, KEY_TILE, wide), F32),
            pltpu.VMEM((N_KV, 1, wide), F32),
            pltpu.VMEM((N_KV, 1, wide), F32),
            pltpu.VMEM((N_KV, 1, wide), F32),
            pltpu.VMEM((N_KV, N_DH + 16, wide), F32),
        ],
        compiler_params=_params(("arbitrary", "arbitrary")),
        name="nsa",
    )(qT, kcmp, vcmpT, ks, emap, vsT, kw, vwT, smallT)


def _merge_kernel(x_ref, n1w_ref, wg_ref, gb_ref, hm_ref, on_ref, wm_ref, wn_ref, wo_ref, o_ref):
    x = x_ref[...]
    hn = _rmsnorm_rows(x, n1w_ref[...]).astype(BF16)
    gm = _sigmoid(_dot(hn, wg_ref[:, 0:D_MODEL]) + gb_ref[0:1, :])
    gn = _sigmoid(_dot(hn, wg_ref[:, D_MODEL:]) + gb_ref[1:2, :])
    y = gm * _dot(hm_ref[...].astype(BF16), wm_ref[...]) + gn * _dot(on_ref[...].astype(BF16), wn_ref[...])
    o_ref[...] = x + _dot(y.astype(BF16), wo_ref[...])


def _merge(x2, n1w, wg, gb, hm, on, wm, wn, wo, tm):
    t = x2.shape[0]
    const = lambda i: (0, 0)
    return pl.pallas_call(
        _merge_kernel,
        grid=(t // tm,),
        in_specs=[
            pl.BlockSpec((tm, D_MODEL), lambda i: (i, 0)),
            pl.BlockSpec((1, D_MODEL), const),
            pl.BlockSpec((D_MODEL, 2 * D_MODEL), const),
            pl.BlockSpec((2, D_MODEL), const),
            pl.BlockSpec((tm, M_V), lambda i: (i, 0)),
            pl.BlockSpec((tm, N_Q), lambda i: (i, 0)),
            pl.BlockSpec((M_V, D_MODEL), const),
            pl.BlockSpec((N_Q, D_MODEL), const),
            pl.BlockSpec((D_MODEL, D_MODEL), const),
        ],
        out_specs=pl.BlockSpec((tm, D_MODEL), lambda i: (i, 0)),
        out_shape=jax.ShapeDtypeStruct((t, D_MODEL), F32),
        compiler_params=_params(("arbitrary",)),
        name="merge",
    )(x2, n1w, wg, gb, hm, on, wm, wn, wo)


def _ffn_kernel(x_ref, n2w_ref, wup_ref, cw_ref, cb_ref, wdn_ref, o_ref, buf_ref, *, tiles_per_seq):
    i = pl.program_id(0)
    tm = x_ref.shape[0]
    x = x_ref[...]
    hn = _rmsnorm_rows(x, n2w_ref[...]).astype(BF16)

    @pl.when(i % tiles_per_seq == 0)
    def _():
        buf_ref[0:8, :] = jnp.zeros((8, D_FF), F32)

    a = _dot(hn, wup_ref[:, 0:D_FF])
    buf_ref[8:8 + tm, :] = a
    acc = cb_ref[...] + cw_ref[FFN_CONV - 1:FFN_CONV, :] * a
    for k in range(FFN_CONV - 1):
        acc = acc + cw_ref[k:k + 1, :] * buf_ref[8 - (FFN_CONV - 1) + k:8 - (FFN_CONV - 1) + k + tm, :]
    buf_ref[0:8, :] = buf_ref[tm:tm + 8, :]
    v = _dot(hn, wup_ref[:, D_FF:])
    o_ref[...] = x + _dot((_gelu(acc) * v).astype(BF16), wdn_ref[...])


def _ffn(x2, n2w, wup, cw, cb, wdn, seq, tm):
    t = x2.shape[0]
    const = lambda i: (0, 0)
    kern = functools.partial(_ffn_kernel, tiles_per_seq=seq // tm)
    return pl.pallas_call(
        kern,
        grid=(t // tm,),
        in_specs=[
            pl.BlockSpec((tm, D_MODEL), lambda i: (i, 0)),
            pl.BlockSpec((1, D_MODEL), const),
            pl.BlockSpec((D_MODEL, 2 * D_FF), const, pipeline_mode=pl.Buffered(1)),
            pl.BlockSpec((FFN_CONV, D_FF), const),
            pl.BlockSpec((1, D_FF), const),
            pl.BlockSpec((D_FF, D_MODEL), const, pipeline_mode=pl.Buffered(1)),
        ],
        out_specs=pl.BlockSpec((tm, D_MODEL), lambda i: (i, 0)),
        out_shape=jax.ShapeDtypeStruct((t, D_MODEL), F32),
        scratch_shapes=[pltpu.VMEM((tm + 8, D_FF), F32)],
        compiler_params=_params(("arbitrary",)),
        name="ffn",
    )(x2, n2w, wup, cw, cb, wdn)


def _cols(w, *names):
    return jnp.concatenate([w[:, _OFF[n][0]:_OFF[n][1]] for n in names], axis=1)


def _layer(x, n1w, w_in, m_conv_w, m_conv_b, m_igate_b, m_fgate_b, m_out_norm_w,
           q_norm_w, kcmp_norm_w, kslc_norm_w, kwin_norm_w,
           cmp_k_pe, cmp_k_w1, cmp_k_w2, cmp_v_pe, cmp_v_w1, cmp_v_w2,
           w_up_m, w_up_n, merge_gate_b, w_out, norm2_w, ffn_w_up, ffn_conv_w, ffn_conv_b, ffn_w_down):
    batch, seq, _ = x.shape
    t = batch * seq
    x2 = x.reshape(t, D_MODEL)
    n1w2 = n1w.reshape(1, D_MODEL)
    tm = 256
    tm_proj = 512

    w_m = _cols(w_in, "mq", "mk", "mo").astype(BF16)
    w_mvT = _cols(w_in, "mv", "mi", "mf", "ng").T.astype(BF16)
    w_n = _cols(w_in, "nq", "kc", "vc", "ks", "vs", "kw", "vw").astype(BF16)
    w_g = _cols(w_in, "gm", "gn").astype(BF16)
    sbias = jnp.concatenate([m_igate_b, m_fgate_b, jnp.zeros((3 * N_HEADS,), F32)]).reshape(32, 1)

    half = N_DH // 2
    pos = jnp.arange(seq, dtype=F32)
    inv = ROPE_THETA ** (-jnp.arange(0, N_DH, 2, dtype=F32) / N_DH)
    ang = pos[:, None] * inv[None, :]
    cos, sin = jnp.cos(ang), jnp.sin(ang)
    cosn = jnp.tile(cos, (1, N_KVW // half))
    sinn = jnp.tile(jnp.concatenate([-sin, sin], axis=1), (1, N_KV))
    cosT, sinT = cos.T, sin.T
    knw = jnp.zeros((8, N_KVW), F32).at[0:3].set(
        jnp.stack([jnp.tile(w, N_KV) for w in (kcmp_norm_w, kslc_norm_w, kwin_norm_w)]))

    qT_m, k_m, vT_m, og, small, smallT = _proj_m(
        x2, n1w2, w_m, w_mvT, m_conv_w, m_conv_b.reshape(1, -1), sbias, seq, tm_proj)
    qT, kvb, ks, kw, vsT, vwT = _proj_n(
        x2, n1w2, w_n, q_norm_w.reshape(N_DH, 1), knw, cosn, sinn, cosT, sinT, seq, tm_proj)

    nb = seq // CMP_STRIDE
    xkv = kvb.reshape(2 * N_KV, batch, nb, CMP_STRIDE * N_DH)
    pe = jnp.stack([cmp_k_pe.reshape(1, -1), cmp_v_pe.reshape(1, -1)])
    w1 = jnp.stack([cmp_k_w1, cmp_v_w1]).astype(BF16)
    w2 = jnp.stack([cmp_k_w2, cmp_v_w2]).astype(BF16)
    kcmp, vcmpT = _compress(xkv, pe, w1, w2)

    onw = jnp.zeros((8, M_DV), F32).at[0:M_HEADS].set(m_out_norm_w)
    hm = _mlstm(qT_m, k_m, vT_m, small, smallT, og, onw, batch, seq, 4 * MLSTM_CHUNK)
    nselp = -(-(seq // SEL_BLOCK) // LANES) * LANES
    emap = (np.arange(seq)[:, None] // SEL_BLOCK == np.arange(nselp)[None, :]).astype(np.float32)
    on = _nsa(qT, kcmp, vcmpT, ks, jnp.asarray(emap, dtype=BF16), vsT, kw, vwT, smallT, batch, seq)

    x1 = _merge(x2, n1w2, w_g, merge_gate_b, hm, on, w_up_m.astype(BF16), w_up_n.astype(BF16),
                w_out.astype(BF16), tm_proj)
    out = _ffn(x1, norm2_w.reshape(1, D_MODEL), ffn_w_up.astype(BF16), ffn_conv_w, ffn_conv_b.reshape(1, -1),
               ffn_w_down.astype(BF16), seq, tm)
    return out.reshape(batch, seq, D_MODEL)


def kernel(x, norm1_w, w_in, m_conv_w, m_conv_b, m_igate_b, m_fgate_b, m_out_norm_w, q_norm_w, kcmp_norm_w,
           kslc_norm_w, kwin_norm_w, cmp_k_pe, cmp_k_w1, cmp_k_w2, cmp_v_pe, cmp_v_w1, cmp_v_w2, w_up_m, w_up_n,
           merge_gate_b, w_out, norm2_w, ffn_w_up, ffn_conv_w, ffn_conv_b, ffn_w_down):
    params = (norm1_w, w_in, m_conv_w, m_conv_b, m_igate_b, m_fgate_b, m_out_norm_w, q_norm_w, kcmp_norm_w,
              kslc_norm_w, kwin_norm_w, cmp_k_pe, cmp_k_w1, cmp_k_w2, cmp_v_pe, cmp_v_w1, cmp_v_w2, w_up_m, w_up_n,
              merge_gate_b, w_out, norm2_w, ffn_w_up, ffn_conv_w, ffn_conv_b, ffn_w_down)
    for layer in range(norm1_w.shape[0]):
        x = _layer(x, *[p[layer] for p in params])
    return x
```

```python
import functools
import math

import jax
import jax.numpy as jnp
import numpy as np
from jax import lax
from jax.experimental import pallas as pl
from jax.experimental.pallas import tpu as pltpu

D_MODEL = 1024
EPS = 1e-6
ROPE_THETA = 10000.0
NEG = -1e30
M_HEADS = 4
M_DQK = 64
M_DV = 128
M_CONV = 4
M_QK = M_HEADS * M_DQK
M_V = M_HEADS * M_DV
N_HEADS = 8
N_KV = 2
N_HG = N_HEADS // N_KV
N_DH = 64
N_Q = N_HEADS * N_DH
N_KVW = N_KV * N_DH
CMP_LEN = 32
CMP_STRIDE = 16
CMP_HIDDEN = 256
SEL_BLOCK = 64
SEL_TOPN = 16
WINDOW = 512
D_FF = 2816
FFN_CONV = 3

_OFF = {}
_o = 0
for _name, _size in (("mq", M_QK), ("mk", M_QK), ("mv", M_V), ("mo", M_V), ("mi", M_HEADS), ("mf", M_HEADS),
                     ("nq", N_Q), ("kc", N_KVW), ("vc", N_KVW), ("ks", N_KVW), ("vs", N_KVW), ("kw", N_KVW),
                     ("vw", N_KVW), ("ng", 3 * N_HEADS), ("gm", D_MODEL), ("gn", D_MODEL)):
    _OFF[_name] = (_o, _o + _size)
    _o += _size

LANES = 128
MLSTM_CHUNK = 128
SUB_ROWS = 256
Q_TILE = 128
KEY_TILE = 256
WIN_SPAN = WINDOW + Q_TILE
VMEM_LIMIT = 56 * 1024 * 1024

LOG2E = math.log2(math.e)

F32 = jnp.float32
BF16 = jnp.bfloat16
HIGHEST = lax.Precision.HIGHEST


def _dot(a, b):
    return jnp.dot(a, b, preferred_element_type=F32)


def _dot_nt(a, b):
    return lax.dot_general(a, b, (((1,), (1,)), ((), ())), preferred_element_type=F32)


def _rmsnorm_rows(x, w):
    return x * lax.rsqrt(jnp.mean(x * x, axis=-1, keepdims=True) + EPS) * w


def _sigmoid(x):
    return 1.0 / (1.0 + jnp.exp(-x))


def _gelu(x):
    return 0.5 * x * (1.0 + lax.erf(x * (1.0 / math.sqrt(2.0))))


def _params(sem):
    return pltpu.CompilerParams(dimension_semantics=sem, vmem_limit_bytes=VMEM_LIMIT)


def _proj_m_kernel(x_ref, n1w_ref, w_ref, wvT_ref, cw_ref, cb_ref, sb_ref,
                   qT_ref, k_ref, vT_ref, og_ref, small_ref, smallT_ref, buf_ref, *, tiles_per_seq):
    i = pl.program_id(0)
    tm = x_ref.shape[0]
    sub = SUB_ROWS

    @pl.when(i % tiles_per_seq == 0)
    def _():
        buf_ref[0:8, :] = jnp.zeros((8, 2 * M_QK), F32)

    hns = [_rmsnorm_rows(x_ref[r * sub:(r + 1) * sub, :], n1w_ref[...]).astype(BF16) for r in range(tm // sub)]
    for r in range(tm // sub):
        rows = slice(r * sub, (r + 1) * sub)
        hn = hns[r]
        qk = _dot(hn, w_ref[:, 0:2 * M_QK])
        buf_ref[8:8 + sub, :] = qk
        acc = cb_ref[...] + cw_ref[M_CONV - 1:M_CONV, :] * qk
        for k in range(M_CONV - 1):
            acc = acc + cw_ref[k:k + 1, :] * buf_ref[8 - (M_CONV - 1) + k:8 - (M_CONV - 1) + k + sub, :]
        buf_ref[0:8, :] = buf_ref[sub:sub + 8, :]
        act = acc * _sigmoid(acc)
        qT_ref[:, rows] = (act[:, 0:M_QK] * (M_DQK ** -0.5)).T.astype(BF16)
        k_ref[rows, :] = act[:, M_QK:2 * M_QK].astype(BF16)
        vs = _dot_nt(wvT_ref[...], hn)
        vT_ref[:, rows] = vs[0:M_V, :].astype(BF16)
        og_ref[rows, :] = _sigmoid(_dot(hn, w_ref[:, 2 * M_QK:2 * M_QK + M_V]))
        smT = vs[M_V:, :] + sb_ref[...]
        rowi = lax.broadcasted_iota(jnp.int32, smT.shape, 0)
        logsig = jnp.minimum(smT, 0.0) - jnp.log1p(jnp.exp(-jnp.abs(smT)))
        smT = jnp.where(rowi < M_HEADS, smT, jnp.where(rowi < 2 * M_HEADS, logsig, _sigmoid(smT)))
        smallT_ref[:, rows] = smT
        small_ref[rows, :] = jnp.concatenate([smT, jnp.zeros((LANES - 32, sub), F32)], axis=0).T


def _proj_m(x2, n1w, w, wvT, cw, cb, sb, seq, tm):
    t = x2.shape[0]
    ncol = w.shape[1]
    kern = functools.partial(_proj_m_kernel, tiles_per_seq=seq // tm)
    return pl.pallas_call(
        kern,
        grid=(t // tm,),
        in_specs=[
            pl.BlockSpec((tm, D_MODEL), lambda i: (i, 0)),
            pl.BlockSpec((1, D_MODEL), lambda i: (0, 0)),
            pl.BlockSpec((D_MODEL, ncol), lambda i: (0, 0)),
            pl.BlockSpec((M_V + 32, D_MODEL), lambda i: (0, 0)),
            pl.BlockSpec((M_CONV, 2 * M_QK), lambda i: (0, 0)),
            pl.BlockSpec((1, 2 * M_QK), lambda i: (0, 0)),
            pl.BlockSpec((32, 1), lambda i: (0, 0)),
        ],
        out_specs=[
            pl.BlockSpec((M_QK, tm), lambda i: (0, i)),
            pl.BlockSpec((tm, M_QK), lambda i: (i, 0)),
            pl.BlockSpec((M_V, tm), lambda i: (0, i)),
            pl.BlockSpec((tm, M_V), lambda i: (i, 0)),
            pl.BlockSpec((tm, LANES), lambda i: (i, 0)),
            pl.BlockSpec((32, tm), lambda i: (0, i)),
        ],
        out_shape=[
            jax.ShapeDtypeStruct((M_QK, t), BF16),
            jax.ShapeDtypeStruct((t, M_QK), BF16),
            jax.ShapeDtypeStruct((M_V, t), BF16),
            jax.ShapeDtypeStruct((t, M_V), F32),
            jax.ShapeDtypeStruct((t, LANES), F32),
            jax.ShapeDtypeStruct((32, t), F32),
        ],
        scratch_shapes=[pltpu.VMEM((SUB_ROWS + 8, 2 * M_QK), F32)],
        compiler_params=_params(("arbitrary",)),
        name="proj_m",
    )(x2, n1w, w, wvT, cw, cb, sb)


def _proj_n_kernel(x_ref, n1w_ref, w_ref, qnw_ref, knw_ref, cosn_ref, sinn_ref, cosT_ref, sinT_ref,
                   qT_ref, kvb_ref, ks_ref, kw_ref, vsT_ref, vwT_ref, regroup_ref):
    tm = x_ref.shape[0]
    hn = _rmsnorm_rows(x_ref[...], n1w_ref[...]).astype(BF16)
    qT = _dot(hn, w_ref[:, 0:N_Q]).T
    cosT = cosT_ref[...]
    sinT = sinT_ref[...]
    qnw = qnw_ref[...]
    half = N_DH // 2
    for h in range(N_HEADS):
        xh = qT[h * N_DH:(h + 1) * N_DH, :]
        xn = xh * lax.rsqrt(jnp.mean(xh * xh, axis=0, keepdims=True) + EPS) * qnw
        x1 = xn[0:half, :]
        x2 = xn[half:, :]
        o = jnp.concatenate([x1 * cosT - x2 * sinT, x2 * cosT + x1 * sinT], axis=0) * (LOG2E * N_DH ** -0.5)
        qT_ref[h * N_DH:(h + 1) * N_DH, :] = o.astype(BF16)

    cosn = cosn_ref[...]
    sinn = sinn_ref[...]
    li = lax.broadcasted_iota(jnp.int32, (N_KVW, N_KVW), 0)
    lj = lax.broadcasted_iota(jnp.int32, (N_KVW, N_KVW), 1)
    head_sum = (li // N_DH == lj // N_DH).astype(BF16)
    swap_half = (lj == li + jnp.where(li % N_DH < half, half, -half)).astype(BF16)
    head_sum = jnp.concatenate([head_sum, head_sum], axis=0)
    swap_half = jnp.concatenate([swap_half, swap_half], axis=0)

    def lane_map(v, m01x2):
        hi = v.astype(BF16)
        lo = (v - hi.astype(F32)).astype(BF16)
        return _dot(jnp.concatenate([hi, lo], axis=1), m01x2)

    c0 = N_Q
    k_raw = [_dot(hn, w_ref[:, c0 + 2 * i * N_KVW:c0 + (2 * i + 1) * N_KVW]) for i in range(3)]
    vc, vs, vw = [_dot(hn, w_ref[:, c0 + (2 * i + 1) * N_KVW:c0 + (2 * i + 2) * N_KVW]) for i in range(3)]
    ms = [lane_map(k * k, head_sum) * (1.0 / N_DH) for k in k_raw]
    kn = [k_raw[i] * lax.rsqrt(ms[i] + EPS) * knw_ref[i:i + 1, :] for i in range(3)]
    kc, ks, kw = [kn[i] * cosn + lane_map(kn[i], swap_half) * sinn for i in range(3)]
    lane_kv = lax.broadcasted_iota(jnp.int32, (tm // CMP_STRIDE, N_KVW), 1)
    for a, arr in enumerate((kc, vc)):
        regroup_ref[...] = arr
        for l in range(0, CMP_STRIDE, 2):
            even = regroup_ref[pl.ds(l, tm // CMP_STRIDE, stride=CMP_STRIDE), :]
            odd = regroup_ref[pl.ds(l + 1, tm // CMP_STRIDE, stride=CMP_STRIDE), :]
            cols = slice((l // 2) * N_KVW, (l // 2 + 1) * N_KVW)
            kvb_ref[a * N_KV, :, cols] = jnp.where(lane_kv < N_DH, even, pltpu.roll(odd, N_DH, 1))
            kvb_ref[a * N_KV + 1, :, cols] = jnp.where(lane_kv < N_DH, pltpu.roll(even, N_DH, 1), odd)
    ks_ref[...] = ks.astype(BF16)
    kw_ref[...] = kw.astype(BF16)
    vsT = vs.T.astype(BF16)
    vwT = vw.T.astype(BF16)
    for j in range(tm // LANES):
        vsT_ref[j] = vsT[:, j * LANES:(j + 1) * LANES]
        vwT_ref[j] = vwT[:, j * LANES:(j + 1) * LANES]


def _proj_n(x2, n1w, w, qnw, knw, cosn, sinn, cosT, sinT, seq, tm):
    t = x2.shape[0]
    ncol = w.shape[1]
    tps = seq // tm
    half = N_DH // 2
    return pl.pallas_call(
        _proj_n_kernel,
        grid=(t // tm,),
        in_specs=[
            pl.BlockSpec((tm, D_MODEL), lambda i: (i, 0)),
            pl.BlockSpec((1, D_MODEL), lambda i: (0, 0)),
            pl.BlockSpec((D_MODEL, ncol), lambda i: (0, 0)),
            pl.BlockSpec((N_DH, 1), lambda i: (0, 0)),
            pl.BlockSpec((8, N_KVW), lambda i: (0, 0)),
            pl.BlockSpec((tm, N_KVW), lambda i: (i % tps, 0)),
            pl.BlockSpec((tm, N_KVW), lambda i: (i % tps, 0)),
            pl.BlockSpec((half, tm), lambda i: (0, i % tps)),
            pl.BlockSpec((half, tm), lambda i: (0, i % tps)),
        ],
        out_specs=[
            pl.BlockSpec((N_Q, tm), lambda i: (0, i)),
            pl.BlockSpec((2 * N_KV, tm // CMP_STRIDE, CMP_STRIDE * N_DH), lambda i: (0, i, 0)),
            pl.BlockSpec((tm, N_KVW), lambda i: (i, 0)),
            pl.BlockSpec((tm, N_KVW), lambda i: (i, 0)),
            pl.BlockSpec((tm // LANES, N_KVW, LANES), lambda i: (i, 0, 0)),
            pl.BlockSpec((tm // LANES, N_KVW, LANES), lambda i: (i, 0, 0)),
        ],
        out_shape=[
            jax.ShapeDtypeStruct((N_Q, t), BF16),
            jax.ShapeDtypeStruct((2 * N_KV, t // CMP_STRIDE, CMP_STRIDE * N_DH), F32),
            jax.ShapeDtypeStruct((t, N_KVW), BF16),
            jax.ShapeDtypeStruct((t, N_KVW), BF16),
            jax.ShapeDtypeStruct((t // LANES, N_KVW, LANES), BF16),
            jax.ShapeDtypeStruct((t // LANES, N_KVW, LANES), BF16),
        ],
        scratch_shapes=[pltpu.VMEM((tm, N_KVW), F32)],
        compiler_params=_params(("arbitrary",)),
        name="proj_n",
    )(x2, n1w, w, qnw, knw, cosn, sinn, cosT, sinT)


def _compress_kernel(x_ref, pe_ref, w1_ref, w2_ref, kcmp_ref, vcmpT_ref):
    nb = x_ref.shape[2]
    half = (CMP_LEN // 2) * N_DH

    def mlp(a, kv):
        pe = pe_ref[kv]
        x = x_ref[a, 0]
        first = _dot((x + pe[:, 0:half]).astype(BF16), w1_ref[kv, 0:half, :])
        second = _dot((x + pe[:, half:]).astype(BF16), w1_ref[kv, half:, :])
        hid = first + pltpu.roll(second, nb - 1, 0)
        return _dot(_gelu(hid).astype(BF16), w2_ref[kv])

    kcmp_ref[...] = jnp.concatenate([mlp(g, 0) for g in range(N_KV)], axis=1).astype(BF16)
    vcmpT_ref[...] = jnp.concatenate([mlp(N_KV + g, 1) for g in range(N_KV)], axis=1).T.astype(BF16)


def _compress(xkv, pe, w1, w2):
    na, batch, nb, width = xkv.shape
    const3 = lambda b: (0, 0, 0)
    return pl.pallas_call(
        _compress_kernel,
        grid=(batch,),
        in_specs=[
            pl.BlockSpec((na, 1, nb, width), lambda b: (0, b, 0, 0)),
            pl.BlockSpec((2, 1, CMP_LEN * N_DH), const3),
            pl.BlockSpec((2, CMP_LEN * N_DH, CMP_HIDDEN), const3),
            pl.BlockSpec((2, CMP_HIDDEN, N_DH), const3),
        ],
        out_specs=[
            pl.BlockSpec((None, nb, N_KVW), lambda b: (b, 0, 0)),
            pl.BlockSpec((None, N_KVW, nb), lambda b: (b, 0, 0)),
        ],
        out_shape=[
            jax.ShapeDtypeStruct((batch, nb, N_KVW), BF16),
            jax.ShapeDtypeStruct((batch, N_KVW, nb), BF16),
        ],
        compiler_params=_params(("arbitrary",)),
        name="compress",
    )(xkv, pe, w1, w2)


def _mlstm_kernel(qT_ref, k_ref, vT_ref, small_ref, smallT_ref, og_ref, onw_ref, o_ref, c_ref, m_ref):
    L = MLSTM_CHUNK
    tb = k_ref.shape[0]

    @pl.when(pl.program_id(1) == 0)
    def _():
        c_ref[...] = jnp.zeros(c_ref.shape, F32)
        m_ref[...] = jnp.zeros(m_ref.shape, F32)

    row = lax.broadcasted_iota(jnp.int32, (L, L), 0)
    col = lax.broadcasted_iota(jnp.int32, (L, L), 1)
    causal = row <= col
    tril = (col <= row).astype(F32)
    triu = causal.astype(F32)
    ones_rows = jnp.ones((16, L), BF16)
    zeros_q = jnp.zeros((M_DQK, L), BF16)

    heads = range(M_HEADS)
    chunks = range(tb // L)
    sls = [slice(c * L, (c + 1) * L) for c in chunks]
    sms = [small_ref[sl, :] for sl in sls]
    bcols = [jnp.dot(tril, sm, preferred_element_type=F32, precision=HIGHEST) for sm in sms]
    brows = [jnp.dot(smallT_ref[0:8, sl], triu, preferred_element_type=F32, precision=HIGHEST) for sl in sls]
    c_state = [c_ref[h] for h in heads]
    m_state = [m_ref[h:h + 1, 0:1] for h in heads]
    for c in chunks:
        sl = sls[c]
        k_pairs = [k_ref[sl, p * LANES:(p + 1) * LANES] for p in range(M_HEADS // 2)]
        qT_pad, s, qc, vT_aug = [], [], [], []
        for h in heads:
            qT_h = qT_ref[h * M_DQK:(h + 1) * M_DQK, sl]
            qT_pad.append(jnp.concatenate([qT_h, zeros_q] if h % 2 == 0 else [zeros_q, qT_h], axis=0))
            s.append(_dot(k_pairs[h // 2], qT_pad[h]))
            qc.append(_dot(c_state[h].astype(BF16), qT_pad[h]))
            vT_aug.append(jnp.concatenate([vT_ref[h * M_DV:(h + 1) * M_DV, sl], ones_rows], axis=0))
        m_t, isc, sw, kw, m_new, decay, scale = [], [], [], [], [], [], []
        for h in heads:
            in_head = (col // M_DQK) == (h % 2)
            b_col = bcols[c][:, M_HEADS + h:M_HEADS + h + 1]
            i_col = sms[c][:, h:h + 1]
            b_row = brows[c][M_HEADS + h:M_HEADS + h + 1, :]
            g = b_row[:, L - 1:L]
            d = jnp.where(causal, b_row + (i_col - b_col), NEG)
            inter = b_row + m_state[h]
            m_t.append(jnp.maximum(inter, jnp.max(d, axis=0, keepdims=True)))
            isc.append(jnp.exp(inter - m_t[h]))
            sw.append((s[h] * jnp.exp(d - m_t[h])).astype(BF16))
            a_col = g - b_col + i_col
            a_max = jnp.max(a_col, axis=0, keepdims=True)
            kw.append(jnp.where(in_head, k_pairs[h // 2].astype(F32) * jnp.exp(a_col - a_max), 0.0).astype(BF16))
            m_new.append(jnp.maximum(g + m_state[h], a_max))
            decay.append(jnp.exp(g + m_state[h] - m_new[h]))
            scale.append(jnp.exp(a_max - m_new[h]))
        sv = [_dot(vT_aug[h], sw[h]) for h in heads]
        dc = [_dot(vT_aug[h], kw[h]) for h in heads]
        for h in heads:
            nd = isc[h] * qc[h] + sv[h]
            den = nd[M_DV:M_DV + 1, :]
            hh = nd[0:M_DV, :] / jnp.maximum(jnp.abs(den), jnp.exp(-m_t[h]))
            hn = hh * lax.rsqrt(jnp.mean(hh * hh, axis=0, keepdims=True) + EPS)
            o_ref[sl, h * M_DV:(h + 1) * M_DV] = hn.T * onw_ref[h:h + 1, :] * og_ref[sl, h * M_DV:(h + 1) * M_DV]
            c_state[h] = decay[h] * c_state[h] + scale[h] * dc[h]
            m_state[h] = m_new[h]
    for h in heads:
        c_ref[h] = c_state[h]
        m_ref[h:h + 1, :] = jnp.broadcast_to(m_state[h], (1, LANES))


def _mlstm(qT, k, vT, small, smallT, og, onw, batch, seq, tb):
    t = k.shape[0]
    nblk = seq // tb
    return pl.pallas_call(
        _mlstm_kernel,
        grid=(batch, nblk),
        in_specs=[
            pl.BlockSpec((M_QK, tb), lambda b, j: (0, b * nblk + j)),
            pl.BlockSpec((tb, M_QK), lambda b, j: (b * nblk + j, 0)),
            pl.BlockSpec((M_V, tb), lambda b, j: (0, b * nblk + j)),
            pl.BlockSpec((tb, LANES), lambda b, j: (b * nblk + j, 0)),
            pl.BlockSpec((32, tb), lambda b, j: (0, b * nblk + j)),
            pl.BlockSpec((tb, M_V), lambda b, j: (b * nblk + j, 0)),
            pl.BlockSpec((8, M_DV), lambda b, j: (0, 0)),
        ],
        out_specs=pl.BlockSpec((tb, M_V), lambda b, j: (b * nblk + j, 0)),
        out_shape=jax.ShapeDtypeStruct((t, M_V), F32),
        scratch_shapes=[pltpu.VMEM((M_HEADS, M_DV + 16, LANES), F32), pltpu.VMEM((8, LANES), F32)],
        compiler_params=_params(("arbitrary", "arbitrary")),
        name="mlstm",
    )(qT, k, vT, small, smallT, og, onw)


def _nsa_kernel(qT_ref, kcmp_ref, vcmpT_ref, ks_ref, e_ref, vsT_ref, kw_ref, vwT_ref, gT_ref,
                o_ref, rhs_ref, ps_ref, ocmp_ref, owin_ref, s0_ref, s1_ref, c0_ref, c1_ref, m_ref, acc_ref,
                *, nsel, n_rounds):
    qi = pl.program_id(1)
    tq = Q_TILE
    nb = kcmp_ref.shape[0]
    nselp = e_ref.shape[1]
    q0 = qi * tq
    wide = N_HG * tq
    lane_w = lax.broadcasted_iota(jnp.int32, (1, wide), 1)
    tpos_w = q0 + (lane_w % tq)
    tpos = q0 + lax.broadcasted_iota(jnp.int32, (1, tq), 1)
    zeros_q = jnp.zeros((N_DH, wide), BF16)
    ones_v = jnp.ones((16, LANES), BF16)
    grows = [slice(g * N_DH, (g + 1) * N_DH) for g in range(N_KV)]
    gcols = [slice(g * wide, (g + 1) * wide) for g in range(N_KV)]

    def values(v_ref, first, count, rows):
        return jnp.concatenate(
            [jnp.concatenate([v_ref[first + j, rows, :], ones_v], axis=0) for j in range(count)], axis=1)

    start_w = pl.multiple_of(jnp.maximum(q0 - WINDOW, 0), LANES)
    qpads = []
    for g in range(N_KV):
        q4 = jnp.concatenate(
            [qT_ref[(g * N_HG + h) * N_DH:(g * N_HG + h + 1) * N_DH, :] for h in range(N_HG)], axis=1)
        qpads.append(jnp.concatenate([q4, zeros_q] if g == 0 else [zeros_q, q4], axis=0))
        rhs_ref[0:2 * N_DH, gcols[g]] = qpads[g]

    def scores(kt, g):
        start = pl.multiple_of(kt * KEY_TILE, KEY_TILE)
        lhs = jnp.concatenate([ks_ref[pl.ds(start, KEY_TILE), :], e_ref[pl.ds(start, KEY_TILE), :]], axis=1)
        return _dot(lhs, rhs_ref[:, g * wide:(g + 1) * wide])

    def produce(kt, s_ref, c_ref):
        for g in range(N_KV):
            s = scores(kt, g)
            s_ref[g] = s
            c_ref[g] = jnp.max(s, axis=0, keepdims=True)

    def consume(kt, s_ref, c_ref, causal_rows=0):
        for g in range(N_KV):
            if causal_rows:
                kpos = kt * KEY_TILE + lax.broadcasted_iota(jnp.int32, (causal_rows, 1), 0)
                s = jnp.where(kpos <= tpos_w, s_ref[g, 0:causal_rows, :], NEG)
                smax = jnp.max(s, axis=0, keepdims=True)
            else:
                s = s_ref[g]
                smax = c_ref[g]
            m = m_ref[g]
            m_new = jnp.maximum(m, smax)
            p = jnp.exp2(s - m_new).astype(BF16)
            vt = values(vsT_ref, kt * (KEY_TILE // LANES), (causal_rows or KEY_TILE) // LANES,
                        slice(g * N_DH, (g + 1) * N_DH))
            acc_ref[g] = jnp.exp2(m - m_new) * acc_ref[g] + _dot(vt, p)
            m_ref[g] = m_new

    ratio = SEL_BLOCK // CMP_STRIDE

    def front(nrows):
        nbv = nrows * ratio
        cend = lax.broadcasted_iota(jnp.int32, (nbv, 1), 0) * CMP_STRIDE + (CMP_LEN - 1)
        cmask = cend <= tpos_w
        any_visible = tpos_w >= CMP_LEN - 1
        sc = [_dot(kcmp_ref[0:nbv, :], qpads[g]) for g in range(N_KV)]
        sw = [_dot(kw_ref[pl.ds(start_w, WIN_SPAN), :], qpads[g]) for g in range(N_KV)]
        imp = []
        for g in range(N_KV):
            s = jnp.where(cmask, sc[g], NEG)
            pc = jnp.exp2(s - jnp.max(s, axis=0, keepdims=True))
            lc = jnp.sum(pc, axis=0, keepdims=True)
            pc = pc * jnp.where(any_visible, 1.0 / lc, 0.0)
            ocmp_ref[g] = _dot(vcmpT_ref[grows[g], 0:nbv], pc.astype(BF16))
            psum = pc[:, 0:tq]
            for h in range(1, N_HG):
                psum = psum + pc[:, h * tq:(h + 1) * tq]
            parts = []
            for cchunk in range(tq // LANES):
                ps_ref[g, cchunk, 0:8, :] = jnp.zeros((8, LANES), F32)
                ps_ref[g, cchunk, 8:8 + nbv, :] = psum[:, cchunk * LANES:(cchunk + 1) * LANES]
                acc = None
                for k in range(-((CMP_LEN - 1) // CMP_STRIDE), ratio):
                    part = ps_ref[g, cchunk, pl.ds(8 + k, nrows, stride=ratio), :]
                    acc = part if acc is None else acc + part
                parts.append(acc)
            imp.append(jnp.concatenate(parts, axis=1))

        jblk = lax.broadcasted_iota(jnp.int32, (nrows, tq), 0)
        cur = tpos // SEL_BLOCK
        forced = (jblk == 0) | (jblk == cur) | (jblk == cur - 1)
        cand = (jblk >= 1) & (jblk <= cur - 2)
        jblk_f = jblk.astype(F32)
        val = [jnp.where(cand, imp[g], -jnp.inf) for g in range(N_KV)]
        for _ in range(n_rounds):
            for g in range(N_KV):
                mx = jnp.max(val[g], axis=0, keepdims=True)
                first = jnp.min(jnp.where(val[g] == mx, jblk_f, float(nrows)), axis=0, keepdims=True)
                val[g] = jnp.where(jblk_f == first, -jnp.inf, val[g])
        for g in range(N_KV):
            picked = cand & (val[g] == -jnp.inf)
            bias = jnp.where(forced | picked, 0.0, NEG).astype(BF16)
            if nselp > nrows:
                bias = jnp.concatenate([bias, jnp.zeros((nselp - nrows, tq), BF16)], axis=0)
            rhs_ref[2 * N_DH:, gcols[g]] = jnp.concatenate([bias] * N_HG, axis=1)

        produce(0, s0_ref, c0_ref)

        dist = (tpos_w - start_w) - lax.broadcasted_iota(jnp.int32, (WIN_SPAN, 1), 0)
        wmask = lax.bitcast_convert_type(dist, jnp.uint32) < WINDOW
        for g in range(N_KV):
            s = jnp.where(wmask, sw[g], NEG)
            pw = jnp.exp2(s - jnp.max(s, axis=0, keepdims=True))
            ow = _dot(values(vwT_ref, start_w // LANES, WIN_SPAN // LANES, grows[g]), pw.astype(BF16))
            owin_ref[g] = ow[0:N_DH, :] * (1.0 / ow[N_DH:N_DH + 1, :])

    n_full = qi // (KEY_TILE // tq)
    odd = n_full % 2
    m_ref[...] = jnp.full(m_ref.shape, NEG, F32)
    acc_ref[...] = jnp.zeros(acc_ref.shape, F32)
    n_var = 4 if nsel % 32 == 0 else 1
    if n_var == 1:
        front(nsel)
    else:
        tiles_per_var = (nsel * SEL_BLOCK // tq) // n_var
        for v in range(n_var):
            pl.when(qi // tiles_per_var == v)(functools.partial(front, (v + 1) * nsel // n_var))

    def pair(kt):
        produce(kt + 1, s1_ref, c1_ref)
        consume(kt, s0_ref, c0_ref)
        produce(kt + 2, s0_ref, c0_ref)
        consume(kt + 1, s1_ref, c1_ref)

    def quad(j, _):
        pair(4 * j)
        pair(4 * j + 2)
        return 0

    n_quad = n_full // 4
    lax.fori_loop(0, n_quad, quad, 0)

    @pl.when(n_full - 4 * n_quad >= 2)
    def _():
        pair(4 * n_quad)

    for at_start in (False, True):
        rows = tq if at_start else KEY_TILE
        starts_tile = (q0 % KEY_TILE == 0) == at_start

        @pl.when((odd == 0) & starts_tile)
        def _():
            consume(n_full, s0_ref, c0_ref, causal_rows=rows)

        @pl.when((odd == 1) & starts_tile)
        def _():
            produce(n_full, s1_ref, c1_ref)
            consume(n_full - 1, s0_ref, c0_ref)
            consume(n_full, s1_ref, c1_ref, causal_rows=rows)

    outs = []
    for g in range(N_KV):
        o_slc = acc_ref[g, 0:N_DH, :] * (1.0 / acc_ref[g, N_DH:N_DH + 1, :])
        for h in range(N_HG):
            cs = slice(h * tq, (h + 1) * tq)
            r = 2 * M_HEADS + (g * N_HG + h) * 3
            outs.append(gT_ref[r:r + 1, :] * ocmp_ref[g, :, cs] + gT_ref[r + 1:r + 2, :] * o_slc[:, cs]
                        + gT_ref[r + 2:r + 3, :] * owin_ref[g, :, cs])

    o_ref[...] = jnp.concatenate(outs, axis=0).T


def _nsa(qT, kcmp, vcmpT, ks, emap, vsT, kw, vwT, smallT, batch, seq):
    t = qT.shape[1]
    nq = seq // Q_TILE
    nb = kcmp.shape[1]
    nsel = seq // SEL_BLOCK
    nselp = emap.shape[1]
    wide = N_HG * Q_TILE
    n_rounds = max(min(SEL_TOPN, nsel) - 3, 0)
    kern = functools.partial(_nsa_kernel, nsel=nsel, n_rounds=n_rounds)
    return pl.pallas_call(
        kern,
        grid=(batch, nq),
        in_specs=[
            pl.BlockSpec((N_Q, Q_TILE), lambda b, i: (0, b * nq + i)),
            pl.BlockSpec((None, nb, N_KVW), lambda b, i: (b, 0, 0)),
            pl.BlockSpec((None, N_KVW, nb), lambda b, i: (b, 0, 0)),
            pl.BlockSpec((seq, N_KVW), lambda b, i: (b, 0)),
            pl.BlockSpec((seq, nselp), lambda b, i: (0, 0)),
            pl.BlockSpec((seq // LANES, N_KVW, LANES), lambda b, i: (b, 0, 0)),
            pl.BlockSpec((seq, N_KVW), lambda b, i: (b, 0)),
            pl.BlockSpec((seq // LANES, N_KVW, LANES), lambda b, i: (b, 0, 0)),
            pl.BlockSpec((32, Q_TILE), lambda b, i: (0, b * nq + i)),
        ],
        out_specs=pl.BlockSpec((Q_TILE, N_Q), lambda b, i: (b * nq + i, 0)),
        out_shape=jax.ShapeDtypeStruct((t, N_Q), F32),
        scratch_shapes=[
            pltpu.VMEM((2 * N_DH + nselp, N_KV * wide), BF16),
            pltpu.VMEM((N_KV, Q_TILE // LANES, nb + 8, LANES), F32),
            pltpu.VMEM((N_KV, N_DH, wide), F32),
            pltpu.VMEM((N_KV, N_DH, wide), F32),
            pltpu.VMEM((N_KV, KEY_TILE, wide), F32),
            pltpu.VMEM((N_KV, KEY_TILE, wide), F32),
            pltpu.VMEM((N_KV, 1, wide), F32),
            pltpu.VMEM((N_KV, 1, wide), F32),
            pltpu.VMEM((N_KV, 1, wide), F32),
            pltpu.VMEM((N_KV, N_DH + 16, wide), F32),
        ],
        compiler_params=_params(("arbitrary", "arbitrary")),
        name="nsa",
    )(qT, kcmp, vcmpT, ks, emap, vsT, kw, vwT, smallT)


def _merge_kernel(x_ref, n1w_ref, wg_ref, gb_ref, hm_ref, on_ref, wm_ref, wn_ref, wo_ref, o_ref):
    x = x_ref[...]
    hn = _rmsnorm_rows(x, n1w_ref[...]).astype(BF16)
    gm = _sigmoid(_dot(hn, wg_ref[:, 0:D_MODEL]) + gb_ref[0:1, :])
    gn = _sigmoid(_dot(hn, wg_ref[:, D_MODEL:]) + gb_ref[1:2, :])
    y = gm * _dot(hm_ref[...].astype(BF16), wm_ref[...]) + gn * _dot(on_ref[...].astype(BF16), wn_ref[...])
    o_ref[...] = x + _dot(y.astype(BF16), wo_ref[...])


def _merge(x2, n1w, wg, gb, hm, on, wm, wn, wo, tm):
    t = x2.shape[0]
    const = lambda i: (0, 0)
    return pl.pallas_call(
        _merge_kernel,
        grid=(t // tm,),
        in_specs=[
            pl.BlockSpec((tm, D_MODEL), lambda i: (i, 0)),
            pl.BlockSpec((1, D_MODEL), const),
            pl.BlockSpec((D_MODEL, 2 * D_MODEL), const),
            pl.BlockSpec((2, D_MODEL), const),
            pl.BlockSpec((tm, M_V), lambda i: (i, 0)),
            pl.BlockSpec((tm, N_Q), lambda i: (i, 0)),
            pl.BlockSpec((M_V, D_MODEL), const),
            pl.BlockSpec((N_Q, D_MODEL), const),
            pl.BlockSpec((D_MODEL, D_MODEL), const),
        ],
        out_specs=pl.BlockSpec((tm, D_MODEL), lambda i: (i, 0)),
        out_shape=jax.ShapeDtypeStruct((t, D_MODEL), F32),
        compiler_params=_params(("arbitrary",)),
        name="merge",
    )(x2, n1w, wg, gb, hm, on, wm, wn, wo)


def _ffn_kernel(x_ref, n2w_ref, wup_ref, cw_ref, cb_ref, wdn_ref, o_ref, buf_ref, *, tiles_per_seq):
    i = pl.program_id(0)
    tm = x_ref.shape[0]
    x = x_ref[...]
    hn = _rmsnorm_rows(x, n2w_ref[...]).astype(BF16)

    @pl.when(i % tiles_per_seq == 0)
    def _():
        buf_ref[0:8, :] = jnp.zeros((8, D_FF), F32)

    a = _dot(hn, wup_ref[:, 0:D_FF])
    buf_ref[8:8 + tm, :] = a
    acc = cb_ref[...] + cw_ref[FFN_CONV - 1:FFN_CONV, :] * a
    for k in range(FFN_CONV - 1):
        acc = acc + cw_ref[k:k + 1, :] * buf_ref[8 - (FFN_CONV - 1) + k:8 - (FFN_CONV - 1) + k + tm, :]
    buf_ref[0:8, :] = buf_ref[tm:tm + 8, :]
    v = _dot(hn, wup_ref[:, D_FF:])
    o_ref[...] = x + _dot((_gelu(acc) * v).astype(BF16), wdn_ref[...])


def _ffn(x2, n2w, wup, cw, cb, wdn, seq, tm):
    t = x2.shape[0]
    const = lambda i: (0, 0)
    kern = functools.partial(_ffn_kernel, tiles_per_seq=seq // tm)
    return pl.pallas_call(
        kern,
        grid=(t // tm,),
        in_specs=[
            pl.BlockSpec((tm, D_MODEL), lambda i: (i, 0)),
            pl.BlockSpec((1, D_MODEL), const),
            pl.BlockSpec((D_MODEL, 2 * D_FF), const, pipeline_mode=pl.Buffered(1)),
            pl.BlockSpec((FFN_CONV, D_FF), const),
            pl.BlockSpec((1, D_FF), const),
            pl.BlockSpec((D_FF, D_MODEL), const, pipeline_mode=pl.Buffered(1)),
        ],
        out_specs=pl.BlockSpec((tm, D_MODEL), lambda i: (i, 0)),
        out_shape=jax.ShapeDtypeStruct((t, D_MODEL), F32),
        scratch_shapes=[pltpu.VMEM((tm + 8, D_FF), F32)],
        compiler_params=_params(("arbitrary",)),
        name="ffn",
    )(x2, n2w, wup, cw, cb, wdn)


def _cols(w, *names):
    return jnp.concatenate([w[:, _OFF[n][0]:_OFF[n][1]] for n in names], axis=1)


def _layer(x, n1w, w_in, m_conv_w, m_conv_b, m_igate_b, m_fgate_b, m_out_norm_w,
           q_norm_w, kcmp_norm_w, kslc_norm_w, kwin_norm_w,
           cmp_k_pe, cmp_k_w1, cmp_k_w2, cmp_v_pe, cmp_v_w1, cmp_v_w2,
           w_up_m, w_up_n, merge_gate_b, w_out, norm2_w, ffn_w_up, ffn_conv_w, ffn_conv_b, ffn_w_down):
    batch, seq, _ = x.shape
    t = batch * seq
    x2 = x.reshape(t, D_MODEL)
    n1w2 = n1w.reshape(1, D_MODEL)
    tm = 256
    tm_proj = 512

    w_m = _cols(w_in, "mq", "mk", "mo").astype(BF16)
    w_mvT = _cols(w_in, "mv", "mi", "mf", "ng").T.astype(BF16)
    w_n = _cols(w_in, "nq", "kc", "vc", "ks", "vs", "kw", "vw").astype(BF16)
    w_g = _cols(w_in, "gm", "gn").astype(BF16)
    sbias = jnp.concatenate([m_igate_b, m_fgate_b, jnp.zeros((3 * N_HEADS,), F32)]).reshape(32, 1)

    half = N_DH // 2
    pos = jnp.arange(seq, dtype=F32)
    inv = ROPE_THETA ** (-jnp.arange(0, N_DH, 2, dtype=F32) / N_DH)
    ang = pos[:, None] * inv[None, :]
    cos, sin = jnp.cos(ang), jnp.sin(ang)
    cosn = jnp.tile(cos, (1, N_KVW // half))
    sinn = jnp.tile(jnp.concatenate([-sin, sin], axis=1), (1, N_KV))
    cosT, sinT = cos.T, sin.T
    knw = jnp.zeros((8, N_KVW), F32).at[0:3].set(
        jnp.stack([jnp.tile(w, N_KV) for w in (kcmp_norm_w, kslc_norm_w, kwin_norm_w)]))

    qT_m, k_m, vT_m, og, small, smallT = _proj_m(
        x2, n1w2, w_m, w_mvT, m_conv_w, m_conv_b.reshape(1, -1), sbias, seq, tm_proj)
    qT, kvb, ks, kw, vsT, vwT = _proj_n(
        x2, n1w2, w_n, q_norm_w.reshape(N_DH, 1), knw, cosn, sinn, cosT, sinT, seq, tm_proj)

    nb = seq // CMP_STRIDE
    xkv = kvb.reshape(2 * N_KV, batch, nb, CMP_STRIDE * N_DH)
    pe = jnp.stack([cmp_k_pe.reshape(1, -1), cmp_v_pe.reshape(1, -1)])
    w1 = jnp.stack([cmp_k_w1, cmp_v_w1]).astype(BF16)
    w2 = jnp.stack([cmp_k_w2, cmp_v_w2]).astype(BF16)
    kcmp, vcmpT = _compress(xkv, pe, w1, w2)

    onw = jnp.zeros((8, M_DV), F32).at[0:M_HEADS].set(m_out_norm_w)
    hm = _mlstm(qT_m, k_m, vT_m, small, smallT, og, onw, batch, seq, 4 * MLSTM_CHUNK)
    nselp = -(-(seq // SEL_BLOCK) // LANES) * LANES
    emap = (np.arange(seq)[:, None] // SEL_BLOCK == np.arange(nselp)[None, :]).astype(np.float32)
    on = _nsa(qT, kcmp, vcmpT, ks, jnp.asarray(emap, dtype=BF16), vsT, kw, vwT, smallT, batch, seq)

    x1 = _merge(x2, n1w2, w_g, merge_gate_b, hm, on, w_up_m.astype(BF16), w_up_n.astype(BF16),
                w_out.astype(BF16), tm_proj)
    out = _ffn(x1, norm2_w.reshape(1, D_MODEL), ffn_w_up.astype(BF16), ffn_conv_w, ffn_conv_b.reshape(1, -1),
               ffn_w_down.astype(BF16), seq, tm_proj)
    return out.reshape(batch, seq, D_MODEL)


def kernel(x, norm1_w, w_in, m_conv_w, m_conv_b, m_igate_b, m_fgate_b, m_out_norm_w, q_norm_w, kcmp_norm_w,
           kslc_norm_w, kwin_norm_w, cmp_k_pe, cmp_k_w1, cmp_k_w2, cmp_v_pe, cmp_v_w1, cmp_v_w2, w_up_m, w_up_n,
           merge_gate_b, w_out, norm2_w, ffn_w_up, ffn_conv_w, ffn_conv_b, ffn_w_down):
    params = (norm1_w, w_in, m_conv_w, m_conv_b, m_igate_b, m_fgate_b, m_out_norm_w, q_norm_w, kcmp_norm_w,
              kslc_norm_w, kwin_norm_w, cmp_k_pe, cmp_k_w1, cmp_k_w2, cmp_v_pe, cmp_v_w1, cmp_v_w2, w_up_m, w_up_n,
              merge_gate_b, w_out, norm2_w, ffn_w_up, ffn_conv_w, ffn_conv_b, ffn_w_down)
    for layer in range(norm1_w.shape[0]):
        x = _layer(x, *[p[layer] for p in params])
    return x
```

```python
import functools
import math

import jax
import jax.numpy as jnp
import numpy as np
from jax import lax
from jax.experimental import pallas as pl
from jax.experimental.pallas import tpu as pltpu

D_MODEL = 1024
EPS = 1e-6
ROPE_THETA = 10000.0
NEG = -1e30
M_HEADS = 4
M_DQK = 64
M_DV = 128
M_CONV = 4
M_QK = M_HEADS * M_DQK
M_V = M_HEADS * M_DV
N_HEADS = 8
N_KV = 2
N_HG = N_HEADS // N_KV
N_DH = 64
N_Q = N_HEADS * N_DH
N_KVW = N_KV * N_DH
CMP_LEN = 32
CMP_STRIDE = 16
CMP_HIDDEN = 256
SEL_BLOCK = 64
SEL_TOPN = 16
WINDOW = 512
D_FF = 2816
FFN_CONV = 3

_OFF = {}
_o = 0
for _name, _size in (("mq", M_QK), ("mk", M_QK), ("mv", M_V), ("mo", M_V), ("mi", M_HEADS), ("mf", M_HEADS),
                     ("nq", N_Q), ("kc", N_KVW), ("vc", N_KVW), ("ks", N_KVW), ("vs", N_KVW), ("kw", N_KVW),
                     ("vw", N_KVW), ("ng", 3 * N_HEADS), ("gm", D_MODEL), ("gn", D_MODEL)):
    _OFF[_name] = (_o, _o + _size)
    _o += _size

LANES = 128
MLSTM_CHUNK = 128
SUB_ROWS = 256
Q_TILE = 128
KEY_TILE = 256
WIN_SPAN = WINDOW + Q_TILE
VMEM_LIMIT = 56 * 1024 * 1024

LOG2E = math.log2(math.e)

F32 = jnp.float32
BF16 = jnp.bfloat16
HIGHEST = lax.Precision.HIGHEST


def _dot(a, b):
    return jnp.dot(a, b, preferred_element_type=F32)


def _dot_nt(a, b):
    return lax.dot_general(a, b, (((1,), (1,)), ((), ())), preferred_element_type=F32)


def _rmsnorm_rows(x, w):
    return x * lax.rsqrt(jnp.mean(x * x, axis=-1, keepdims=True) + EPS) * w


def _sigmoid(x):
    return 1.0 / (1.0 + jnp.exp(-x))


def _gelu(x):
    return 0.5 * x * (1.0 + lax.erf(x * (1.0 / math.sqrt(2.0))))


def _params(sem):
    return pltpu.CompilerParams(dimension_semantics=sem, vmem_limit_bytes=VMEM_LIMIT)


def _proj_m_kernel(x_ref, n1w_ref, w_ref, wvT_ref, cw_ref, cb_ref, sb_ref,
                   qT_ref, k_ref, vT_ref, og_ref, small_ref, smallT_ref, buf_ref, *, tiles_per_seq):
    i = pl.program_id(0)
    tm = x_ref.shape[0]
    sub = SUB_ROWS

    @pl.when(i % tiles_per_seq == 0)
    def _():
        buf_ref[0:8, :] = jnp.zeros((8, 2 * M_QK), F32)

    hns = [_rmsnorm_rows(x_ref[r * sub:(r + 1) * sub, :], n1w_ref[...]).astype(BF16) for r in range(tm // sub)]
    for r in range(tm // sub):
        rows = slice(r * sub, (r + 1) * sub)
        hn = hns[r]
        qk = _dot(hn, w_ref[:, 0:2 * M_QK])
        buf_ref[8:8 + sub, :] = qk
        acc = cb_ref[...] + cw_ref[M_CONV - 1:M_CONV, :] * qk
        for k in range(M_CONV - 1):
            acc = acc + cw_ref[k:k + 1, :] * buf_ref[8 - (M_CONV - 1) + k:8 - (M_CONV - 1) + k + sub, :]
        buf_ref[0:8, :] = buf_ref[sub:sub + 8, :]
        act = acc * _sigmoid(acc)
        qT_ref[:, rows] = (act[:, 0:M_QK] * (M_DQK ** -0.5)).T.astype(BF16)
        k_ref[rows, :] = act[:, M_QK:2 * M_QK].astype(BF16)
        vs = _dot_nt(wvT_ref[...], hn)
        vT_ref[:, rows] = vs[0:M_V, :].astype(BF16)
        og_ref[rows, :] = _sigmoid(_dot(hn, w_ref[:, 2 * M_QK:2 * M_QK + M_V]))
        smT = vs[M_V:, :] + sb_ref[...]
        rowi = lax.broadcasted_iota(jnp.int32, smT.shape, 0)
        logsig = jnp.minimum(smT, 0.0) - jnp.log1p(jnp.exp(-jnp.abs(smT)))
        smT = jnp.where(rowi < M_HEADS, smT, jnp.where(rowi < 2 * M_HEADS, logsig, _sigmoid(smT)))
        smallT_ref[:, rows] = smT
        small_ref[rows, :] = jnp.concatenate([smT, jnp.zeros((LANES - 32, sub), F32)], axis=0).T


def _proj_m(x2, n1w, w, wvT, cw, cb, sb, seq, tm):
    t = x2.shape[0]
    ncol = w.shape[1]
    kern = functools.partial(_proj_m_kernel, tiles_per_seq=seq // tm)
    return pl.pallas_call(
        kern,
        grid=(t // tm,),
        in_specs=[
            pl.BlockSpec((tm, D_MODEL), lambda i: (i, 0)),
            pl.BlockSpec((1, D_MODEL), lambda i: (0, 0)),
            pl.BlockSpec((D_MODEL, ncol), lambda i: (0, 0)),
            pl.BlockSpec((M_V + 32, D_MODEL), lambda i: (0, 0)),
            pl.BlockSpec((M_CONV, 2 * M_QK), lambda i: (0, 0)),
            pl.BlockSpec((1, 2 * M_QK), lambda i: (0, 0)),
            pl.BlockSpec((32, 1), lambda i: (0, 0)),
        ],
        out_specs=[
            pl.BlockSpec((M_QK, tm), lambda i: (0, i)),
            pl.BlockSpec((tm, M_QK), lambda i: (i, 0)),
            pl.BlockSpec((M_V, tm), lambda i: (0, i)),
            pl.BlockSpec((tm, M_V), lambda i: (i, 0)),
            pl.BlockSpec((tm, LANES), lambda i: (i, 0)),
            pl.BlockSpec((32, tm), lambda i: (0, i)),
        ],
        out_shape=[
            jax.ShapeDtypeStruct((M_QK, t), BF16),
            jax.ShapeDtypeStruct((t, M_QK), BF16),
            jax.ShapeDtypeStruct((M_V, t), BF16),
            jax.ShapeDtypeStruct((t, M_V), F32),
            jax.ShapeDtypeStruct((t, LANES), F32),
            jax.ShapeDtypeStruct((32, t), F32),
        ],
        scratch_shapes=[pltpu.VMEM((SUB_ROWS + 8, 2 * M_QK), F32)],
        compiler_params=_params(("arbitrary",)),
        name="proj_m",
    )(x2, n1w, w, wvT, cw, cb, sb)


def _proj_n_kernel(x_ref, n1w_ref, w_ref, wT_ref, qnw_ref, knw_ref, cosn_ref, sinn_ref, cosT_ref, sinT_ref,
                   qT_ref, kvb_ref, ks_ref, kw_ref, vsT_ref, vwT_ref, regroup_ref):
    tm = x_ref.shape[0]
    hn = _rmsnorm_rows(x_ref[...], n1w_ref[...]).astype(BF16)
    fm = _dot_nt(wT_ref[...], hn)
    qT = fm[0:N_Q, :]
    cosT = cosT_ref[...]
    sinT = sinT_ref[...]
    qnw = qnw_ref[...]
    half = N_DH // 2
    for h in range(N_HEADS):
        xh = qT[h * N_DH:(h + 1) * N_DH, :]
        xn = xh * lax.rsqrt(jnp.mean(xh * xh, axis=0, keepdims=True) + EPS) * qnw
        x1 = xn[0:half, :]
        x2 = xn[half:, :]
        o = jnp.concatenate([x1 * cosT - x2 * sinT, x2 * cosT + x1 * sinT], axis=0) * (LOG2E * N_DH ** -0.5)
        qT_ref[h * N_DH:(h + 1) * N_DH, :] = o.astype(BF16)

    cosn = cosn_ref[...]
    sinn = sinn_ref[...]
    li = lax.broadcasted_iota(jnp.int32, (N_KVW, N_KVW), 0)
    lj = lax.broadcasted_iota(jnp.int32, (N_KVW, N_KVW), 1)
    head_sum = (li // N_DH == lj // N_DH).astype(BF16)
    swap_half = (lj == li + jnp.where(li % N_DH < half, half, -half)).astype(BF16)
    head_sum = jnp.concatenate([head_sum, head_sum], axis=0)
    swap_half = jnp.concatenate([swap_half, swap_half], axis=0)

    def lane_map(v, m01x2):
        hi = v.astype(BF16)
        lo = (v - hi.astype(F32)).astype(BF16)
        return _dot(jnp.concatenate([hi, lo], axis=1), m01x2)

    k_raw = [_dot(hn, w_ref[:, i * N_KVW:(i + 1) * N_KVW]) for i in range(3)]
    vc = _dot(hn, w_ref[:, 3 * N_KVW:])
    ms = [lane_map(k * k, head_sum) * (1.0 / N_DH) for k in k_raw]
    kn = [k_raw[i] * lax.rsqrt(ms[i] + EPS) * knw_ref[i:i + 1, :] for i in range(3)]
    kc, ks, kw = [kn[i] * cosn + lane_map(kn[i], swap_half) * sinn for i in range(3)]
    lane_kv = lax.broadcasted_iota(jnp.int32, (tm // CMP_STRIDE, N_KVW), 1)
    for a, arr in enumerate((kc, vc)):
        regroup_ref[...] = arr
        for l in range(0, CMP_STRIDE, 2):
            even = regroup_ref[pl.ds(l, tm // CMP_STRIDE, stride=CMP_STRIDE), :]
            odd = regroup_ref[pl.ds(l + 1, tm // CMP_STRIDE, stride=CMP_STRIDE), :]
            cols = slice((l // 2) * N_KVW, (l // 2 + 1) * N_KVW)
            kvb_ref[a * N_KV, :, cols] = jnp.where(lane_kv < N_DH, even, pltpu.roll(odd, N_DH, 1))
            kvb_ref[a * N_KV + 1, :, cols] = jnp.where(lane_kv < N_DH, pltpu.roll(even, N_DH, 1), odd)
    ks_ref[...] = ks.astype(BF16)
    kw_ref[...] = kw.astype(BF16)
    vsT = fm[N_Q:N_Q + N_KVW, :].astype(BF16)
    vwT = fm[N_Q + N_KVW:, :].astype(BF16)
    for j in range(tm // LANES):
        vsT_ref[j] = vsT[:, j * LANES:(j + 1) * LANES]
        vwT_ref[j] = vwT[:, j * LANES:(j + 1) * LANES]


def _proj_n(x2, n1w, w, wT, qnw, knw, cosn, sinn, cosT, sinT, seq, tm):
    t = x2.shape[0]
    ncol = w.shape[1]
    tps = seq // tm
    half = N_DH // 2
    return pl.pallas_call(
        _proj_n_kernel,
        grid=(t // tm,),
        in_specs=[
            pl.BlockSpec((tm, D_MODEL), lambda i: (i, 0)),
            pl.BlockSpec((1, D_MODEL), lambda i: (0, 0)),
            pl.BlockSpec((D_MODEL, ncol), lambda i: (0, 0)),
            pl.BlockSpec((N_Q + 2 * N_KVW, D_MODEL), lambda i: (0, 0)),
            pl.BlockSpec((N_DH, 1), lambda i: (0, 0)),
            pl.BlockSpec((8, N_KVW), lambda i: (0, 0)),
            pl.BlockSpec((tm, N_KVW), lambda i: (i % tps, 0)),
            pl.BlockSpec((tm, N_KVW), lambda i: (i % tps, 0)),
            pl.BlockSpec((half, tm), lambda i: (0, i % tps)),
            pl.BlockSpec((half, tm), lambda i: (0, i % tps)),
        ],
        out_specs=[
            pl.BlockSpec((N_Q, tm), lambda i: (0, i)),
            pl.BlockSpec((2 * N_KV, tm // CMP_STRIDE, CMP_STRIDE * N_DH), lambda i: (0, i, 0)),
            pl.BlockSpec((tm, N_KVW), lambda i: (i, 0)),
            pl.BlockSpec((tm, N_KVW), lambda i: (i, 0)),
            pl.BlockSpec((tm // LANES, N_KVW, LANES), lambda i: (i, 0, 0)),
            pl.BlockSpec((tm // LANES, N_KVW, LANES), lambda i: (i, 0, 0)),
        ],
        out_shape=[
            jax.ShapeDtypeStruct((N_Q, t), BF16),
            jax.ShapeDtypeStruct((2 * N_KV, t // CMP_STRIDE, CMP_STRIDE * N_DH), F32),
            jax.ShapeDtypeStruct((t, N_KVW), BF16),
            jax.ShapeDtypeStruct((t, N_KVW), BF16),
            jax.ShapeDtypeStruct((t // LANES, N_KVW, LANES), BF16),
            jax.ShapeDtypeStruct((t // LANES, N_KVW, LANES), BF16),
        ],
        scratch_shapes=[pltpu.VMEM((tm, N_KVW), F32)],
        compiler_params=_params(("arbitrary",)),
        name="proj_n",
    )(x2, n1w, w, wT, qnw, knw, cosn, sinn, cosT, sinT)


def _compress_kernel(x_ref, pe_ref, w1_ref, w2_ref, kcmp_ref, vcmpT_ref):
    nb = x_ref.shape[2]
    half = (CMP_LEN // 2) * N_DH

    def mlp(a, kv):
        pe = pe_ref[kv]
        x = x_ref[a, 0]
        first = _dot((x + pe[:, 0:half]).astype(BF16), w1_ref[kv, 0:half, :])
        second = _dot((x + pe[:, half:]).astype(BF16), w1_ref[kv, half:, :])
        hid = first + pltpu.roll(second, nb - 1, 0)
        return _dot(_gelu(hid).astype(BF16), w2_ref[kv])

    kcmp_ref[...] = jnp.concatenate([mlp(g, 0) for g in range(N_KV)], axis=1).astype(BF16)
    vcmpT_ref[...] = jnp.concatenate([mlp(N_KV + g, 1) for g in range(N_KV)], axis=1).T.astype(BF16)


def _compress(xkv, pe, w1, w2):
    na, batch, nb, width = xkv.shape
    const3 = lambda b: (0, 0, 0)
    return pl.pallas_call(
        _compress_kernel,
        grid=(batch,),
        in_specs=[
            pl.BlockSpec((na, 1, nb, width), lambda b: (0, b, 0, 0)),
            pl.BlockSpec((2, 1, CMP_LEN * N_DH), const3),
            pl.BlockSpec((2, CMP_LEN * N_DH, CMP_HIDDEN), const3),
            pl.BlockSpec((2, CMP_HIDDEN, N_DH), const3),
        ],
        out_specs=[
            pl.BlockSpec((None, nb, N_KVW), lambda b: (b, 0, 0)),
            pl.BlockSpec((None, N_KVW, nb), lambda b: (b, 0, 0)),
        ],
        out_shape=[
            jax.ShapeDtypeStruct((batch, nb, N_KVW), BF16),
            jax.ShapeDtypeStruct((batch, N_KVW, nb), BF16),
        ],
        compiler_params=_params(("arbitrary",)),
        name="compress",
    )(xkv, pe, w1, w2)


def _mlstm_kernel(qT_ref, k_ref, vT_ref, small_ref, smallT_ref, og_ref, onw_ref, o_ref, c_ref, m_ref):
    L = MLSTM_CHUNK
    tb = k_ref.shape[0]

    @pl.when(pl.program_id(1) == 0)
    def _():
        c_ref[...] = jnp.zeros(c_ref.shape, F32)
        m_ref[...] = jnp.zeros(m_ref.shape, F32)

    row = lax.broadcasted_iota(jnp.int32, (L, L), 0)
    col = lax.broadcasted_iota(jnp.int32, (L, L), 1)
    causal = row <= col
    tril = (col <= row).astype(F32)
    triu = causal.astype(F32)
    ones_rows = jnp.ones((16, L), BF16)
    zeros_q = jnp.zeros((M_DQK, L), BF16)

    heads = range(M_HEADS)
    chunks = range(tb // L)
    sls = [slice(c * L, (c + 1) * L) for c in chunks]
    sms = [small_ref[sl, :] for sl in sls]
    bcols = [jnp.dot(tril, sm, preferred_element_type=F32, precision=HIGHEST) for sm in sms]
    brows = [jnp.dot(smallT_ref[0:8, sl], triu, preferred_element_type=F32, precision=HIGHEST) for sl in sls]
    c_state = [c_ref[h] for h in heads]
    m_state = [m_ref[h:h + 1, 0:1] for h in heads]
    for c in chunks:
        sl = sls[c]
        k_pairs = [k_ref[sl, p * LANES:(p + 1) * LANES] for p in range(M_HEADS // 2)]
        qT_pad, s, qc, vT_aug = [], [], [], []
        for h in heads:
            qT_h = qT_ref[h * M_DQK:(h + 1) * M_DQK, sl]
            qT_pad.append(jnp.concatenate([qT_h, zeros_q] if h % 2 == 0 else [zeros_q, qT_h], axis=0))
            s.append(_dot(k_pairs[h // 2], qT_pad[h]))
            qc.append(_dot(c_state[h].astype(BF16), qT_pad[h]))
            vT_aug.append(jnp.concatenate([vT_ref[h * M_DV:(h + 1) * M_DV, sl], ones_rows], axis=0))
        m_t, isc, sw, kw, m_new, decay, scale = [], [], [], [], [], [], []
        for h in heads:
            in_head = (col // M_DQK) == (h % 2)
            b_col = bcols[c][:, M_HEADS + h:M_HEADS + h + 1]
            i_col = sms[c][:, h:h + 1]
            b_row = brows[c][M_HEADS + h:M_HEADS + h + 1, :]
            g = b_row[:, L - 1:L]
            d = jnp.where(causal, b_row + (i_col - b_col), NEG)
            inter = b_row + m_state[h]
            m_t.append(jnp.maximum(inter, jnp.max(d, axis=0, keepdims=True)))
            isc.append(jnp.exp(inter - m_t[h]))
            sw.append((s[h] * jnp.exp(d - m_t[h])).astype(BF16))
            a_col = g - b_col + i_col
            a_max = jnp.max(a_col, axis=0, keepdims=True)
            kw.append(jnp.where(in_head, k_pairs[h // 2].astype(F32) * jnp.exp(a_col - a_max), 0.0).astype(BF16))
            m_new.append(jnp.maximum(g + m_state[h], a_max))
            decay.append(jnp.exp(g + m_state[h] - m_new[h]))
            scale.append(jnp.exp(a_max - m_new[h]))
        sv = [_dot(vT_aug[h], sw[h]) for h in heads]
        dc = [_dot(vT_aug[h], kw[h]) for h in heads]
        for h in heads:
            nd = isc[h] * qc[h] + sv[h]
            den = nd[M_DV:M_DV + 1, :]
            hh = nd[0:M_DV, :] / jnp.maximum(jnp.abs(den), jnp.exp(-m_t[h]))
            hn = hh * lax.rsqrt(jnp.mean(hh * hh, axis=0, keepdims=True) + EPS)
            o_ref[sl, h * M_DV:(h + 1) * M_DV] = hn.T * onw_ref[h:h + 1, :] * og_ref[sl, h * M_DV:(h + 1) * M_DV]
            c_state[h] = decay[h] * c_state[h] + scale[h] * dc[h]
            m_state[h] = m_new[h]
    for h in heads:
        c_ref[h] = c_state[h]
        m_ref[h:h + 1, :] = jnp.broadcast_to(m_state[h], (1, LANES))


def _mlstm(qT, k, vT, small, smallT, og, onw, batch, seq, tb):
    t = k.shape[0]
    nblk = seq // tb
    return pl.pallas_call(
        _mlstm_kernel,
        grid=(batch, nblk),
        in_specs=[
            pl.BlockSpec((M_QK, tb), lambda b, j: (0, b * nblk + j)),
            pl.BlockSpec((tb, M_QK), lambda b, j: (b * nblk + j, 0)),
            pl.BlockSpec((M_V, tb), lambda b, j: (0, b * nblk + j)),
            pl.BlockSpec((tb, LANES), lambda b, j: (b * nblk + j, 0)),
            pl.BlockSpec((32, tb), lambda b, j: (0, b * nblk + j)),
            pl.BlockSpec((tb, M_V), lambda b, j: (b * nblk + j, 0)),
            pl.BlockSpec((8, M_DV), lambda b, j: (0, 0)),
        ],
        out_specs=pl.BlockSpec((tb, M_V), lambda b, j: (b * nblk + j, 0)),
        out_shape=jax.ShapeDtypeStruct((t, M_V), F32),
        scratch_shapes=[pltpu.VMEM((M_HEADS, M_DV + 16, LANES), F32), pltpu.VMEM((8, LANES), F32)],
        compiler_params=_params(("arbitrary", "arbitrary")),
        name="mlstm",
    )(qT, k, vT, small, smallT, og, onw)


def _nsa_kernel(qT_ref, kcmp_ref, vcmpT_ref, ks_ref, e_ref, vsT_ref, kw_ref, vwT_ref, gT_ref,
                o_ref, rhs_ref, ps_ref, ocmp_ref, owin_ref, s0_ref, s1_ref, c0_ref, c1_ref, m_ref, acc_ref,
                *, nsel, n_rounds):
    qi = pl.program_id(1)
    tq = Q_TILE
    nb = kcmp_ref.shape[0]
    nselp = e_ref.shape[1]
    q0 = qi * tq
    wide = N_HG * tq
    lane_w = lax.broadcasted_iota(jnp.int32, (1, wide), 1)
    tpos_w = q0 + (lane_w % tq)
    tpos = q0 + lax.broadcasted_iota(jnp.int32, (1, tq), 1)
    zeros_q = jnp.zeros((N_DH, wide), BF16)
    ones_v = jnp.ones((16, LANES), BF16)
    grows = [slice(g * N_DH, (g + 1) * N_DH) for g in range(N_KV)]
    gcols = [slice(g * wide, (g + 1) * wide) for g in range(N_KV)]

    def values(v_ref, first, count, rows):
        return jnp.concatenate(
            [jnp.concatenate([v_ref[first + j, rows, :], ones_v], axis=0) for j in range(count)], axis=1)

    start_w = pl.multiple_of(jnp.maximum(q0 - WINDOW, 0), LANES)
    qpads = []
    for g in range(N_KV):
        q4 = jnp.concatenate(
            [qT_ref[(g * N_HG + h) * N_DH:(g * N_HG + h + 1) * N_DH, :] for h in range(N_HG)], axis=1)
        qpads.append(jnp.concatenate([q4, zeros_q] if g == 0 else [zeros_q, q4], axis=0))
        rhs_ref[0:2 * N_DH, gcols[g]] = qpads[g]

    def scores(kt, g):
        start = pl.multiple_of(kt * KEY_TILE, KEY_TILE)
        lhs = jnp.concatenate([ks_ref[pl.ds(start, KEY_TILE), :], e_ref[pl.ds(start, KEY_TILE), :]], axis=1)
        return _dot(lhs, rhs_ref[:, g * wide:(g + 1) * wide])

    def produce(kt, s_ref, c_ref):
        for g in range(N_KV):
            s = scores(kt, g)
            s_ref[g] = s
            c_ref[g] = jnp.max(s, axis=0, keepdims=True)

    def consume(kt, s_ref, c_ref, causal_rows=0):
        for g in range(N_KV):
            if causal_rows:
                kpos = kt * KEY_TILE + lax.broadcasted_iota(jnp.int32, (causal_rows, 1), 0)
                s = jnp.where(kpos <= tpos_w, s_ref[g, 0:causal_rows, :], NEG)
                smax = jnp.max(s, axis=0, keepdims=True)
            else:
                s = s_ref[g]
                smax = c_ref[g]
            m = m_ref[g]
            m_new = jnp.maximum(m, smax)
            p = jnp.exp2(s - m_new).astype(BF16)
            vt = values(vsT_ref, kt * (KEY_TILE // LANES), (causal_rows or KEY_TILE) // LANES,
                        slice(g * N_DH, (g + 1) * N_DH))
            acc_ref[g] = jnp.exp2(m - m_new) * acc_ref[g] + _dot(vt, p)
            m_ref[g] = m_new

    ratio = SEL_BLOCK // CMP_STRIDE

    def front(nrows, first_q0):
        nbv = nrows * ratio
        free = max((first_q0 - (CMP_LEN - 1)) // CMP_STRIDE + 1, 0) // 8 * 8
        cend = (free + lax.broadcasted_iota(jnp.int32, (nbv - free, 1), 0)) * CMP_STRIDE + (CMP_LEN - 1)
        cmask = cend <= tpos_w
        any_visible = tpos_w >= CMP_LEN - 1
        sc = [_dot(kcmp_ref[0:nbv, :], qpads[g]) for g in range(N_KV)]
        sw = [_dot(kw_ref[pl.ds(start_w, WIN_SPAN), :], qpads[g]) for g in range(N_KV)]
        imp = []
        for g in range(N_KV):
            s = jnp.where(cmask, sc[g][free:, :], NEG)
            if free:
                s = jnp.concatenate([sc[g][0:free, :], s], axis=0)
            pc = jnp.exp2(s - jnp.max(s, axis=0, keepdims=True))
            lc = jnp.sum(pc, axis=0, keepdims=True)
            pc = pc * jnp.where(any_visible, 1.0 / lc, 0.0)
            ocmp_ref[g] = _dot(vcmpT_ref[grows[g], 0:nbv], pc.astype(BF16))
            psum = pc[:, 0:tq]
            for h in range(1, N_HG):
                psum = psum + pc[:, h * tq:(h + 1) * tq]
            parts = []
            for cchunk in range(tq // LANES):
                ps_ref[g, cchunk, 0:8, :] = jnp.zeros((8, LANES), F32)
                ps_ref[g, cchunk, 8:8 + nbv, :] = psum[:, cchunk * LANES:(cchunk + 1) * LANES]
                acc = None
                for k in range(-((CMP_LEN - 1) // CMP_STRIDE), ratio):
                    part = ps_ref[g, cchunk, pl.ds(8 + k, nrows, stride=ratio), :]
                    acc = part if acc is None else acc + part
                parts.append(acc)
            imp.append(jnp.concatenate(parts, axis=1))

        jblk = lax.broadcasted_iota(jnp.int32, (nrows, tq), 0)
        cur = tpos // SEL_BLOCK
        forced = (jblk == 0) | (jblk == cur) | (jblk == cur - 1)
        cand = (jblk >= 1) & (jblk <= cur - 2)
        jblk_f = jblk.astype(F32)
        val = [jnp.where(cand, imp[g], -jnp.inf) for g in range(N_KV)]
        for _ in range(n_rounds):
            for g in range(N_KV):
                mx = jnp.max(val[g], axis=0, keepdims=True)
                first = jnp.min(jnp.where(val[g] == mx, jblk_f, float(nrows)), axis=0, keepdims=True)
                val[g] = jnp.where(jblk_f == first, -jnp.inf, val[g])
        for g in range(N_KV):
            picked = cand & (val[g] == -jnp.inf)
            bias = jnp.where(forced | picked, 0.0, NEG).astype(BF16)
            if nselp > nrows:
                bias = jnp.concatenate([bias, jnp.zeros((nselp - nrows, tq), BF16)], axis=0)
            rhs_ref[2 * N_DH:, gcols[g]] = jnp.concatenate([bias] * N_HG, axis=1)

        produce(0, s0_ref, c0_ref)

        def window_mask(s, r0, r1):
            dist = (tpos_w - start_w) - (r0 + lax.broadcasted_iota(jnp.int32, (r1 - r0, 1), 0))
            return jnp.where(lax.bitcast_convert_type(dist, jnp.uint32) < WINDOW, s[r0:r1, :], NEG)

        for g in range(N_KV):
            if first_q0 >= WINDOW:
                s = jnp.concatenate([window_mask(sw[g], 0, tq), sw[g][tq:WINDOW, :],
                                     window_mask(sw[g], WINDOW, WIN_SPAN)], axis=0)
            else:
                s = window_mask(sw[g], 0, WIN_SPAN)
            pw = jnp.exp2(s - jnp.max(s, axis=0, keepdims=True))
            ow = _dot(values(vwT_ref, start_w // LANES, WIN_SPAN // LANES, grows[g]), pw.astype(BF16))
            owin_ref[g] = ow[0:N_DH, :] * (1.0 / ow[N_DH:N_DH + 1, :])

    n_full = qi // (KEY_TILE // tq)
    odd = n_full % 2
    m_ref[...] = jnp.full(m_ref.shape, NEG, F32)
    acc_ref[...] = jnp.zeros(acc_ref.shape, F32)
    n_var = 4 if nsel % 32 == 0 else 1
    if n_var == 1:
        front(nsel, 0)
    else:
        tiles_per_var = (nsel * SEL_BLOCK // tq) // n_var
        for v in range(n_var):
            pl.when(qi // tiles_per_var == v)(
                functools.partial(front, (v + 1) * nsel // n_var, v * tiles_per_var * tq))

    def pair(kt):
        produce(kt + 1, s1_ref, c1_ref)
        consume(kt, s0_ref, c0_ref)
        produce(kt + 2, s0_ref, c0_ref)
        consume(kt + 1, s1_ref, c1_ref)

    def quad(j, _):
        pair(4 * j)
        pair(4 * j + 2)
        return 0

    n_quad = n_full // 4
    lax.fori_loop(0, n_quad, quad, 0)

    @pl.when(n_full - 4 * n_quad >= 2)
    def _():
        pair(4 * n_quad)

    for at_start in (False, True):
        rows = tq if at_start else KEY_TILE
        starts_tile = (q0 % KEY_TILE == 0) == at_start

        @pl.when((odd == 0) & starts_tile)
        def _():
            consume(n_full, s0_ref, c0_ref, causal_rows=rows)

        @pl.when((odd == 1) & starts_tile)
        def _():
            produce(n_full, s1_ref, c1_ref)
            consume(n_full - 1, s0_ref, c0_ref)
            consume(n_full, s1_ref, c1_ref, causal_rows=rows)

    outs = []
    for g in range(N_KV):
        o_slc = acc_ref[g, 0:N_DH, :] * (1.0 / acc_ref[g, N_DH:N_DH + 1, :])
        for h in range(N_HG):
            cs = slice(h * tq, (h + 1) * tq)
            r = 2 * M_HEADS + (g * N_HG + h) * 3
            outs.append(gT_ref[r:r + 1, :] * ocmp_ref[g, :, cs] + gT_ref[r + 1:r + 2, :] * o_slc[:, cs]
                        + gT_ref[r + 2:r + 3, :] * owin_ref[g, :, cs])

    o_ref[...] = jnp.concatenate(outs, axis=0).T


def _nsa(qT, kcmp, vcmpT, ks, emap, vsT, kw, vwT, smallT, batch, seq):
    t = qT.shape[1]
    nq = seq // Q_TILE
    nb = kcmp.shape[1]
    nsel = seq // SEL_BLOCK
    nselp = emap.shape[1]
    wide = N_HG * Q_TILE
    n_rounds = max(min(SEL_TOPN, nsel) - 3, 0)
    kern = functools.partial(_nsa_kernel, nsel=nsel, n_rounds=n_rounds)
    return pl.pallas_call(
        kern,
        grid=(batch, nq),
        in_specs=[
            pl.BlockSpec((N_Q, Q_TILE), lambda b, i: (0, b * nq + i)),
            pl.BlockSpec((None, nb, N_KVW), lambda b, i: (b, 0, 0)),
            pl.BlockSpec((None, N_KVW, nb), lambda b, i: (b, 0, 0)),
            pl.BlockSpec((seq, N_KVW), lambda b, i: (b, 0)),
            pl.BlockSpec((seq, nselp), lambda b, i: (0, 0)),
            pl.BlockSpec((seq // LANES, N_KVW, LANES), lambda b, i: (b, 0, 0)),
            pl.BlockSpec((seq, N_KVW), lambda b, i: (b, 0)),
            pl.BlockSpec((seq // LANES, N_KVW, LANES), lambda b, i: (b, 0, 0)),
            pl.BlockSpec((32, Q_TILE), lambda b, i: (0, b * nq + i)),
        ],
        out_specs=pl.BlockSpec((Q_TILE, N_Q), lambda b, i: (b * nq + i, 0)),
        out_shape=jax.ShapeDtypeStruct((t, N_Q), F32),
        scratch_shapes=[
            pltpu.VMEM((2 * N_DH + nselp, N_KV * wide), BF16),
            pltpu.VMEM((N_KV, Q_TILE // LANES, nb + 8, LANES), F32),
            pltpu.VMEM((N_KV, N_DH, wide), F32),
            pltpu.VMEM((N_KV, N_DH, wide), F32),
            pltpu.VMEM((N_KV, KEY_TILE, wide), F32),
            pltpu.VMEM((N_KV, KEY_TILE, wide), F32),
            pltpu.VMEM((N_KV, 1, wide), F32),
            pltpu.VMEM((N_KV, 1, wide), F32),
            pltpu.VMEM((N_KV, 1, wide), F32),
            pltpu.VMEM((N_KV, N_DH + 16, wide), F32),
        ],
        compiler_params=_params(("arbitrary", "arbitrary")),
        name="nsa",
    )(qT, kcmp, vcmpT, ks, emap, vsT, kw, vwT, smallT)


def _merge_kernel(x_ref, n1w_ref, wg_ref, gb_ref, hm_ref, on_ref, wm_ref, wn_ref, wo_ref, o_ref):
    x = x_ref[...]
    hn = _rmsnorm_rows(x, n1w_ref[...]).astype(BF16)
    gm = _sigmoid(_dot(hn, wg_ref[:, 0:D_MODEL]) + gb_ref[0:1, :])
    gn = _sigmoid(_dot(hn, wg_ref[:, D_MODEL:]) + gb_ref[1:2, :])
    y = gm * _dot(hm_ref[...].astype(BF16), wm_ref[...]) + gn * _dot(on_ref[...].astype(BF16), wn_ref[...])
    o_ref[...] = x + _dot(y.astype(BF16), wo_ref[...])


def _merge(x2, n1w, wg, gb, hm, on, wm, wn, wo, tm):
    t = x2.shape[0]
    const = lambda i: (0, 0)
    return pl.pallas_call(
        _merge_kernel,
        grid=(t // tm,),
        in_specs=[
            pl.BlockSpec((tm, D_MODEL), lambda i: (i, 0)),
            pl.BlockSpec((1, D_MODEL), const),
            pl.BlockSpec((D_MODEL, 2 * D_MODEL), const),
            pl.BlockSpec((2, D_MODEL), const),
            pl.BlockSpec((tm, M_V), lambda i: (i, 0)),
            pl.BlockSpec((tm, N_Q), lambda i: (i, 0)),
            pl.BlockSpec((M_V, D_MODEL), const),
            pl.BlockSpec((N_Q, D_MODEL), const),
            pl.BlockSpec((D_MODEL, D_MODEL), const),
        ],
        out_specs=pl.BlockSpec((tm, D_MODEL), lambda i: (i, 0)),
        out_shape=jax.ShapeDtypeStruct((t, D_MODEL), F32),
        compiler_params=_params(("arbitrary",)),
        name="merge",
    )(x2, n1w, wg, gb, hm, on, wm, wn, wo)


def _ffn_kernel(x_ref, n2w_ref, wup_ref, cw_ref, cb_ref, wdn_ref, o_ref, buf_ref, *, tiles_per_seq):
    i = pl.program_id(0)
    tm = x_ref.shape[0]
    x = x_ref[...]
    hn = _rmsnorm_rows(x, n2w_ref[...]).astype(BF16)

    @pl.when(i % tiles_per_seq == 0)
    def _():
        buf_ref[0:8, :] = jnp.zeros((8, D_FF), F32)

    a = _dot(hn, wup_ref[:, 0:D_FF])
    buf_ref[8:8 + tm, :] = a
    acc = cb_ref[...] + cw_ref[FFN_CONV - 1:FFN_CONV, :] * a
    for k in range(FFN_CONV - 1):
        acc = acc + cw_ref[k:k + 1, :] * buf_ref[8 - (FFN_CONV - 1) + k:8 - (FFN_CONV - 1) + k + tm, :]
    buf_ref[0:8, :] = buf_ref[tm:tm + 8, :]
    v = _dot(hn, wup_ref[:, D_FF:])
    o_ref[...] = x + _dot((_gelu(acc) * v).astype(BF16), wdn_ref[...])


def _ffn(x2, n2w, wup, cw, cb, wdn, seq, tm):
    t = x2.shape[0]
    const = lambda i: (0, 0)
    kern = functools.partial(_ffn_kernel, tiles_per_seq=seq // tm)
    return pl.pallas_call(
        kern,
        grid=(t // tm,),
        in_specs=[
            pl.BlockSpec((tm, D_MODEL), lambda i: (i, 0)),
            pl.BlockSpec((1, D_MODEL), const),
            pl.BlockSpec((D_MODEL, 2 * D_FF), const, pipeline_mode=pl.Buffered(1)),
            pl.BlockSpec((FFN_CONV, D_FF), const),
            pl.BlockSpec((1, D_FF), const),
            pl.BlockSpec((D_FF, D_MODEL), const, pipeline_mode=pl.Buffered(1)),
        ],
        out_specs=pl.BlockSpec((tm, D_MODEL), lambda i: (i, 0)),
        out_shape=jax.ShapeDtypeStruct((t, D_MODEL), F32),
        scratch_shapes=[pltpu.VMEM((tm + 8, D_FF), F32)],
        compiler_params=_params(("arbitrary",)),
        name="ffn",
    )(x2, n2w, wup, cw, cb, wdn)


def _cols(w, *names):
    return jnp.concatenate([w[:, _OFF[n][0]:_OFF[n][1]] for n in names], axis=1)


def _layer(x, n1w, w_in, m_conv_w, m_conv_b, m_igate_b, m_fgate_b, m_out_norm_w,
           q_norm_w, kcmp_norm_w, kslc_norm_w, kwin_norm_w,
           cmp_k_pe, cmp_k_w1, cmp_k_w2, cmp_v_pe, cmp_v_w1, cmp_v_w2,
           w_up_m, w_up_n, merge_gate_b, w_out, norm2_w, ffn_w_up, ffn_conv_w, ffn_conv_b, ffn_w_down):
    batch, seq, _ = x.shape
    t = batch * seq
    x2 = x.reshape(t, D_MODEL)
    n1w2 = n1w.reshape(1, D_MODEL)
    tm = 256
    tm_proj = 512

    w_m = _cols(w_in, "mq", "mk", "mo").astype(BF16)
    w_mvT = _cols(w_in, "mv", "mi", "mf", "ng").T.astype(BF16)
    w_n = _cols(w_in, "kc", "ks", "kw", "vc").astype(BF16)
    w_nT = _cols(w_in, "nq", "vs", "vw").T.astype(BF16)
    w_g = _cols(w_in, "gm", "gn").astype(BF16)
    sbias = jnp.concatenate([m_igate_b, m_fgate_b, jnp.zeros((3 * N_HEADS,), F32)]).reshape(32, 1)

    half = N_DH // 2
    pos = jnp.arange(seq, dtype=F32)
    inv = ROPE_THETA ** (-jnp.arange(0, N_DH, 2, dtype=F32) / N_DH)
    ang = pos[:, None] * inv[None, :]
    cos, sin = jnp.cos(ang), jnp.sin(ang)
    cosn = jnp.tile(cos, (1, N_KVW // half))
    sinn = jnp.tile(jnp.concatenate([-sin, sin], axis=1), (1, N_KV))
    cosT, sinT = cos.T, sin.T
    knw = jnp.zeros((8, N_KVW), F32).at[0:3].set(
        jnp.stack([jnp.tile(w, N_KV) for w in (kcmp_norm_w, kslc_norm_w, kwin_norm_w)]))

    qT_m, k_m, vT_m, og, small, smallT = _proj_m(
        x2, n1w2, w_m, w_mvT, m_conv_w, m_conv_b.reshape(1, -1), sbias, seq, tm_proj)
    qT, kvb, ks, kw, vsT, vwT = _proj_n(
        x2, n1w2, w_n, w_nT, q_norm_w.reshape(N_DH, 1), knw, cosn, sinn, cosT, sinT, seq, tm_proj)

    nb = seq // CMP_STRIDE
    xkv = kvb.reshape(2 * N_KV, batch, nb, CMP_STRIDE * N_DH)
    pe = jnp.stack([cmp_k_pe.reshape(1, -1), cmp_v_pe.reshape(1, -1)])
    w1 = jnp.stack([cmp_k_w1, cmp_v_w1]).astype(BF16)
    w2 = jnp.stack([cmp_k_w2, cmp_v_w2]).astype(BF16)
    kcmp, vcmpT = _compress(xkv, pe, w1, w2)

    onw = jnp.zeros((8, M_DV), F32).at[0:M_HEADS].set(m_out_norm_w)
    hm = _mlstm(qT_m, k_m, vT_m, small, smallT, og, onw, batch, seq, 4 * MLSTM_CHUNK)
    nselp = -(-(seq // SEL_BLOCK) // LANES) * LANES
    emap = (np.arange(seq)[:, None] // SEL_BLOCK == np.arange(nselp)[None, :]).astype(np.float32)
    on = _nsa(qT, kcmp, vcmpT, ks, jnp.asarray(emap, dtype=BF16), vsT, kw, vwT, smallT, batch, seq)

    x1 = _merge(x2, n1w2, w_g, merge_gate_b, hm, on, w_up_m.astype(BF16), w_up_n.astype(BF16),
                w_out.astype(BF16), tm_proj)
    out = _ffn(x1, norm2_w.reshape(1, D_MODEL), ffn_w_up.astype(BF16), ffn_conv_w, ffn_conv_b.reshape(1, -1),
               ffn_w_down.astype(BF16), seq, tm_proj)
    return out.reshape(batch, seq, D_MODEL)


def kernel(x, norm1_w, w_in, m_conv_w, m_conv_b, m_igate_b, m_fgate_b, m_out_norm_w, q_norm_w, kcmp_norm_w,
           kslc_norm_w, kwin_norm_w, cmp_k_pe, cmp_k_w1, cmp_k_w2, cmp_v_pe, cmp_v_w1, cmp_v_w2, w_up_m, w_up_n,
           merge_gate_b, w_out, norm2_w, ffn_w_up, ffn_conv_w, ffn_conv_b, ffn_w_down):
    params = (norm1_w, w_in, m_conv_w, m_conv_b, m_igate_b, m_fgate_b, m_out_norm_w, q_norm_w, kcmp_norm_w,
              kslc_norm_w, kwin_norm_w, cmp_k_pe, cmp_k_w1, cmp_k_w2, cmp_v_pe, cmp_v_w1, cmp_v_w2, w_up_m, w_up_n,
              merge_gate_b, w_out, norm2_w, ffn_w_up, ffn_conv_w, ffn_conv_b, ffn_w_down)
    for layer in range(norm1_w.shape[0]):
        x = _layer(x, *[p[layer] for p in params])
    return x
```

```python
import functools
import math

import jax
import jax.numpy as jnp
import numpy as np
from jax import lax
from jax.experimental import pallas as pl
from jax.experimental.pallas import tpu as pltpu

D_MODEL = 1024
EPS = 1e-6
ROPE_THETA = 10000.0
NEG = -1e30
M_HEADS = 4
M_DQK = 64
M_DV = 128
M_CONV = 4
M_QK = M_HEADS * M_DQK
M_V = M_HEADS * M_DV
N_HEADS = 8
N_KV = 2
N_HG = N_HEADS // N_KV
N_DH = 64
N_Q = N_HEADS * N_DH
N_KVW = N_KV * N_DH
CMP_LEN = 32
CMP_STRIDE = 16
CMP_HIDDEN = 256
SEL_BLOCK = 64
SEL_TOPN = 16
WINDOW = 512
D_FF = 2816
FFN_CONV = 3

_OFF = {}
_o = 0
for _name, _size in (("mq", M_QK), ("mk", M_QK), ("mv", M_V), ("mo", M_V), ("mi", M_HEADS), ("mf", M_HEADS),
                     ("nq", N_Q), ("kc", N_KVW), ("vc", N_KVW), ("ks", N_KVW), ("vs", N_KVW), ("kw", N_KVW),
                     ("vw", N_KVW), ("ng", 3 * N_HEADS), ("gm", D_MODEL), ("gn", D_MODEL)):
    _OFF[_name] = (_o, _o + _size)
    _o += _size

LANES = 128
MLSTM_CHUNK = 128
SUB_ROWS = 256
Q_TILE = 128
KEY_TILE = 256
WIN_SPAN = WINDOW + Q_TILE
VMEM_LIMIT = 56 * 1024 * 1024

LOG2E = math.log2(math.e)

F32 = jnp.float32
BF16 = jnp.bfloat16
HIGHEST = lax.Precision.HIGHEST


def _dot(a, b):
    return jnp.dot(a, b, preferred_element_type=F32)


def _dot_nt(a, b):
    return lax.dot_general(a, b, (((1,), (1,)), ((), ())), preferred_element_type=F32)


def _rmsnorm_rows(x, w):
    return x * lax.rsqrt(jnp.mean(x * x, axis=-1, keepdims=True) + EPS) * w


def _sigmoid(x):
    return 1.0 / (1.0 + jnp.exp(-x))


def _gelu(x):
    return 0.5 * x * (1.0 + lax.erf(x * (1.0 / math.sqrt(2.0))))


def _params(sem):
    return pltpu.CompilerParams(dimension_semantics=sem, vmem_limit_bytes=VMEM_LIMIT)


def _proj_m_body(hns, w_ref, wvT_ref, cw_ref, cb_ref, sb_ref,
                 qT_ref, k_ref, vT_ref, og_ref, small_ref, smallT_ref, buf_ref, tiles_per_seq):
    i = pl.program_id(0)
    sub = SUB_ROWS

    @pl.when(i % tiles_per_seq == 0)
    def _():
        buf_ref[0:8, :] = jnp.zeros((8, 2 * M_QK), F32)

    for r in range(len(hns)):
        rows = slice(r * sub, (r + 1) * sub)
        hn = hns[r]
        qk = _dot(hn, w_ref[:, 0:2 * M_QK])
        buf_ref[8:8 + sub, :] = qk
        acc = cb_ref[...] + cw_ref[M_CONV - 1:M_CONV, :] * qk
        for k in range(M_CONV - 1):
            acc = acc + cw_ref[k:k + 1, :] * buf_ref[8 - (M_CONV - 1) + k:8 - (M_CONV - 1) + k + sub, :]
        buf_ref[0:8, :] = buf_ref[sub:sub + 8, :]
        act = acc * _sigmoid(acc)
        qT_ref[:, rows] = (act[:, 0:M_QK] * (M_DQK ** -0.5)).T.astype(BF16)
        k_ref[rows, :] = act[:, M_QK:2 * M_QK].astype(BF16)
        vs = _dot_nt(wvT_ref[...], hn)
        vT_ref[:, rows] = vs[0:M_V, :].astype(BF16)
        og_ref[rows, :] = _sigmoid(_dot(hn, w_ref[:, 2 * M_QK:2 * M_QK + M_V]))
        smT = vs[M_V:, :] + sb_ref[...]
        rowi = lax.broadcasted_iota(jnp.int32, smT.shape, 0)
        logsig = jnp.minimum(smT, 0.0) - jnp.log1p(jnp.exp(-jnp.abs(smT)))
        smT = jnp.where(rowi < M_HEADS, smT, jnp.where(rowi < 2 * M_HEADS, logsig, _sigmoid(smT)))
        smallT_ref[:, rows] = smT
        small_ref[rows, :] = jnp.concatenate([smT, jnp.zeros((LANES - 32, sub), F32)], axis=0).T


def _proj_n_body(hn, w_ref, wT_ref, qnw_ref, knw_ref, cosn_ref, sinn_ref, cosT_ref, sinT_ref,
                 qT_ref, kvb_ref, ks_ref, kw_ref, vsT_ref, vwT_ref, regroup_ref):
    tm = hn.shape[0]
    fm = _dot_nt(wT_ref[...], hn)
    qT = fm[0:N_Q, :]
    cosT = cosT_ref[...]
    sinT = sinT_ref[...]
    qnw = qnw_ref[...]
    half = N_DH // 2
    for h in range(N_HEADS):
        xh = qT[h * N_DH:(h + 1) * N_DH, :]
        xn = xh * lax.rsqrt(jnp.mean(xh * xh, axis=0, keepdims=True) + EPS) * qnw
        x1 = xn[0:half, :]
        x2 = xn[half:, :]
        o = jnp.concatenate([x1 * cosT - x2 * sinT, x2 * cosT + x1 * sinT], axis=0) * (LOG2E * N_DH ** -0.5)
        qT_ref[h * N_DH:(h + 1) * N_DH, :] = o.astype(BF16)

    cosn = cosn_ref[...]
    sinn = sinn_ref[...]
    li = lax.broadcasted_iota(jnp.int32, (N_KVW, N_KVW), 0)
    lj = lax.broadcasted_iota(jnp.int32, (N_KVW, N_KVW), 1)
    head_sum = (li // N_DH == lj // N_DH).astype(BF16)
    swap_half = (lj == li + jnp.where(li % N_DH < half, half, -half)).astype(BF16)
    head_sum = jnp.concatenate([head_sum, head_sum], axis=0)
    swap_half = jnp.concatenate([swap_half, swap_half], axis=0)

    def lane_map(v, m01x2):
        hi = v.astype(BF16)
        lo = (v - hi.astype(F32)).astype(BF16)
        return _dot(jnp.concatenate([hi, lo], axis=1), m01x2)

    k_raw = [_dot(hn, w_ref[:, i * N_KVW:(i + 1) * N_KVW]) for i in range(3)]
    vc = _dot(hn, w_ref[:, 3 * N_KVW:])
    ms = [lane_map(k * k, head_sum) * (1.0 / N_DH) for k in k_raw]
    kn = [k_raw[i] * lax.rsqrt(ms[i] + EPS) * knw_ref[i:i + 1, :] for i in range(3)]
    kc, ks, kw = [kn[i] * cosn + lane_map(kn[i], swap_half) * sinn for i in range(3)]
    lane_kv = lax.broadcasted_iota(jnp.int32, (tm // CMP_STRIDE, N_KVW), 1)
    for a, arr in enumerate((kc, vc)):
        regroup_ref[...] = arr
        for l in range(0, CMP_STRIDE, 2):
            even = regroup_ref[pl.ds(l, tm // CMP_STRIDE, stride=CMP_STRIDE), :]
            odd = regroup_ref[pl.ds(l + 1, tm // CMP_STRIDE, stride=CMP_STRIDE), :]
            cols = slice((l // 2) * N_KVW, (l // 2 + 1) * N_KVW)
            kvb_ref[a * N_KV, :, cols] = jnp.where(lane_kv < N_DH, even, pltpu.roll(odd, N_DH, 1))
            kvb_ref[a * N_KV + 1, :, cols] = jnp.where(lane_kv < N_DH, pltpu.roll(even, N_DH, 1), odd)
    ks_ref[...] = ks.astype(BF16)
    kw_ref[...] = kw.astype(BF16)
    vsT = fm[N_Q:N_Q + N_KVW, :].astype(BF16)
    vwT = fm[N_Q + N_KVW:, :].astype(BF16)
    for j in range(tm // LANES):
        vsT_ref[j] = vsT[:, j * LANES:(j + 1) * LANES]
        vwT_ref[j] = vwT[:, j * LANES:(j + 1) * LANES]


def _proj_kernel(x_ref, n1w_ref, *refs, tiles_per_seq):
    tm = x_ref.shape[0]
    sub = SUB_ROWS
    hns = [_rmsnorm_rows(x_ref[r * sub:(r + 1) * sub, :], n1w_ref[...]).astype(BF16) for r in range(tm // sub)]
    m_in, n_in, m_out, n_out, (buf_ref, regroup_ref) = (refs[0:5], refs[5:13], refs[13:19], refs[19:25], refs[25:27])
    _proj_m_body(hns, *m_in, *m_out, buf_ref, tiles_per_seq)
    _proj_n_body(jnp.concatenate(hns, axis=0), *n_in, *n_out, regroup_ref)


def _proj(x2, n1w, m_args, n_args, seq, tm):
    t = x2.shape[0]
    tps = seq // tm
    half = N_DH // 2
    const = lambda i: (0, 0)
    kern = functools.partial(_proj_kernel, tiles_per_seq=tps)
    return pl.pallas_call(
        kern,
        grid=(t // tm,),
        in_specs=[
            pl.BlockSpec((tm, D_MODEL), lambda i: (i, 0)),
            pl.BlockSpec((1, D_MODEL), const),
            pl.BlockSpec(m_args[0].shape, const),
            pl.BlockSpec((M_V + 32, D_MODEL), const),
            pl.BlockSpec((M_CONV, 2 * M_QK), const),
            pl.BlockSpec((1, 2 * M_QK), const),
            pl.BlockSpec((32, 1), const),
            pl.BlockSpec(n_args[0].shape, const),
            pl.BlockSpec((N_Q + 2 * N_KVW, D_MODEL), lambda i: (0, 0)),
            pl.BlockSpec((N_DH, 1), lambda i: (0, 0)),
            pl.BlockSpec((8, N_KVW), lambda i: (0, 0)),
            pl.BlockSpec((tm, N_KVW), lambda i: (i % tps, 0)),
            pl.BlockSpec((tm, N_KVW), lambda i: (i % tps, 0)),
            pl.BlockSpec((half, tm), lambda i: (0, i % tps)),
            pl.BlockSpec((half, tm), lambda i: (0, i % tps)),
        ],
        out_specs=[
            pl.BlockSpec((M_QK, tm), lambda i: (0, i)),
            pl.BlockSpec((tm, M_QK), lambda i: (i, 0)),
            pl.BlockSpec((M_V, tm), lambda i: (0, i)),
            pl.BlockSpec((tm, M_V), lambda i: (i, 0)),
            pl.BlockSpec((tm, LANES), lambda i: (i, 0)),
            pl.BlockSpec((32, tm), lambda i: (0, i)),
            pl.BlockSpec((N_Q, tm), lambda i: (0, i)),
            pl.BlockSpec((2 * N_KV, tm // CMP_STRIDE, CMP_STRIDE * N_DH), lambda i: (0, i, 0)),
            pl.BlockSpec((tm, N_KVW), lambda i: (i, 0)),
            pl.BlockSpec((tm, N_KVW), lambda i: (i, 0)),
            pl.BlockSpec((tm // LANES, N_KVW, LANES), lambda i: (i, 0, 0)),
            pl.BlockSpec((tm // LANES, N_KVW, LANES), lambda i: (i, 0, 0)),
        ],
        out_shape=[
            jax.ShapeDtypeStruct((M_QK, t), BF16),
            jax.ShapeDtypeStruct((t, M_QK), BF16),
            jax.ShapeDtypeStruct((M_V, t), BF16),
            jax.ShapeDtypeStruct((t, M_V), F32),
            jax.ShapeDtypeStruct((t, LANES), F32),
            jax.ShapeDtypeStruct((32, t), F32),
            jax.ShapeDtypeStruct((N_Q, t), BF16),
            jax.ShapeDtypeStruct((2 * N_KV, t // CMP_STRIDE, CMP_STRIDE * N_DH), F32),
            jax.ShapeDtypeStruct((t, N_KVW), BF16),
            jax.ShapeDtypeStruct((t, N_KVW), BF16),
            jax.ShapeDtypeStruct((t // LANES, N_KVW, LANES), BF16),
            jax.ShapeDtypeStruct((t // LANES, N_KVW, LANES), BF16),
        ],
        scratch_shapes=[pltpu.VMEM((SUB_ROWS + 8, 2 * M_QK), F32), pltpu.VMEM((tm, N_KVW), F32)],
        compiler_params=_params(("arbitrary",)),
        name="proj",
    )(x2, n1w, *m_args, *n_args)


def _compress_kernel(x_ref, pe_ref, w1_ref, w2_ref, kcmp_ref, vcmpT_ref):
    nb = x_ref.shape[2]
    half = (CMP_LEN // 2) * N_DH

    def mlp(a, kv):
        pe = pe_ref[kv]
        x = x_ref[a, 0]
        first = _dot((x + pe[:, 0:half]).astype(BF16), w1_ref[kv, 0:half, :])
        second = _dot((x + pe[:, half:]).astype(BF16), w1_ref[kv, half:, :])
        hid = first + pltpu.roll(second, nb - 1, 0)
        return _dot(_gelu(hid).astype(BF16), w2_ref[kv])

    kcmp_ref[...] = jnp.concatenate([mlp(g, 0) for g in range(N_KV)], axis=1).astype(BF16)
    vcmpT_ref[...] = jnp.concatenate([mlp(N_KV + g, 1) for g in range(N_KV)], axis=1).T.astype(BF16)


def _compress(xkv, pe, w1, w2):
    na, batch, nb, width = xkv.shape
    const3 = lambda b: (0, 0, 0)
    return pl.pallas_call(
        _compress_kernel,
        grid=(batch,),
        in_specs=[
            pl.BlockSpec((na, 1, nb, width), lambda b: (0, b, 0, 0)),
            pl.BlockSpec((2, 1, CMP_LEN * N_DH), const3),
            pl.BlockSpec((2, CMP_LEN * N_DH, CMP_HIDDEN), const3),
            pl.BlockSpec((2, CMP_HIDDEN, N_DH), const3),
        ],
        out_specs=[
            pl.BlockSpec((None, nb, N_KVW), lambda b: (b, 0, 0)),
            pl.BlockSpec((None, N_KVW, nb), lambda b: (b, 0, 0)),
        ],
        out_shape=[
            jax.ShapeDtypeStruct((batch, nb, N_KVW), BF16),
            jax.ShapeDtypeStruct((batch, N_KVW, nb), BF16),
        ],
        compiler_params=_params(("arbitrary",)),
        name="compress",
    )(xkv, pe, w1, w2)


def _mlstm_kernel(qT_ref, k_ref, vT_ref, small_ref, smallT_ref, og_ref, onw_ref, o_ref, c_ref, m_ref):
    L = MLSTM_CHUNK
    tb = k_ref.shape[0]

    @pl.when(pl.program_id(1) == 0)
    def _():
        c_ref[...] = jnp.zeros(c_ref.shape, F32)
        m_ref[...] = jnp.zeros(m_ref.shape, F32)

    row = lax.broadcasted_iota(jnp.int32, (L, L), 0)
    col = lax.broadcasted_iota(jnp.int32, (L, L), 1)
    causal = row <= col
    tril = (col <= row).astype(F32)
    triu = causal.astype(F32)
    ones_rows = jnp.ones((16, L), BF16)
    zeros_q = jnp.zeros((M_DQK, L), BF16)

    heads = range(M_HEADS)
    chunks = range(tb // L)
    sls = [slice(c * L, (c + 1) * L) for c in chunks]
    sms = [small_ref[sl, :] for sl in sls]
    bcols = [jnp.dot(tril, sm, preferred_element_type=F32, precision=HIGHEST) for sm in sms]
    brows = [jnp.dot(smallT_ref[0:8, sl], triu, preferred_element_type=F32, precision=HIGHEST) for sl in sls]
    c_state = [c_ref[h] for h in heads]
    m_state = [m_ref[h:h + 1, 0:1] for h in heads]
    for c in chunks:
        sl = sls[c]
        k_pairs = [k_ref[sl, p * LANES:(p + 1) * LANES] for p in range(M_HEADS // 2)]
        qT_pad, s, qc, vT_aug = [], [], [], []
        for h in heads:
            qT_h = qT_ref[h * M_DQK:(h + 1) * M_DQK, sl]
            qT_pad.append(jnp.concatenate([qT_h, zeros_q] if h % 2 == 0 else [zeros_q, qT_h], axis=0))
            s.append(_dot(k_pairs[h // 2], qT_pad[h]))
            qc.append(_dot(c_state[h].astype(BF16), qT_pad[h]))
            vT_aug.append(jnp.concatenate([vT_ref[h * M_DV:(h + 1) * M_DV, sl], ones_rows], axis=0))
        m_t, isc, sw, kw, m_new, decay, scale = [], [], [], [], [], [], []
        for h in heads:
            in_head = (col // M_DQK) == (h % 2)
            b_col = bcols[c][:, M_HEADS + h:M_HEADS + h + 1]
            i_col = sms[c][:, h:h + 1]
            b_row = brows[c][M_HEADS + h:M_HEADS + h + 1, :]
            g = b_row[:, L - 1:L]
            d = jnp.where(causal, b_row + (i_col - b_col), NEG)
            inter = b_row + m_state[h]
            m_t.append(jnp.maximum(inter, jnp.max(d, axis=0, keepdims=True)))
            isc.append(jnp.exp(inter - m_t[h]))
            sw.append((s[h] * jnp.exp(d - m_t[h])).astype(BF16))
            a_col = g - b_col + i_col
            a_max = jnp.max(a_col, axis=0, keepdims=True)
            kw.append(jnp.where(in_head, k_pairs[h // 2].astype(F32) * jnp.exp(a_col - a_max), 0.0).astype(BF16))
            m_new.append(jnp.maximum(g + m_state[h], a_max))
            decay.append(jnp.exp(g + m_state[h] - m_new[h]))
            scale.append(jnp.exp(a_max - m_new[h]))
        sv = [_dot(vT_aug[h], sw[h]) for h in heads]
        dc = [_dot(vT_aug[h], kw[h]) for h in heads]
        for h in heads:
            nd = isc[h] * qc[h] + sv[h]
            den = nd[M_DV:M_DV + 1, :]
            hh = nd[0:M_DV, :] / jnp.maximum(jnp.abs(den), jnp.exp(-m_t[h]))
            hn = hh * lax.rsqrt(jnp.mean(hh * hh, axis=0, keepdims=True) + EPS)
            o_ref[sl, h * M_DV:(h + 1) * M_DV] = hn.T * onw_ref[h:h + 1, :] * og_ref[sl, h * M_DV:(h + 1) * M_DV]
            c_state[h] = decay[h] * c_state[h] + scale[h] * dc[h]
            m_state[h] = m_new[h]
    for h in heads:
        c_ref[h] = c_state[h]
        m_ref[h:h + 1, :] = jnp.broadcast_to(m_state[h], (1, LANES))


def _mlstm(qT, k, vT, small, smallT, og, onw, batch, seq, tb):
    t = k.shape[0]
    nblk = seq // tb
    return pl.pallas_call(
        _mlstm_kernel,
        grid=(batch, nblk),
        in_specs=[
            pl.BlockSpec((M_QK, tb), lambda b, j: (0, b * nblk + j)),
            pl.BlockSpec((tb, M_QK), lambda b, j: (b * nblk + j, 0)),
            pl.BlockSpec((M_V, tb), lambda b, j: (0, b * nblk + j)),
            pl.BlockSpec((tb, LANES), lambda b, j: (b * nblk + j, 0)),
            pl.BlockSpec((32, tb), lambda b, j: (0, b * nblk + j)),
            pl.BlockSpec((tb, M_V), lambda b, j: (b * nblk + j, 0)),
            pl.BlockSpec((8, M_DV), lambda b, j: (0, 0)),
        ],
        out_specs=pl.BlockSpec((tb, M_V), lambda b, j: (b * nblk + j, 0)),
        out_shape=jax.ShapeDtypeStruct((t, M_V), F32),
        scratch_shapes=[pltpu.VMEM((M_HEADS, M_DV + 16, LANES), F32), pltpu.VMEM((8, LANES), F32)],
        compiler_params=_params(("arbitrary", "arbitrary")),
        name="mlstm",
    )(qT, k, vT, small, smallT, og, onw)


def _nsa_kernel(qT_ref, kcmp_ref, vcmpT_ref, ks_ref, e_ref, vsT_ref, kw_ref, vwT_ref, gT_ref,
                o_ref, rhs_ref, ps_ref, ocmp_ref, owin_ref, s0_ref, s1_ref, c0_ref, c1_ref, m_ref, acc_ref,
                *, nsel, n_rounds):
    qi = pl.program_id(1)
    tq = Q_TILE
    nb = kcmp_ref.shape[0]
    nselp = e_ref.shape[1]
    q0 = qi * tq
    wide = N_HG * tq
    lane_w = lax.broadcasted_iota(jnp.int32, (1, wide), 1)
    tpos_w = q0 + (lane_w % tq)
    tpos = q0 + lax.broadcasted_iota(jnp.int32, (1, tq), 1)
    zeros_q = jnp.zeros((N_DH, wide), BF16)
    ones_v = jnp.ones((16, LANES), BF16)
    grows = [slice(g * N_DH, (g + 1) * N_DH) for g in range(N_KV)]
    gcols = [slice(g * wide, (g + 1) * wide) for g in range(N_KV)]

    def values(v_ref, first, count, rows):
        return jnp.concatenate(
            [jnp.concatenate([v_ref[first + j, rows, :], ones_v], axis=0) for j in range(count)], axis=1)

    start_w = pl.multiple_of(jnp.maximum(q0 - WINDOW, 0), LANES)
    qpads = []
    for g in range(N_KV):
        q4 = jnp.concatenate(
            [qT_ref[(g * N_HG + h) * N_DH:(g * N_HG + h + 1) * N_DH, :] for h in range(N_HG)], axis=1)
        qpads.append(jnp.concatenate([q4, zeros_q] if g == 0 else [zeros_q, q4], axis=0))
        rhs_ref[0:2 * N_DH, gcols[g]] = qpads[g]

    def scores(kt, g):
        start = pl.multiple_of(kt * KEY_TILE, KEY_TILE)
        lhs = jnp.concatenate([ks_ref[pl.ds(start, KEY_TILE), :], e_ref[pl.ds(start, KEY_TILE), :]], axis=1)
        return _dot(lhs, rhs_ref[:, g * wide:(g + 1) * wide])

    def produce(kt, s_ref, c_ref):
        for g in range(N_KV):
            s = scores(kt, g)
            s_ref[g] = s
            c_ref[g] = jnp.max(s, axis=0, keepdims=True)

    def consume(kt, s_ref, c_ref, causal_rows=0):
        for g in range(N_KV):
            if causal_rows:
                kpos = kt * KEY_TILE + lax.broadcasted_iota(jnp.int32, (causal_rows, 1), 0)
                s = jnp.where(kpos <= tpos_w, s_ref[g, 0:causal_rows, :], NEG)
                smax = jnp.max(s, axis=0, keepdims=True)
            else:
                s = s_ref[g]
                smax = c_ref[g]
            m = m_ref[g]
            m_new = jnp.maximum(m, smax)
            p = jnp.exp2(s - m_new).astype(BF16)
            vt = values(vsT_ref, kt * (KEY_TILE // LANES), (causal_rows or KEY_TILE) // LANES,
                        slice(g * N_DH, (g + 1) * N_DH))
            acc_ref[g] = jnp.exp2(m - m_new) * acc_ref[g] + _dot(vt, p)
            m_ref[g] = m_new

    ratio = SEL_BLOCK // CMP_STRIDE

    def front(nrows, first_q0):
        nbv = nrows * ratio
        free = max((first_q0 - (CMP_LEN - 1)) // CMP_STRIDE + 1, 0) // 8 * 8
        cend = (free + lax.broadcasted_iota(jnp.int32, (nbv - free, 1), 0)) * CMP_STRIDE + (CMP_LEN - 1)
        cmask = cend <= tpos_w
        any_visible = tpos_w >= CMP_LEN - 1
        sc = [_dot(kcmp_ref[0:nbv, :], qpads[g]) for g in range(N_KV)]
        sw = [_dot(kw_ref[pl.ds(start_w, WIN_SPAN), :], qpads[g]) for g in range(N_KV)]
        imp = []
        for g in range(N_KV):
            s = jnp.where(cmask, sc[g][free:, :], NEG)
            if free:
                s = jnp.concatenate([sc[g][0:free, :], s], axis=0)
            pc = jnp.exp2(s - jnp.max(s, axis=0, keepdims=True))
            lc = jnp.sum(pc, axis=0, keepdims=True)
            pc = pc * jnp.where(any_visible, 1.0 / lc, 0.0)
            ocmp_ref[g] = _dot(vcmpT_ref[grows[g], 0:nbv], pc.astype(BF16))
            psum = pc[:, 0:tq]
            for h in range(1, N_HG):
                psum = psum + pc[:, h * tq:(h + 1) * tq]
            parts = []
            for cchunk in range(tq // LANES):
                ps_ref[g, cchunk, 0:8, :] = jnp.zeros((8, LANES), F32)
                ps_ref[g, cchunk, 8:8 + nbv, :] = psum[:, cchunk * LANES:(cchunk + 1) * LANES]
                acc = None
                for k in range(-((CMP_LEN - 1) // CMP_STRIDE), ratio):
                    part = ps_ref[g, cchunk, pl.ds(8 + k, nrows, stride=ratio), :]
                    acc = part if acc is None else acc + part
                parts.append(acc)
            imp.append(jnp.concatenate(parts, axis=1))

        jblk = lax.broadcasted_iota(jnp.int32, (nrows, tq), 0)
        cur = tpos // SEL_BLOCK
        forced = (jblk == 0) | (jblk == cur) | (jblk == cur - 1)
        cand = (jblk >= 1) & (jblk <= cur - 2)
        jblk_f = jblk.astype(F32)
        val = [jnp.where(cand, imp[g], -jnp.inf) for g in range(N_KV)]
        for _ in range(n_rounds):
            for g in range(N_KV):
                mx = jnp.max(val[g], axis=0, keepdims=True)
                first = jnp.min(jnp.where(val[g] == mx, jblk_f, float(nrows)), axis=0, keepdims=True)
                val[g] = jnp.where(jblk_f == first, -jnp.inf, val[g])
        for g in range(N_KV):
            picked = cand & (val[g] == -jnp.inf)
            bias = jnp.where(forced | picked, 0.0, NEG).astype(BF16)
            if nselp > nrows:
                bias = jnp.concatenate([bias, jnp.zeros((nselp - nrows, tq), BF16)], axis=0)
            rhs_ref[2 * N_DH:, gcols[g]] = jnp.concatenate([bias] * N_HG, axis=1)

        produce(0, s0_ref, c0_ref)

        def window_mask(s, r0, r1):
            dist = (tpos_w - start_w) - (r0 + lax.broadcasted_iota(jnp.int32, (r1 - r0, 1), 0))
            return jnp.where(lax.bitcast_convert_type(dist, jnp.uint32) < WINDOW, s[r0:r1, :], NEG)

        for g in range(N_KV):
            if first_q0 >= WINDOW:
                s = jnp.concatenate([window_mask(sw[g], 0, tq), sw[g][tq:WINDOW, :],
                                     window_mask(sw[g], WINDOW, WIN_SPAN)], axis=0)
            else:
                s = window_mask(sw[g], 0, WIN_SPAN)
            pw = jnp.exp2(s - jnp.max(s, axis=0, keepdims=True))
            ow = _dot(values(vwT_ref, start_w // LANES, WIN_SPAN // LANES, grows[g]), pw.astype(BF16))
            owin_ref[g] = ow[0:N_DH, :] * (1.0 / ow[N_DH:N_DH + 1, :])

    n_full = qi // (KEY_TILE // tq)
    odd = n_full % 2
    m_ref[...] = jnp.full(m_ref.shape, NEG, F32)
    acc_ref[...] = jnp.zeros(acc_ref.shape, F32)
    n_var = 4 if nsel % 32 == 0 else 1
    if n_var == 1:
        front(nsel, 0)
    else:
        tiles_per_var = (nsel * SEL_BLOCK // tq) // n_var
        for v in range(n_var):
            pl.when(qi // tiles_per_var == v)(
                functools.partial(front, (v + 1) * nsel // n_var, v * tiles_per_var * tq))

    def pair(kt):
        produce(kt + 1, s1_ref, c1_ref)
        consume(kt, s0_ref, c0_ref)
        produce(kt + 2, s0_ref, c0_ref)
        consume(kt + 1, s1_ref, c1_ref)

    def octet(j, _):
        for i in range(0, 8, 2):
            pair(8 * j + i)
        return 0

    n_oct = n_full // 8
    lax.fori_loop(0, n_oct, octet, 0)
    done = 8 * n_oct

    @pl.when(n_full - done >= 4)
    def _():
        pair(done)
        pair(done + 2)

    done = done + jnp.where(n_full - done >= 4, 4, 0)

    @pl.when(n_full - done >= 2)
    def _():
        pair(done)

    for at_start in (False, True):
        rows = tq if at_start else KEY_TILE
        starts_tile = (q0 % KEY_TILE == 0) == at_start

        @pl.when((odd == 0) & starts_tile)
        def _():
            consume(n_full, s0_ref, c0_ref, causal_rows=rows)

        @pl.when((odd == 1) & starts_tile)
        def _():
            produce(n_full, s1_ref, c1_ref)
            consume(n_full - 1, s0_ref, c0_ref)
            consume(n_full, s1_ref, c1_ref, causal_rows=rows)

    outs = []
    for g in range(N_KV):
        o_slc = acc_ref[g, 0:N_DH, :] * (1.0 / acc_ref[g, N_DH:N_DH + 1, :])
        for h in range(N_HG):
            cs = slice(h * tq, (h + 1) * tq)
            r = 2 * M_HEADS + (g * N_HG + h) * 3
            outs.append(gT_ref[r:r + 1, :] * ocmp_ref[g, :, cs] + gT_ref[r + 1:r + 2, :] * o_slc[:, cs]
                        + gT_ref[r + 2:r + 3, :] * owin_ref[g, :, cs])

    o_ref[...] = jnp.concatenate(outs, axis=0).T


def _nsa(qT, kcmp, vcmpT, ks, emap, vsT, kw, vwT, smallT, batch, seq):
    t = qT.shape[1]
    nq = seq // Q_TILE
    nb = kcmp.shape[1]
    nsel = seq // SEL_BLOCK
    nselp = emap.shape[1]
    wide = N_HG * Q_TILE
    n_rounds = max(min(SEL_TOPN, nsel) - 3, 0)
    kern = functools.partial(_nsa_kernel, nsel=nsel, n_rounds=n_rounds)
    return pl.pallas_call(
        kern,
        grid=(batch, nq),
        in_specs=[
            pl.BlockSpec((N_Q, Q_TILE), lambda b, i: (0, b * nq + i)),
            pl.BlockSpec((None, nb, N_KVW), lambda b, i: (b, 0, 0)),
            pl.BlockSpec((None, N_KVW, nb), lambda b, i: (b, 0, 0)),
            pl.BlockSpec((seq, N_KVW), lambda b, i: (b, 0)),
            pl.BlockSpec((seq, nselp), lambda b, i: (0, 0)),
            pl.BlockSpec((seq // LANES, N_KVW, LANES), lambda b, i: (b, 0, 0)),
            pl.BlockSpec((seq, N_KVW), lambda b, i: (b, 0)),
            pl.BlockSpec((seq // LANES, N_KVW, LANES), lambda b, i: (b, 0, 0)),
            pl.BlockSpec((32, Q_TILE), lambda b, i: (0, b * nq + i)),
        ],
        out_specs=pl.BlockSpec((Q_TILE, N_Q), lambda b, i: (b * nq + i, 0)),
        out_shape=jax.ShapeDtypeStruct((t, N_Q), F32),
        scratch_shapes=[
            pltpu.VMEM((2 * N_DH + nselp, N_KV * wide), BF16),
            pltpu.VMEM((N_KV, Q_TILE // LANES, nb + 8, LANES), F32),
            pltpu.VMEM((N_KV, N_DH, wide), F32),
            pltpu.VMEM((N_KV, N_DH, wide), F32),
            pltpu.VMEM((N_KV, KEY_TILE, wide), F32),
            pltpu.VMEM((N_KV, KEY_TILE, wide), F32),
            pltpu.VMEM((N_KV, 1, wide), F32),
            pltpu.VMEM((N_KV, 1, wide), F32),
            pltpu.VMEM((N_KV, 1, wide), F32),
            pltpu.VMEM((N_KV, N_DH + 16, wide), F32),
        ],
        compiler_params=_params(("arbitrary", "arbitrary")),
        name="nsa",
    )(qT, kcmp, vcmpT, ks, emap, vsT, kw, vwT, smallT)


def _merge_kernel(x_ref, n1w_ref, wg_ref, gb_ref, hm_ref, on_ref, wm_ref, wn_ref, wo_ref, o_ref):
    x = x_ref[...]
    hn = _rmsnorm_rows(x, n1w_ref[...]).astype(BF16)
    gm = _sigmoid(_dot(hn, wg_ref[:, 0:D_MODEL]) + gb_ref[0:1, :])
    gn = _sigmoid(_dot(hn, wg_ref[:, D_MODEL:]) + gb_ref[1:2, :])
    y = gm * _dot(hm_ref[...].astype(BF16), wm_ref[...]) + gn * _dot(on_ref[...].astype(BF16), wn_ref[...])
    o_ref[...] = x + _dot(y.astype(BF16), wo_ref[...])


def _merge(x2, n1w, wg, gb, hm, on, wm, wn, wo, tm):
    t = x2.shape[0]
    const = lambda i: (0, 0)
    return pl.pallas_call(
        _merge_kernel,
        grid=(t // tm,),
        in_specs=[
            pl.BlockSpec((tm, D_MODEL), lambda i: (i, 0)),
            pl.BlockSpec((1, D_MODEL), const),
            pl.BlockSpec((D_MODEL, 2 * D_MODEL), const),
            pl.BlockSpec((2, D_MODEL), const),
            pl.BlockSpec((tm, M_V), lambda i: (i, 0)),
            pl.BlockSpec((tm, N_Q), lambda i: (i, 0)),
            pl.BlockSpec((M_V, D_MODEL), const),
            pl.BlockSpec((N_Q, D_MODEL), const),
            pl.BlockSpec((D_MODEL, D_MODEL), const),
        ],
        out_specs=pl.BlockSpec((tm, D_MODEL), lambda i: (i, 0)),
        out_shape=jax.ShapeDtypeStruct((t, D_MODEL), F32),
        compiler_params=_params(("arbitrary",)),
        name="merge",
    )(x2, n1w, wg, gb, hm, on, wm, wn, wo)


def _ffn_kernel(x_ref, n2w_ref, wup_ref, cw_ref, cb_ref, wdn_ref, o_ref, buf_ref, *, tiles_per_seq):
    i = pl.program_id(0)
    tm = x_ref.shape[0]
    x = x_ref[...]
    hn = _rmsnorm_rows(x, n2w_ref[...]).astype(BF16)

    @pl.when(i % tiles_per_seq == 0)
    def _():
        buf_ref[0:8, :] = jnp.zeros((8, D_FF), F32)

    a = _dot(hn, wup_ref[:, 0:D_FF])
    buf_ref[8:8 + tm, :] = a
    acc = cb_ref[...] + cw_ref[FFN_CONV - 1:FFN_CONV, :] * a
    for k in range(FFN_CONV - 1):
        acc = acc + cw_ref[k:k + 1, :] * buf_ref[8 - (FFN_CONV - 1) + k:8 - (FFN_CONV - 1) + k + tm, :]
    buf_ref[0:8, :] = buf_ref[tm:tm + 8, :]
    v = _dot(hn, wup_ref[:, D_FF:])
    o_ref[...] = x + _dot((_gelu(acc) * v).astype(BF16), wdn_ref[...])


def _ffn(x2, n2w, wup, cw, cb, wdn, seq, tm):
    t = x2.shape[0]
    const = lambda i: (0, 0)
    kern = functools.partial(_ffn_kernel, tiles_per_seq=seq // tm)
    return pl.pallas_call(
        kern,
        grid=(t // tm,),
        in_specs=[
            pl.BlockSpec((tm, D_MODEL), lambda i: (i, 0)),
            pl.BlockSpec((1, D_MODEL), const),
            pl.BlockSpec((D_MODEL, 2 * D_FF), const, pipeline_mode=pl.Buffered(1)),
            pl.BlockSpec((FFN_CONV, D_FF), const),
            pl.BlockSpec((1, D_FF), const),
            pl.BlockSpec((D_FF, D_MODEL), const, pipeline_mode=pl.Buffered(1)),
        ],
        out_specs=pl.BlockSpec((tm, D_MODEL), lambda i: (i, 0)),
        out_shape=jax.ShapeDtypeStruct((t, D_MODEL), F32),
        scratch_shapes=[pltpu.VMEM((tm + 8, D_FF), F32)],
        compiler_params=_params(("arbitrary",)),
        name="ffn",
    )(x2, n2w, wup, cw, cb, wdn)


def _cols(w, *names):
    return jnp.concatenate([w[:, _OFF[n][0]:_OFF[n][1]] for n in names], axis=1)


def _layer(x, n1w, w_in, m_conv_w, m_conv_b, m_igate_b, m_fgate_b, m_out_norm_w,
           q_norm_w, kcmp_norm_w, kslc_norm_w, kwin_norm_w,
           cmp_k_pe, cmp_k_w1, cmp_k_w2, cmp_v_pe, cmp_v_w1, cmp_v_w2,
           w_up_m, w_up_n, merge_gate_b, w_out, norm2_w, ffn_w_up, ffn_conv_w, ffn_conv_b, ffn_w_down):
    batch, seq, _ = x.shape
    t = batch * seq
    x2 = x.reshape(t, D_MODEL)
    n1w2 = n1w.reshape(1, D_MODEL)
    tm = 256
    tm_proj = 512

    w_m = _cols(w_in, "mq", "mk", "mo").astype(BF16)
    w_mvT = _cols(w_in, "mv", "mi", "mf", "ng").T.astype(BF16)
    w_n = _cols(w_in, "kc", "ks", "kw", "vc").astype(BF16)
    w_nT = _cols(w_in, "nq", "vs", "vw").T.astype(BF16)
    w_g = _cols(w_in, "gm", "gn").astype(BF16)
    sbias = jnp.concatenate([m_igate_b, m_fgate_b, jnp.zeros((3 * N_HEADS,), F32)]).reshape(32, 1)

    half = N_DH // 2
    pos = jnp.arange(seq, dtype=F32)
    inv = ROPE_THETA ** (-jnp.arange(0, N_DH, 2, dtype=F32) / N_DH)
    ang = pos[:, None] * inv[None, :]
    cos, sin = jnp.cos(ang), jnp.sin(ang)
    cosn = jnp.tile(cos, (1, N_KVW // half))
    sinn = jnp.tile(jnp.concatenate([-sin, sin], axis=1), (1, N_KV))
    cosT, sinT = cos.T, sin.T
    knw = jnp.zeros((8, N_KVW), F32).at[0:3].set(
        jnp.stack([jnp.tile(w, N_KV) for w in (kcmp_norm_w, kslc_norm_w, kwin_norm_w)]))

    qT_m, k_m, vT_m, og, small, smallT, qT, kvb, ks, kw, vsT, vwT = _proj(
        x2, n1w2, (w_m, w_mvT, m_conv_w, m_conv_b.reshape(1, -1), sbias),
        (w_n, w_nT, q_norm_w.reshape(N_DH, 1), knw, cosn, sinn, cosT, sinT), seq, tm_proj)

    nb = seq // CMP_STRIDE
    xkv = kvb.reshape(2 * N_KV, batch, nb, CMP_STRIDE * N_DH)
    pe = jnp.stack([cmp_k_pe.reshape(1, -1), cmp_v_pe.reshape(1, -1)])
    w1 = jnp.stack([cmp_k_w1, cmp_v_w1]).astype(BF16)
    w2 = jnp.stack([cmp_k_w2, cmp_v_w2]).astype(BF16)
    kcmp, vcmpT = _compress(xkv, pe, w1, w2)

    onw = jnp.zeros((8, M_DV), F32).at[0:M_HEADS].set(m_out_norm_w)
    hm = _mlstm(qT_m, k_m, vT_m, small, smallT, og, onw, batch, seq, 4 * MLSTM_CHUNK)
    nselp = -(-(seq // SEL_BLOCK) // LANES) * LANES
    emap = (np.arange(seq)[:, None] // SEL_BLOCK == np.arange(nselp)[None, :]).astype(np.float32)
    on = _nsa(qT, kcmp, vcmpT, ks, jnp.asarray(emap, dtype=BF16), vsT, kw, vwT, smallT, batch, seq)

    x1 = _merge(x2, n1w2, w_g, merge_gate_b, hm, on, w_up_m.astype(BF16), w_up_n.astype(BF16),
                w_out.astype(BF16), tm_proj)
    out = _ffn(x1, norm2_w.reshape(1, D_MODEL), ffn_w_up.astype(BF16), ffn_conv_w, ffn_conv_b.reshape(1, -1),
               ffn_w_down.astype(BF16), seq, tm_proj)
    return out.reshape(batch, seq, D_MODEL)


def kernel(x, norm1_w, w_in, m_conv_w, m_conv_b, m_igate_b, m_fgate_b, m_out_norm_w, q_norm_w, kcmp_norm_w,
           kslc_norm_w, kwin_norm_w, cmp_k_pe, cmp_k_w1, cmp_k_w2, cmp_v_pe, cmp_v_w1, cmp_v_w2, w_up_m, w_up_n,
           merge_gate_b, w_out, norm2_w, ffn_w_up, ffn_conv_w, ffn_conv_b, ffn_w_down):
    params = (norm1_w, w_in, m_conv_w, m_conv_b, m_igate_b, m_fgate_b, m_out_norm_w, q_norm_w, kcmp_norm_w,
              kslc_norm_w, kwin_norm_w, cmp_k_pe, cmp_k_w1, cmp_k_w2, cmp_v_pe, cmp_v_w1, cmp_v_w2, w_up_m, w_up_n,
              merge_gate_b, w_out, norm2_w, ffn_w_up, ffn_conv_w, ffn_conv_b, ffn_w_down)
    for layer in range(norm1_w.shape[0]):
        x = _layer(x, *[p[layer] for p in params])
    return x
```

```python
import functools
import math

import jax
import jax.numpy as jnp
import numpy as np
from jax import lax
from jax.experimental import pallas as pl
from jax.experimental.pallas import tpu as pltpu

D_MODEL = 1024
EPS = 1e-6
ROPE_THETA = 10000.0
NEG = -1e30
M_HEADS = 4
M_DQK = 64
M_DV = 128
M_CONV = 4
M_QK = M_HEADS * M_DQK
M_V = M_HEADS * M_DV
N_HEADS = 8
N_KV = 2
N_HG = N_HEADS // N_KV
N_DH = 64
N_Q = N_HEADS * N_DH
N_KVW = N_KV * N_DH
CMP_LEN = 32
CMP_STRIDE = 16
CMP_HIDDEN = 256
SEL_BLOCK = 64
SEL_TOPN = 16
WINDOW = 512
D_FF = 2816
FFN_CONV = 3

_OFF = {}
_o = 0
for _name, _size in (("mq", M_QK), ("mk", M_QK), ("mv", M_V), ("mo", M_V), ("mi", M_HEADS), ("mf", M_HEADS),
                     ("nq", N_Q), ("kc", N_KVW), ("vc", N_KVW), ("ks", N_KVW), ("vs", N_KVW), ("kw", N_KVW),
                     ("vw", N_KVW), ("ng", 3 * N_HEADS), ("gm", D_MODEL), ("gn", D_MODEL)):
    _OFF[_name] = (_o, _o + _size)
    _o += _size

LANES = 128
MLSTM_CHUNK = 128
SUB_ROWS = 256
Q_TILE = 128
KEY_TILE = 256
WIN_SPAN = WINDOW + Q_TILE
VMEM_LIMIT = 56 * 1024 * 1024

LOG2E = math.log2(math.e)

F32 = jnp.float32
BF16 = jnp.bfloat16
HIGHEST = lax.Precision.HIGHEST


def _dot(a, b):
    return jnp.dot(a, b, preferred_element_type=F32)


def _dot_nt(a, b):
    return lax.dot_general(a, b, (((1,), (1,)), ((), ())), preferred_element_type=F32)


def _rmsnorm_rows(x, w):
    return x * lax.rsqrt(jnp.mean(x * x, axis=-1, keepdims=True) + EPS) * w


def _sigmoid(x):
    return 1.0 / (1.0 + jnp.exp(-x))


def _gelu(x):
    return 0.5 * x * (1.0 + lax.erf(x * (1.0 / math.sqrt(2.0))))


def _params(sem):
    return pltpu.CompilerParams(dimension_semantics=sem, vmem_limit_bytes=VMEM_LIMIT)


def _proj_m_body(hns, w_ref, wvT_ref, cw_ref, cb_ref, sb_ref,
                 qT_ref, k_ref, vT_ref, og_ref, small_ref, smallT_ref, buf_ref, tiles_per_seq):
    i = pl.program_id(0)
    sub = SUB_ROWS

    @pl.when(i % tiles_per_seq == 0)
    def _():
        buf_ref[0:8, :] = jnp.zeros((8, 2 * M_QK), F32)

    for r in range(len(hns)):
        rows = slice(r * sub, (r + 1) * sub)
        hn = hns[r]
        qk = _dot(hn, w_ref[:, 0:2 * M_QK])
        buf_ref[8:8 + sub, :] = qk
        acc = cb_ref[...] + cw_ref[M_CONV - 1:M_CONV, :] * qk
        for k in range(M_CONV - 1):
            acc = acc + cw_ref[k:k + 1, :] * buf_ref[8 - (M_CONV - 1) + k:8 - (M_CONV - 1) + k + sub, :]
        buf_ref[0:8, :] = buf_ref[sub:sub + 8, :]
        act = acc * _sigmoid(acc)
        qT_ref[:, rows] = (act[:, 0:M_QK] * (M_DQK ** -0.5)).T.astype(BF16)
        k_ref[rows, :] = act[:, M_QK:2 * M_QK].astype(BF16)
        vs = _dot_nt(wvT_ref[...], hn)
        vT_ref[:, rows] = vs[0:M_V, :].astype(BF16)
        og_ref[rows, :] = _sigmoid(_dot(hn, w_ref[:, 2 * M_QK:2 * M_QK + M_V]))
        smT = vs[M_V:, :] + sb_ref[...]
        rowi = lax.broadcasted_iota(jnp.int32, smT.shape, 0)
        logsig = jnp.minimum(smT, 0.0) - jnp.log1p(jnp.exp(-jnp.abs(smT)))
        smT = jnp.where(rowi < M_HEADS, smT, jnp.where(rowi < 2 * M_HEADS, logsig, _sigmoid(smT)))
        smallT_ref[:, rows] = smT
        small_ref[rows, :] = jnp.concatenate([smT, jnp.zeros((LANES - 32, sub), F32)], axis=0).T


def _proj_n_body(hn, w_ref, wT_ref, qnw_ref, knw_ref, cosn_ref, sinn_ref, cosT_ref, sinT_ref,
                 qT_ref, kvb_ref, ks_ref, kw_ref, vsT_ref, vwT_ref, regroup_ref):
    tm = hn.shape[0]
    fm = _dot_nt(wT_ref[...], hn)
    qT = fm[0:N_Q, :]
    cosT = cosT_ref[...]
    sinT = sinT_ref[...]
    qnw = qnw_ref[...]
    half = N_DH // 2
    for h in range(N_HEADS):
        xh = qT[h * N_DH:(h + 1) * N_DH, :]
        xn = xh * lax.rsqrt(jnp.mean(xh * xh, axis=0, keepdims=True) + EPS) * qnw
        x1 = xn[0:half, :]
        x2 = xn[half:, :]
        o = jnp.concatenate([x1 * cosT - x2 * sinT, x2 * cosT + x1 * sinT], axis=0) * (LOG2E * N_DH ** -0.5)
        qT_ref[h * N_DH:(h + 1) * N_DH, :] = o.astype(BF16)

    cosn = cosn_ref[...]
    sinn = sinn_ref[...]
    li = lax.broadcasted_iota(jnp.int32, (N_KVW, N_KVW), 0)
    lj = lax.broadcasted_iota(jnp.int32, (N_KVW, N_KVW), 1)
    head_sum = (li // N_DH == lj // N_DH).astype(BF16)
    swap_half = (lj == li + jnp.where(li % N_DH < half, half, -half)).astype(BF16)
    head_sum = jnp.concatenate([head_sum, head_sum], axis=0)
    swap_half = jnp.concatenate([swap_half, swap_half], axis=0)

    def lane_map(v, m01x2):
        hi = v.astype(BF16)
        lo = (v - hi.astype(F32)).astype(BF16)
        return _dot(jnp.concatenate([hi, lo], axis=1), m01x2)

    k_raw = [_dot(hn, w_ref[:, i * N_KVW:(i + 1) * N_KVW]) for i in range(3)]
    vc = _dot(hn, w_ref[:, 3 * N_KVW:])
    ms = [lane_map(k * k, head_sum) * (1.0 / N_DH) for k in k_raw]
    kn = [k_raw[i] * lax.rsqrt(ms[i] + EPS) * knw_ref[i:i + 1, :] for i in range(3)]
    kc, ks, kw = [kn[i] * cosn + lane_map(kn[i], swap_half) * sinn for i in range(3)]
    lane_kv = lax.broadcasted_iota(jnp.int32, (tm // CMP_STRIDE, N_KVW), 1)
    for a, arr in enumerate((kc, vc)):
        regroup_ref[...] = arr
        for l in range(0, CMP_STRIDE, 2):
            even = regroup_ref[pl.ds(l, tm // CMP_STRIDE, stride=CMP_STRIDE), :]
            odd = regroup_ref[pl.ds(l + 1, tm // CMP_STRIDE, stride=CMP_STRIDE), :]
            cols = slice((l // 2) * N_KVW, (l // 2 + 1) * N_KVW)
            kvb_ref[a * N_KV, :, cols] = jnp.where(lane_kv < N_DH, even, pltpu.roll(odd, N_DH, 1))
            kvb_ref[a * N_KV + 1, :, cols] = jnp.where(lane_kv < N_DH, pltpu.roll(even, N_DH, 1), odd)
    ks_ref[...] = ks.astype(BF16)
    kw_ref[...] = kw.astype(BF16)
    vsT = fm[N_Q:N_Q + N_KVW, :].astype(BF16)
    vwT = fm[N_Q + N_KVW:, :].astype(BF16)
    for j in range(tm // LANES):
        vsT_ref[j] = vsT[:, j * LANES:(j + 1) * LANES]
        vwT_ref[j] = vwT[:, j * LANES:(j + 1) * LANES]


def _proj_kernel(x_ref, n1w_ref, *refs, tiles_per_seq):
    tm = x_ref.shape[0]
    sub = SUB_ROWS
    hns = [_rmsnorm_rows(x_ref[r * sub:(r + 1) * sub, :], n1w_ref[...]).astype(BF16) for r in range(tm // sub)]
    m_in, n_in, m_out, n_out, (buf_ref, regroup_ref) = (refs[0:5], refs[5:13], refs[13:19], refs[19:25], refs[25:27])
    _proj_m_body(hns, *m_in, *m_out, buf_ref, tiles_per_seq)
    _proj_n_body(jnp.concatenate(hns, axis=0), *n_in, *n_out, regroup_ref)


def _proj(x2, n1w, m_args, n_args, seq, tm):
    t = x2.shape[0]
    tps = seq // tm
    half = N_DH // 2
    const = lambda i: (0, 0)
    kern = functools.partial(_proj_kernel, tiles_per_seq=tps)
    return pl.pallas_call(
        kern,
        grid=(t // tm,),
        in_specs=[
            pl.BlockSpec((tm, D_MODEL), lambda i: (i, 0)),
            pl.BlockSpec((1, D_MODEL), const),
            pl.BlockSpec(m_args[0].shape, const),
            pl.BlockSpec((M_V + 32, D_MODEL), const),
            pl.BlockSpec((M_CONV, 2 * M_QK), const),
            pl.BlockSpec((1, 2 * M_QK), const),
            pl.BlockSpec((32, 1), const),
            pl.BlockSpec(n_args[0].shape, const),
            pl.BlockSpec((N_Q + 2 * N_KVW, D_MODEL), lambda i: (0, 0)),
            pl.BlockSpec((N_DH, 1), lambda i: (0, 0)),
            pl.BlockSpec((8, N_KVW), lambda i: (0, 0)),
            pl.BlockSpec((tm, N_KVW), lambda i: (i % tps, 0)),
            pl.BlockSpec((tm, N_KVW), lambda i: (i % tps, 0)),
            pl.BlockSpec((half, tm), lambda i: (0, i % tps)),
            pl.BlockSpec((half, tm), lambda i: (0, i % tps)),
        ],
        out_specs=[
            pl.BlockSpec((M_QK, tm), lambda i: (0, i)),
            pl.BlockSpec((tm, M_QK), lambda i: (i, 0)),
            pl.BlockSpec((M_V, tm), lambda i: (0, i)),
            pl.BlockSpec((tm, M_V), lambda i: (i, 0)),
            pl.BlockSpec((tm, LANES), lambda i: (i, 0)),
            pl.BlockSpec((32, tm), lambda i: (0, i)),
            pl.BlockSpec((N_Q, tm), lambda i: (0, i)),
            pl.BlockSpec((2 * N_KV, tm // CMP_STRIDE, CMP_STRIDE * N_DH), lambda i: (0, i, 0)),
            pl.BlockSpec((tm, N_KVW), lambda i: (i, 0)),
            pl.BlockSpec((tm, N_KVW), lambda i: (i, 0)),
            pl.BlockSpec((tm // LANES, N_KVW, LANES), lambda i: (i, 0, 0)),
            pl.BlockSpec((tm // LANES, N_KVW, LANES), lambda i: (i, 0, 0)),
        ],
        out_shape=[
            jax.ShapeDtypeStruct((M_QK, t), BF16),
            jax.ShapeDtypeStruct((t, M_QK), BF16),
            jax.ShapeDtypeStruct((M_V, t), BF16),
            jax.ShapeDtypeStruct((t, M_V), F32),
            jax.ShapeDtypeStruct((t, LANES), F32),
            jax.ShapeDtypeStruct((32, t), F32),
            jax.ShapeDtypeStruct((N_Q, t), BF16),
            jax.ShapeDtypeStruct((2 * N_KV, t // CMP_STRIDE, CMP_STRIDE * N_DH), F32),
            jax.ShapeDtypeStruct((t, N_KVW), BF16),
            jax.ShapeDtypeStruct((t, N_KVW), BF16),
            jax.ShapeDtypeStruct((t // LANES, N_KVW, LANES), BF16),
            jax.ShapeDtypeStruct((t // LANES, N_KVW, LANES), BF16),
        ],
        scratch_shapes=[pltpu.VMEM((SUB_ROWS + 8, 2 * M_QK), F32), pltpu.VMEM((tm, N_KVW), F32)],
        compiler_params=_params(("arbitrary",)),
        name="proj",
    )(x2, n1w, *m_args, *n_args)


def _compress_kernel(x_ref, pe_ref, w1_ref, w2_ref, kcmp_ref, vcmpT_ref):
    nb = x_ref.shape[2]
    half = (CMP_LEN // 2) * N_DH

    def mlp(a, kv):
        pe = pe_ref[kv]
        x = x_ref[a, 0]
        first = _dot((x + pe[:, 0:half]).astype(BF16), w1_ref[kv, 0:half, :])
        second = _dot((x + pe[:, half:]).astype(BF16), w1_ref[kv, half:, :])
        hid = first + pltpu.roll(second, nb - 1, 0)
        return _dot(_gelu(hid).astype(BF16), w2_ref[kv])

    kcmp_ref[...] = jnp.concatenate([mlp(g, 0) for g in range(N_KV)], axis=1).astype(BF16)
    vcmpT_ref[...] = jnp.concatenate([mlp(N_KV + g, 1) for g in range(N_KV)], axis=1).T.astype(BF16)


def _compress(xkv, pe, w1, w2):
    na, batch, nb, width = xkv.shape
    const3 = lambda b: (0, 0, 0)
    return pl.pallas_call(
        _compress_kernel,
        grid=(batch,),
        in_specs=[
            pl.BlockSpec((na, 1, nb, width), lambda b: (0, b, 0, 0)),
            pl.BlockSpec((2, 1, CMP_LEN * N_DH), const3),
            pl.BlockSpec((2, CMP_LEN * N_DH, CMP_HIDDEN), const3),
            pl.BlockSpec((2, CMP_HIDDEN, N_DH), const3),
        ],
        out_specs=[
            pl.BlockSpec((None, nb, N_KVW), lambda b: (b, 0, 0)),
            pl.BlockSpec((None, N_KVW, nb), lambda b: (b, 0, 0)),
        ],
        out_shape=[
            jax.ShapeDtypeStruct((batch, nb, N_KVW), BF16),
            jax.ShapeDtypeStruct((batch, N_KVW, nb), BF16),
        ],
        compiler_params=_params(("arbitrary",)),
        name="compress",
    )(xkv, pe, w1, w2)


def _mlstm_kernel(qT_ref, k_ref, vT_ref, small_ref, smallT_ref, og_ref, onw_ref, o_ref, c_ref, m_ref):
    L = MLSTM_CHUNK
    tb = k_ref.shape[0]

    @pl.when(pl.program_id(1) == 0)
    def _():
        c_ref[...] = jnp.zeros(c_ref.shape, F32)
        m_ref[...] = jnp.zeros(m_ref.shape, F32)

    row = lax.broadcasted_iota(jnp.int32, (L, L), 0)
    col = lax.broadcasted_iota(jnp.int32, (L, L), 1)
    causal = row <= col
    tril = (col <= row).astype(F32)
    triu = causal.astype(F32)
    ones_rows = jnp.ones((16, L), BF16)
    zeros_q = jnp.zeros((M_DQK, L), BF16)

    heads = range(M_HEADS)
    chunks = range(tb // L)
    sls = [slice(c * L, (c + 1) * L) for c in chunks]
    sms = [small_ref[sl, :] for sl in sls]
    bcols = [jnp.dot(tril, sm, preferred_element_type=F32, precision=HIGHEST) for sm in sms]
    brows = [jnp.dot(smallT_ref[0:8, sl], triu, preferred_element_type=F32, precision=HIGHEST) for sl in sls]
    c_state = [c_ref[h] for h in heads]
    m_state = [m_ref[h:h + 1, 0:1] for h in heads]
    for c in chunks:
        sl = sls[c]
        k_pairs = [k_ref[sl, p * LANES:(p + 1) * LANES] for p in range(M_HEADS // 2)]
        qT_pad, s, qc, vT_aug = [], [], [], []
        for h in heads:
            qT_h = qT_ref[h * M_DQK:(h + 1) * M_DQK, sl]
            qT_pad.append(jnp.concatenate([qT_h, zeros_q] if h % 2 == 0 else [zeros_q, qT_h], axis=0))
            s.append(_dot(k_pairs[h // 2], qT_pad[h]))
            qc.append(_dot(c_state[h].astype(BF16), qT_pad[h]))
            vT_aug.append(jnp.concatenate([vT_ref[h * M_DV:(h + 1) * M_DV, sl], ones_rows], axis=0))
        m_t, isc, sw, kw, m_new, decay, scale = [], [], [], [], [], [], []
        for h in heads:
            in_head = (col // M_DQK) == (h % 2)
            b_col = bcols[c][:, M_HEADS + h:M_HEADS + h + 1]
            i_col = sms[c][:, h:h + 1]
            b_row = brows[c][M_HEADS + h:M_HEADS + h + 1, :]
            g = b_row[:, L - 1:L]
            d = jnp.where(causal, b_row + (i_col - b_col), NEG)
            inter = b_row + m_state[h]
            m_t.append(jnp.maximum(inter, jnp.max(d, axis=0, keepdims=True)))
            isc.append(jnp.exp(inter - m_t[h]))
            sw.append((s[h] * jnp.exp(d - m_t[h])).astype(BF16))
            a_col = g - b_col + i_col
            a_max = jnp.max(a_col, axis=0, keepdims=True)
            kw.append(jnp.where(in_head, k_pairs[h // 2].astype(F32) * jnp.exp(a_col - a_max), 0.0).astype(BF16))
            m_new.append(jnp.maximum(g + m_state[h], a_max))
            decay.append(jnp.exp(g + m_state[h] - m_new[h]))
            scale.append(jnp.exp(a_max - m_new[h]))
        sv = [_dot(vT_aug[h], sw[h]) for h in heads]
        dc = [_dot(vT_aug[h], kw[h]) for h in heads]
        for h in heads:
            nd = isc[h] * qc[h] + sv[h]
            den = nd[M_DV:M_DV + 1, :]
            hh = nd[0:M_DV, :] / jnp.maximum(jnp.abs(den), jnp.exp(-m_t[h]))
            hn = hh * lax.rsqrt(jnp.mean(hh * hh, axis=0, keepdims=True) + EPS)
            o_ref[sl, h * M_DV:(h + 1) * M_DV] = hn.T * onw_ref[h:h + 1, :] * og_ref[sl, h * M_DV:(h + 1) * M_DV]
            c_state[h] = decay[h] * c_state[h] + scale[h] * dc[h]
            m_state[h] = m_new[h]
    for h in heads:
        c_ref[h] = c_state[h]
        m_ref[h:h + 1, :] = jnp.broadcast_to(m_state[h], (1, LANES))


def _mlstm(qT, k, vT, small, smallT, og, onw, batch, seq, tb):
    t = k.shape[0]
    nblk = seq // tb
    return pl.pallas_call(
        _mlstm_kernel,
        grid=(batch, nblk),
        in_specs=[
            pl.BlockSpec((M_QK, tb), lambda b, j: (0, b * nblk + j)),
            pl.BlockSpec((tb, M_QK), lambda b, j: (b * nblk + j, 0)),
            pl.BlockSpec((M_V, tb), lambda b, j: (0, b * nblk + j)),
            pl.BlockSpec((tb, LANES), lambda b, j: (b * nblk + j, 0)),
            pl.BlockSpec((32, tb), lambda b, j: (0, b * nblk + j)),
            pl.BlockSpec((tb, M_V), lambda b, j: (b * nblk + j, 0)),
            pl.BlockSpec((8, M_DV), lambda b, j: (0, 0)),
        ],
        out_specs=pl.BlockSpec((tb, M_V), lambda b, j: (b * nblk + j, 0)),
        out_shape=jax.ShapeDtypeStruct((t, M_V), F32),
        scratch_shapes=[pltpu.VMEM((M_HEADS, M_DV + 16, LANES), F32), pltpu.VMEM((8, LANES), F32)],
        compiler_params=_params(("arbitrary", "arbitrary")),
        name="mlstm",
    )(qT, k, vT, small, smallT, og, onw)


def _nsa_kernel(qT_ref, kcmp_ref, vcmpT_ref, ks_ref, e_ref, vsT_ref, kw_ref, vwT_ref, gT_ref,
                o_ref, rhs_ref, ps_ref, ocmp_ref, owin_ref, s0_ref, s1_ref, c0_ref, c1_ref, m_ref, acc_ref,
                *, nsel, n_rounds):
    qi = pl.program_id(1)
    tq = Q_TILE
    nb = kcmp_ref.shape[0]
    nselp = e_ref.shape[1]
    q0 = qi * tq
    wide = N_HG * tq
    lane_w = lax.broadcasted_iota(jnp.int32, (1, wide), 1)
    tpos_w = q0 + (lane_w % tq)
    tpos = q0 + lax.broadcasted_iota(jnp.int32, (1, tq), 1)
    zeros_q = jnp.zeros((N_DH, wide), BF16)
    ones_v = jnp.ones((16, LANES), BF16)
    grows = [slice(g * N_DH, (g + 1) * N_DH) for g in range(N_KV)]
    gcols = [slice(g * wide, (g + 1) * wide) for g in range(N_KV)]

    def values(v_ref, first, count, rows):
        return jnp.concatenate(
            [jnp.concatenate([v_ref[first + j, rows, :], ones_v], axis=0) for j in range(count)], axis=1)

    start_w = pl.multiple_of(jnp.maximum(q0 - WINDOW, 0), LANES)
    qpads = []
    for g in range(N_KV):
        q4 = jnp.concatenate(
            [qT_ref[(g * N_HG + h) * N_DH:(g * N_HG + h + 1) * N_DH, :] for h in range(N_HG)], axis=1)
        qpads.append(jnp.concatenate([q4, zeros_q] if g == 0 else [zeros_q, q4], axis=0))
        rhs_ref[0:2 * N_DH, gcols[g]] = qpads[g]

    def scores(kt, g):
        start = pl.multiple_of(kt * KEY_TILE, KEY_TILE)
        lhs = jnp.concatenate([ks_ref[pl.ds(start, KEY_TILE), :], e_ref[pl.ds(start, KEY_TILE), :]], axis=1)
        return _dot(lhs, rhs_ref[:, g * wide:(g + 1) * wide])

    def produce(kt, s_ref, c_ref):
        for g in range(N_KV):
            s = scores(kt, g)
            s_ref[g] = s
            c_ref[g] = jnp.max(s, axis=0, keepdims=True)

    def consume(kt, s_ref, c_ref, causal_rows=0):
        for g in range(N_KV):
            if causal_rows:
                kpos = kt * KEY_TILE + lax.broadcasted_iota(jnp.int32, (causal_rows, 1), 0)
                s = jnp.where(kpos <= tpos_w, s_ref[g, 0:causal_rows, :], NEG)
                smax = jnp.max(s, axis=0, keepdims=True)
            else:
                s = s_ref[g]
                smax = c_ref[g]
            m = m_ref[g]
            m_new = jnp.maximum(m, smax)
            p = jnp.exp2(s - m_new).astype(BF16)
            vt = values(vsT_ref, kt * (KEY_TILE // LANES), (causal_rows or KEY_TILE) // LANES,
                        slice(g * N_DH, (g + 1) * N_DH))
            acc_ref[g] = jnp.exp2(m - m_new) * acc_ref[g] + _dot(vt, p)
            m_ref[g] = m_new

    ratio = SEL_BLOCK // CMP_STRIDE

    def front(nrows, first_q0):
        nbv = nrows * ratio
        free = max((first_q0 - (CMP_LEN - 1)) // CMP_STRIDE + 1, 0) // 8 * 8
        cend = (free + lax.broadcasted_iota(jnp.int32, (nbv - free, 1), 0)) * CMP_STRIDE + (CMP_LEN - 1)
        cmask = cend <= tpos_w
        any_visible = tpos_w >= CMP_LEN - 1
        sc = [_dot(kcmp_ref[0:nbv, :], qpads[g]) for g in range(N_KV)]
        sw = [_dot(kw_ref[pl.ds(start_w, WIN_SPAN), :], qpads[g]) for g in range(N_KV)]
        imp = []
        for g in range(N_KV):
            s = jnp.where(cmask, sc[g][free:, :], NEG)
            if free:
                s = jnp.concatenate([sc[g][0:free, :], s], axis=0)
            pc = jnp.exp2(s - jnp.max(s, axis=0, keepdims=True))
            lc = jnp.sum(pc, axis=0, keepdims=True)
            pc = pc * jnp.where(any_visible, 1.0 / lc, 0.0)
            ocmp_ref[g] = _dot(vcmpT_ref[grows[g], 0:nbv], pc.astype(BF16))
            psum = pc[:, 0:tq]
            for h in range(1, N_HG):
                psum = psum + pc[:, h * tq:(h + 1) * tq]
            parts = []
            for cchunk in range(tq // LANES):
                ps_ref[g, cchunk, 0:8, :] = jnp.zeros((8, LANES), F32)
                ps_ref[g, cchunk, 8:8 + nbv, :] = psum[:, cchunk * LANES:(cchunk + 1) * LANES]
                acc = None
                for k in range(-((CMP_LEN - 1) // CMP_STRIDE), ratio):
                    part = ps_ref[g, cchunk, pl.ds(8 + k, nrows, stride=ratio), :]
                    acc = part if acc is None else acc + part
                parts.append(acc)
            imp.append(jnp.concatenate(parts, axis=1))

        jblk = lax.broadcasted_iota(jnp.int32, (nrows, tq), 0)
        cur = tpos // SEL_BLOCK
        forced = (jblk == 0) | (jblk == cur) | (jblk == cur - 1)
        cand = (jblk >= 1) & (jblk <= cur - 2)
        jblk_f = jblk.astype(F32)
        val = [jnp.where(cand, imp[g], -jnp.inf) for g in range(N_KV)]
        for _ in range(n_rounds):
            for g in range(N_KV):
                mx = jnp.max(val[g], axis=0, keepdims=True)
                first = jnp.min(jnp.where(val[g] == mx, jblk_f, float(nrows)), axis=0, keepdims=True)
                val[g] = jnp.where(jblk_f == first, -jnp.inf, val[g])
        for g in range(N_KV):
            picked = cand & (val[g] == -jnp.inf)
            bias = jnp.where(forced | picked, 0.0, NEG).astype(BF16)
            if nselp > nrows:
                bias = jnp.concatenate([bias, jnp.zeros((nselp - nrows, tq), BF16)], axis=0)
            rhs_ref[2 * N_DH:, gcols[g]] = jnp.concatenate([bias] * N_HG, axis=1)

        produce(0, s0_ref, c0_ref)

        def window_mask(s, r0, r1):
            dist = (tpos_w - start_w) - (r0 + lax.broadcasted_iota(jnp.int32, (r1 - r0, 1), 0))
            return jnp.where(lax.bitcast_convert_type(dist, jnp.uint32) < WINDOW, s[r0:r1, :], NEG)

        for g in range(N_KV):
            if first_q0 >= WINDOW:
                s = jnp.concatenate([window_mask(sw[g], 0, tq), sw[g][tq:WINDOW, :],
                                     window_mask(sw[g], WINDOW, WIN_SPAN)], axis=0)
            else:
                s = window_mask(sw[g], 0, WIN_SPAN)
            pw = jnp.exp2(s - jnp.max(s, axis=0, keepdims=True))
            ow = _dot(values(vwT_ref, start_w // LANES, WIN_SPAN // LANES, grows[g]), pw.astype(BF16))
            owin_ref[g] = ow[0:N_DH, :] * (1.0 / ow[N_DH:N_DH + 1, :])

    n_full = qi // (KEY_TILE // tq)
    odd = n_full % 2
    m_ref[...] = jnp.full(m_ref.shape, NEG, F32)
    acc_ref[...] = jnp.zeros(acc_ref.shape, F32)
    n_var = 4 if nsel % 32 == 0 else 1
    if n_var == 1:
        front(nsel, 0)
    else:
        tiles_per_var = (nsel * SEL_BLOCK // tq) // n_var
        for v in range(n_var):
            pl.when(qi // tiles_per_var == v)(
                functools.partial(front, (v + 1) * nsel // n_var, v * tiles_per_var * tq))

    def pair(kt):
        produce(kt + 1, s1_ref, c1_ref)
        consume(kt, s0_ref, c0_ref)
        produce(kt + 2, s0_ref, c0_ref)
        consume(kt + 1, s1_ref, c1_ref)

    def octet(j, _):
        for i in range(0, 8, 2):
            pair(8 * j + i)
        return 0

    n_oct = n_full // 8
    lax.fori_loop(0, n_oct, octet, 0)
    done = 8 * n_oct

    @pl.when(n_full - done >= 4)
    def _():
        pair(done)
        pair(done + 2)

    done = done + jnp.where(n_full - done >= 4, 4, 0)

    @pl.when(n_full - done >= 2)
    def _():
        pair(done)

    for at_start in (False, True):
        rows = tq if at_start else KEY_TILE
        starts_tile = (q0 % KEY_TILE == 0) == at_start

        @pl.when((odd == 0) & starts_tile)
        def _():
            consume(n_full, s0_ref, c0_ref, causal_rows=rows)

        @pl.when((odd == 1) & starts_tile)
        def _():
            produce(n_full, s1_ref, c1_ref)
            consume(n_full - 1, s0_ref, c0_ref)
            consume(n_full, s1_ref, c1_ref, causal_rows=rows)

    outs = []
    for g in range(N_KV):
        o_slc = acc_ref[g, 0:N_DH, :] * (1.0 / acc_ref[g, N_DH:N_DH + 1, :])
        for h in range(N_HG):
            cs = slice(h * tq, (h + 1) * tq)
            r = 2 * M_HEADS + (g * N_HG + h) * 3
            outs.append(gT_ref[r:r + 1, :] * ocmp_ref[g, :, cs] + gT_ref[r + 1:r + 2, :] * o_slc[:, cs]
                        + gT_ref[r + 2:r + 3, :] * owin_ref[g, :, cs])

    o_ref[...] = jnp.concatenate(outs, axis=0).T


def _nsa(qT, kcmp, vcmpT, ks, emap, vsT, kw, vwT, smallT, batch, seq):
    t = qT.shape[1]
    nq = seq // Q_TILE
    nb = kcmp.shape[1]
    nsel = seq // SEL_BLOCK
    nselp = emap.shape[1]
    wide = N_HG * Q_TILE
    n_rounds = max(min(SEL_TOPN, nsel) - 3, 0)
    kern = functools.partial(_nsa_kernel, nsel=nsel, n_rounds=n_rounds)
    return pl.pallas_call(
        kern,
        grid=(batch, nq),
        in_specs=[
            pl.BlockSpec((N_Q, Q_TILE), lambda b, i: (0, b * nq + i)),
            pl.BlockSpec((None, nb, N_KVW), lambda b, i: (b, 0, 0)),
            pl.BlockSpec((None, N_KVW, nb), lambda b, i: (b, 0, 0)),
            pl.BlockSpec((seq, N_KVW), lambda b, i: (b, 0)),
            pl.BlockSpec((seq, nselp), lambda b, i: (0, 0)),
            pl.BlockSpec((seq // LANES, N_KVW, LANES), lambda b, i: (b, 0, 0)),
            pl.BlockSpec((seq, N_KVW), lambda b, i: (b, 0)),
            pl.BlockSpec((seq // LANES, N_KVW, LANES), lambda b, i: (b, 0, 0)),
            pl.BlockSpec((32, Q_TILE), lambda b, i: (0, b * nq + i)),
        ],
        out_specs=pl.BlockSpec((Q_TILE, N_Q), lambda b, i: (b * nq + i, 0)),
        out_shape=jax.ShapeDtypeStruct((t, N_Q), F32),
        scratch_shapes=[
            pltpu.VMEM((2 * N_DH + nselp, N_KV * wide), BF16),
            pltpu.VMEM((N_KV, Q_TILE // LANES, nb + 8, LANES), F32),
            pltpu.VMEM((N_KV, N_DH, wide), F32),
            pltpu.VMEM((N_KV, N_DH, wide), F32),
            pltpu.VMEM((N_KV, KEY_TILE, wide), F32),
            pltpu.VMEM((N_KV, KEY_TILE, wide), F32),
            pltpu.VMEM((N_KV, 1, wide), F32),
            pltpu.VMEM((N_KV, 1, wide), F32),
            pltpu.VMEM((N_KV, 1, wide), F32),
            pltpu.VMEM((N_KV, N_DH + 16, wide), F32),
        ],
        compiler_params=_params(("arbitrary", "arbitrary")),
        name="nsa",
    )(qT, kcmp, vcmpT, ks, emap, vsT, kw, vwT, smallT)


def _merge_kernel(x_ref, n1w_ref, wg_ref, gb_ref, hm_ref, on_ref, wm_ref, wn_ref, wo_ref, o_ref):
    x = x_ref[...]
    hn = _rmsnorm_rows(x, n1w_ref[...]).astype(BF16)
    gm = _sigmoid(_dot(hn, wg_ref[:, 0:D_MODEL]) + gb_ref[0:1, :])
    gn = _sigmoid(_dot(hn, wg_ref[:, D_MODEL:]) + gb_ref[1:2, :])
    y = gm * _dot(hm_ref[...].astype(BF16), wm_ref[...]) + gn * _dot(on_ref[...].astype(BF16), wn_ref[...])
    o_ref[...] = x + _dot(y.astype(BF16), wo_ref[...])


def _merge(x2, n1w, wg, gb, hm, on, wm, wn, wo, tm):
    t = x2.shape[0]
    const = lambda i: (0, 0)
    return pl.pallas_call(
        _merge_kernel,
        grid=(t // tm,),
        in_specs=[
            pl.BlockSpec((tm, D_MODEL), lambda i: (i, 0)),
            pl.BlockSpec((1, D_MODEL), const),
            pl.BlockSpec((D_MODEL, 2 * D_MODEL), const),
            pl.BlockSpec((2, D_MODEL), const),
            pl.BlockSpec((tm, M_V), lambda i: (i, 0)),
            pl.BlockSpec((tm, N_Q), lambda i: (i, 0)),
            pl.BlockSpec((M_V, D_MODEL), const),
            pl.BlockSpec((N_Q, D_MODEL), const),
            pl.BlockSpec((D_MODEL, D_MODEL), const),
        ],
        out_specs=pl.BlockSpec((tm, D_MODEL), lambda i: (i, 0)),
        out_shape=jax.ShapeDtypeStruct((t, D_MODEL), F32),
        compiler_params=_params(("arbitrary",)),
        name="merge",
    )(x2, n1w, wg, gb, hm, on, wm, wn, wo)


def _ffn_kernel(x_ref, n2w_ref, wup_ref, cw_ref, cb_ref, wdn_ref, o_ref, buf_ref, *, tiles_per_seq):
    i = pl.program_id(0)
    tm = x_ref.shape[0]
    x = x_ref[...]
    hn = _rmsnorm_rows(x, n2w_ref[...]).astype(BF16)

    @pl.when(i % tiles_per_seq == 0)
    def _():
        buf_ref[0:8, :] = jnp.zeros((8, D_FF), F32)

    a = _dot(hn, wup_ref[:, 0:D_FF])
    buf_ref[8:8 + tm, :] = a
    acc = cb_ref[...] + cw_ref[FFN_CONV - 1:FFN_CONV, :] * a
    for k in range(FFN_CONV - 1):
        acc = acc + cw_ref[k:k + 1, :] * buf_ref[8 - (FFN_CONV - 1) + k:8 - (FFN_CONV - 1) + k + tm, :]
    buf_ref[0:8, :] = buf_ref[tm:tm + 8, :]
    v = _dot(hn, wup_ref[:, D_FF:])
    o_ref[...] = x + _dot((_gelu(acc) * v).astype(BF16), wdn_ref[...])


def _ffn(x2, n2w, wup, cw, cb, wdn, seq, tm):
    t = x2.shape[0]
    const = lambda i: (0, 0)
    kern = functools.partial(_ffn_kernel, tiles_per_seq=seq // tm)
    return pl.pallas_call(
        kern,
        grid=(t // tm,),
        in_specs=[
            pl.BlockSpec((tm, D_MODEL), lambda i: (i, 0)),
            pl.BlockSpec((1, D_MODEL), const),
            pl.BlockSpec((D_MODEL, 2 * D_FF), const, pipeline_mode=pl.Buffered(1)),
            pl.BlockSpec((FFN_CONV, D_FF), const),
            pl.BlockSpec((1, D_FF), const),
            pl.BlockSpec((D_FF, D_MODEL), const, pipeline_mode=pl.Buffered(1)),
        ],
        out_specs=pl.BlockSpec((tm, D_MODEL), lambda i: (i, 0)),
        out_shape=jax.ShapeDtypeStruct((t, D_MODEL), F32),
        scratch_shapes=[pltpu.VMEM((tm + 8, D_FF), F32)],
        compiler_params=_params(("arbitrary",)),
        name="ffn",
    )(x2, n2w, wup, cw, cb, wdn)


def _cols(w, *names):
    return jnp.concatenate([w[:, _OFF[n][0]:_OFF[n][1]] for n in names], axis=1)


def _layer(x, n1w, w_in, m_conv_w, m_conv_b, m_igate_b, m_fgate_b, m_out_norm_w,
           q_norm_w, kcmp_norm_w, kslc_norm_w, kwin_norm_w,
           cmp_k_pe, cmp_k_w1, cmp_k_w2, cmp_v_pe, cmp_v_w1, cmp_v_w2,
           w_up_m, w_up_n, merge_gate_b, w_out, norm2_w, ffn_w_up, ffn_conv_w, ffn_conv_b, ffn_w_down):
    batch, seq, _ = x.shape
    t = batch * seq
    x2 = x.reshape(t, D_MODEL)
    n1w2 = n1w.reshape(1, D_MODEL)
    tm_proj = 512

    w_m = _cols(w_in, "mq", "mk", "mo").astype(BF16)
    w_mvT = _cols(w_in, "mv", "mi", "mf", "ng").T.astype(BF16)
    w_n = _cols(w_in, "kc", "ks", "kw", "vc").astype(BF16)
    w_nT = _cols(w_in, "nq", "vs", "vw").T.astype(BF16)
    w_g = _cols(w_in, "gm", "gn").astype(BF16)
    sbias = jnp.concatenate([m_igate_b, m_fgate_b, jnp.zeros((3 * N_HEADS,), F32)]).reshape(32, 1)

    half = N_DH // 2
    pos = jnp.arange(seq, dtype=F32)
    inv = ROPE_THETA ** (-jnp.arange(0, N_DH, 2, dtype=F32) / N_DH)
    ang = pos[:, None] * inv[None, :]
    cos, sin = jnp.cos(ang), jnp.sin(ang)
    cosn = jnp.tile(cos, (1, N_KVW // half))
    sinn = jnp.tile(jnp.concatenate([-sin, sin], axis=1), (1, N_KV))
    cosT, sinT = cos.T, sin.T
    knw = jnp.zeros((8, N_KVW), F32).at[0:3].set(
        jnp.stack([jnp.tile(w, N_KV) for w in (kcmp_norm_w, kslc_norm_w, kwin_norm_w)]))

    qT_m, k_m, vT_m, og, small, smallT, qT, kvb, ks, kw, vsT, vwT = _proj(
        x2, n1w2, (w_m, w_mvT, m_conv_w, m_conv_b.reshape(1, -1), sbias),
        (w_n, w_nT, q_norm_w.reshape(N_DH, 1), knw, cosn, sinn, cosT, sinT), seq, tm_proj)

    nb = seq // CMP_STRIDE
    xkv = kvb.reshape(2 * N_KV, batch, nb, CMP_STRIDE * N_DH)
    pe = jnp.stack([cmp_k_pe.reshape(1, -1), cmp_v_pe.reshape(1, -1)])
    w1 = jnp.stack([cmp_k_w1, cmp_v_w1]).astype(BF16)
    w2 = jnp.stack([cmp_k_w2, cmp_v_w2]).astype(BF16)
    kcmp, vcmpT = _compress(xkv, pe, w1, w2)

    onw = jnp.zeros((8, M_DV), F32).at[0:M_HEADS].set(m_out_norm_w)
    hm = _mlstm(qT_m, k_m, vT_m, small, smallT, og, onw, batch, seq, 4 * MLSTM_CHUNK)
    nselp = -(-(seq // SEL_BLOCK) // LANES) * LANES
    emap = (np.arange(seq)[:, None] // SEL_BLOCK == np.arange(nselp)[None, :]).astype(np.float32)
    on = _nsa(qT, kcmp, vcmpT, ks, jnp.asarray(emap, dtype=BF16), vsT, kw, vwT, smallT, batch, seq)

    x1 = _merge(x2, n1w2, w_g, merge_gate_b, hm, on, w_up_m.astype(BF16), w_up_n.astype(BF16),
                w_out.astype(BF16), tm_proj)
    out = _ffn(x1, norm2_w.reshape(1, D_MODEL), ffn_w_up.astype(BF16), ffn_conv_w, ffn_conv_b.reshape(1, -1),
               ffn_w_down.astype(BF16), seq, tm_proj)
    return out.reshape(batch, seq, D_MODEL)


def kernel(x, norm1_w, w_in, m_conv_w, m_conv_b, m_igate_b, m_fgate_b, m_out_norm_w, q_norm_w, kcmp_norm_w,
           kslc_norm_w, kwin_norm_w, cmp_k_pe, cmp_k_w1, cmp_k_w2, cmp_v_pe, cmp_v_w1, cmp_v_w2, w_up_m, w_up_n,
           merge_gate_b, w_out, norm2_w, ffn_w_up, ffn_conv_w, ffn_conv_b, ffn_w_down):
    params = (norm1_w, w_in, m_conv_w, m_conv_b, m_igate_b, m_fgate_b, m_out_norm_w, q_norm_w, kcmp_norm_w,
              kslc_norm_w, kwin_norm_w, cmp_k_pe, cmp_k_w1, cmp_k_w2, cmp_v_pe, cmp_v_w1, cmp_v_w2, w_up_m, w_up_n,
              merge_gate_b, w_out, norm2_w, ffn_w_up, ffn_conv_w, ffn_conv_b, ffn_w_down)
    for layer in range(norm1_w.shape[0]):
        x = _layer(x, *[p[layer] for p in params])
    return x
```
